```python
import jax, jax.numpy as jnp
from jax import lax
import numpy as np

D_MODEL = 1024
BATCH = 4
SEQ = 8192
DEPTH = 4

CTX_LEN = 256
GRID_W = 64
EPS = 1e-6

FOURIER_GROUPS = 4
FOURIER_GD = 64
FOURIER_W = FOURIER_GROUPS * FOURIER_GD
ATT_HEADS = 8
ATT_KV_HEADS = 2
ATT_GROUP = ATT_HEADS // ATT_KV_HEADS
HEAD_DIM = 64
ATT_W = ATT_HEADS * HEAD_DIM
ATT_KV_W = ATT_KV_HEADS * HEAD_DIM
HG_HEADS = 4
HG_DK = 64
HG_DV = 64
HG_W = HG_HEADS * HG_DK
MIX_W = FOURIER_W + ATT_W + HG_W

OFF_FOURIER = 0
OFF_Q = OFF_FOURIER + FOURIER_W
OFF_K = OFF_Q + ATT_W
OFF_V = OFF_K + ATT_KV_W
OFF_FF = OFF_V + ATT_KV_W
OFF_FB = OFF_FF + HG_W
OFF_I = OFF_FB + HG_W
OFF_HQ = OFF_I + HG_W
OFF_G = OFF_HQ + HG_W
PROJ_W = OFF_G + HG_W

D_FF = -(-8 * D_MODEL // (3 * 256)) * 256
ROPE_THETA = 10000.0
Q_BLOCK = 128
HG_CHUNK = 64

kernel_name = "hybrid_fourier_gqa_hgrn2_dit_prefix"


def rmsnorm(x, gain=None):
    xf = x.astype(jnp.float32)
    y = xf * lax.rsqrt(jnp.mean(xf * xf, axis=-1, keepdims=True) + EPS)
    if gain is not None:
        y = y * gain.astype(jnp.float32)
    return y.astype(x.dtype)


def col(u, off, width, base=0):
    return u[..., off - base: off - base + width]


def axial_rope_tables(n):
    rows = n // GRID_W
    row = jnp.repeat(jnp.arange(rows), GRID_W).astype(jnp.float32)
    colp = jnp.tile(jnp.arange(GRID_W), rows).astype(jnp.float32)
    half = HEAD_DIM // 2
    freqs = ROPE_THETA ** (-jnp.arange(0, half, 2, dtype=jnp.float32) / half)
    ang_r = row[:, None] * freqs
    ang_c = colp[:, None] * freqs
    return (jnp.cos(ang_r), jnp.sin(ang_r), jnp.cos(ang_c), jnp.sin(ang_c))


def rope_axis(x, cos, sin):
    x1, x2 = jnp.split(x, 2, axis=-1)
    return jnp.concatenate([x1 * cos - x2 * sin, x1 * sin + x2 * cos], axis=-1)


def apply_axial_rope(x, tabs):
    cr, sr, cc, sc = tabs
    xf = x.astype(jnp.float32)
    xr, xc = jnp.split(xf, 2, axis=-1)
    return jnp.concatenate([rope_axis(xr, cr, sr), rope_axis(xc, cc, sc)], axis=-1).astype(x.dtype)


def fourier_mix(u, w):
    b, n, _ = u.shape
    z = u.astype(jnp.float32).reshape(b, n, FOURIER_GROUPS, FOURIER_GD)
    z = jnp.fft.fftn(z, axes=(1, 3), norm="ortho").real
    return z.reshape(b, n, FOURIER_W).astype(u.dtype) @ w


def attend(q, k, v):
    s = jnp.einsum('bkgqd,bksd->bkgqs', q, k).astype(jnp.float32) * (HEAD_DIM ** -0.5)
    p = jax.nn.softmax(s, axis=-1).astype(v.dtype)
    return jnp.einsum('bkgqs,bksd->bkgqd', p, v)


def attend_blocked(q, k, v):
    b, kh, g, n, d = q.shape
    nb = n // Q_BLOCK
    qb = jnp.moveaxis(q.reshape(b, kh, g, nb, Q_BLOCK, d), 3, 0)
    ob = lax.map(lambda qi: attend(qi, k, v), qb)
    return jnp.moveaxis(ob, 0, 3).reshape(b, kh, g, n, d)


def q_heads(u, gain):
    b, n, _ = u.shape
    return rmsnorm(u.reshape(b, n, ATT_HEADS, HEAD_DIM), gain).transpose(0, 2, 1, 3)


def kv_heads(u, gain=None):
    b, n, _ = u.shape
    return rmsnorm(u.reshape(b, n, ATT_KV_HEADS, HEAD_DIM), gain).transpose(0, 2, 1, 3) if gain is not None \
        else u.reshape(b, n, ATT_KV_HEADS, HEAD_DIM).transpose(0, 2, 1, 3)


def group_q(qh):
    b, h, n, d = qh.shape
    return qh.reshape(b, ATT_KV_HEADS, ATT_GROUP, n, d)


def ungroup(o):
    b, kh, g, n, d = o.shape
    return o.transpose(0, 3, 1, 2, 4).reshape(b, n, kh * g * d)


def attention_group(ux, uc, base_c, q_gain, k_gain, tabs, ctx_out):
    qx = group_q(apply_axial_rope(q_heads(col(ux, OFF_Q, ATT_W), q_gain), tabs))
    kx = apply_axial_rope(kv_heads(col(ux, OFF_K, ATT_KV_W), k_gain), tabs)
    vx = kv_heads(col(ux, OFF_V, ATT_KV_W))
    kc = kv_heads(col(uc, OFF_K, ATT_KV_W, base_c), k_gain)
    vc = kv_heads(col(uc, OFF_V, ATT_KV_W, base_c))
    k_all = jnp.concatenate([kc, kx], axis=2)
    v_all = jnp.concatenate([vc, vx], axis=2)
    out_x = ungroup(attend_blocked(qx, k_all, v_all))
    out_c = None
    if ctx_out:
        qc = group_q(q_heads(col(uc, OFF_Q, ATT_W, base_c), q_gain))
        out_c = ungroup(attend(qc, kc, vc))
    return out_x, out_c


def hgrn_scan(k, v, log_f, s0, q=None):
    b, h, n, dk = k.shape
    nc = n // HG_CHUNK
    with_out = q is not None

    def chunks(a):
        return jnp.moveaxis(a.reshape(b, h, nc, HG_CHUNK, a.shape[-1]), 2, 0)

    mask = jnp.tril(jnp.ones((HG_CHUNK, HG_CHUNK), dtype=bool))[:, :, None]

    def step(s, inp):
        kc, vc, lf = inp[0], inp[1], inp[2]
        cum = jnp.cumsum(lf, axis=2)
        last = cum[:, :, -1:, :]
        s_new = jnp.exp(last[:, :, 0, :])[..., None] * s + \
            jnp.einsum('bhsd,bhse->bhde', kc * jnp.exp(last - cum), vc)
        if not with_out:
            return s_new, None
        qc = inp[3]
        o_inter = jnp.einsum('bhtd,bhde->bhte', qc * jnp.exp(cum), s)
        diff = cum[:, :, :, None, :] - cum[:, :, None, :, :]
        decay = jnp.exp(jnp.where(mask, diff, -jnp.inf))
        att = jnp.einsum('bhtd,bhsd,bhtsd->bhts', qc, kc, decay)
        return s_new, o_inter + jnp.einsum('bhts,bhse->bhte', att, vc)

    xs = (chunks(k), chunks(v), chunks(log_f)) + ((chunks(q),) if with_out else ())
    s_fin, o = lax.scan(step, s0, xs)
    if with_out:
        o = jnp.moveaxis(o, 0, 2).reshape(b, h, n, v.shape[-1])
    return o, s_fin


def hg_heads(u):
    b, n, _ = u.shape
    return u.reshape(b, n, HG_HEADS, -1).transpose(0, 2, 1, 3).astype(jnp.float32)


def hg_gate(z, lb):
    lb = lb.reshape(HG_HEADS, 1, HG_DK)
    k = (1.0 - lb) * jax.nn.sigmoid(-z)
    log_f = jnp.log(lb + (1.0 - lb) * jax.nn.sigmoid(z))
    return k, log_f


def hg_readout(o, g, gain, dtype):
    b, h, n, dv = o.shape
    y = rmsnorm(o, gain).transpose(0, 2, 1, 3).reshape(b, n, h * dv)
    return (y * jax.nn.silu(g.astype(jnp.float32))).astype(dtype)


def hgrn_group(ux, uc, base_c, lb_dirs, gain, ctx_out):
    b = ux.shape[0]
    vx = hg_heads(col(ux, OFF_I, HG_W))
    qx = hg_heads(col(ux, OFF_HQ, HG_W))
    vc = hg_heads(col(uc, OFF_I, HG_W, base_c))
    qc = hg_heads(col(uc, OFF_HQ, HG_W, base_c)) if ctx_out else None
    s0 = jnp.zeros((b, HG_HEADS, HG_DK, HG_DV), jnp.float32)
    o_x = jnp.zeros_like(vx)
    o_c = jnp.zeros_like(vc) if ctx_out else None
    for d, (off, rev) in enumerate(((OFF_FF, False), (OFF_FB, True))):
        fl = (lambda a: jnp.flip(a, axis=2)) if rev else (lambda a: a)
        kx, lfx = hg_gate(hg_heads(col(ux, off, HG_W)), lb_dirs[d])
        kc, lfc = hg_gate(hg_heads(col(uc, off, HG_W, base_c)), lb_dirs[d])
        oc, s_ctx = hgrn_scan(fl(kc), fl(vc), fl(lfc), s0, fl(qc) if ctx_out else None)
        ox, _ = hgrn_scan(fl(kx), fl(vx), fl(lfx), s_ctx, fl(qx))
        o_x = o_x + fl(ox)
        if ctx_out:
            o_c = o_c + fl(oc)
    out_x = hg_readout(o_x, col(ux, OFF_G, HG_W), gain, ux.dtype)
    out_c = hg_readout(o_c, col(uc, OFF_G, HG_W, base_c), gain, uc.dtype) if ctx_out else None
    return out_x, out_c


def swiglu(h, wg, wu, wd):
    return (jax.nn.silu(h @ wg) * (h @ wu)) @ wd


def setup_inputs(seed: int = 0) -> dict:
    key = jax.random.key(seed)
    ks = jax.random.split(key, 18)
    nrm = jax.random.normal
    f32 = jnp.float32
    return {
        "x": nrm(ks[0], (BATCH, SEQ, D_MODEL), f32),
        "c": nrm(ks[1], (BATCH, D_MODEL), f32),
        "ctx": nrm(ks[2], (BATCH, CTX_LEN, D_MODEL), f32),
        "c_ctx": nrm(ks[3], (D_MODEL,), f32),
        "w_ada": nrm(ks[4], (DEPTH, D_MODEL, 6 * D_MODEL), f32) * D_MODEL ** -0.5,
        "b_ada": nrm(ks[5], (DEPTH, 6 * D_MODEL), f32) * 0.02,
        "w_in": nrm(ks[6], (DEPTH, D_MODEL, PROJ_W), f32) * D_MODEL ** -0.5,
        "w_four": nrm(ks[7], (DEPTH, FOURIER_W, FOURIER_W), f32) * FOURIER_W ** -0.5,
        "q_norm": 1.0 + 0.02 * nrm(ks[8], (DEPTH, HEAD_DIM), f32),
        "k_norm": 1.0 + 0.02 * nrm(ks[9], (DEPTH, HEAD_DIM), f32),
        "hg_lb_logits": 0.5 * nrm(ks[10], (2, DEPTH, HG_W), f32),
        "hg_norm": 1.0 + 0.02 * nrm(ks[11], (DEPTH, HG_DV), f32),
        "w_out": nrm(ks[12], (DEPTH, MIX_W, D_MODEL), f32) * MIX_W ** -0.5,
        "w_gate": nrm(ks[13], (DEPTH, D_MODEL, D_FF), f32) * D_MODEL ** -0.5,
        "w_up": nrm(ks[14], (DEPTH, D_MODEL, D_FF), f32) * D_MODEL ** -0.5,
        "w_down": nrm(ks[15], (DEPTH, D_FF, D_MODEL), f32) * D_FF ** -0.5,
        "final_norm": 1.0 + 0.02 * nrm(ks[16], (D_MODEL,), f32),
    }


def reference(x, c, ctx, c_ctx, w_ada, b_ada, w_in, w_four, q_norm, k_norm,
              hg_lb_logits, hg_norm, w_out, w_gate, w_up, w_down, final_norm):
    d = D_MODEL
    tabs = axial_rope_tables(x.shape[1])
    silu_c = jax.nn.silu(c)
    silu_cc = jax.nn.silu(c_ctx)
    lb_sm = jax.nn.softmax(hg_lb_logits.astype(jnp.float32), axis=1)
    lb_all = jnp.cumsum(lb_sm, axis=1) - lb_sm[:, :1]
    for l in range(DEPTH):
        ctx_out = l < DEPTH - 1
        mod = silu_c @ w_ada[l] + b_ada[l]
        sh_a, sc_a, g_a, sh_f, sc_f, g_f = jnp.split(mod[:, None, :], 6, axis=-1)
        n_c = 6 * d if ctx_out else 2 * d
        mc = jnp.split(silu_cc @ w_ada[l][:, :n_c] + b_ada[l][:n_c], n_c // d)
        hx = rmsnorm(x) * (1 + sc_a) + sh_a
        hc = rmsnorm(ctx) * (1 + mc[1]) + mc[0]
        ux = hx @ w_in[l]
        base_c = 0 if ctx_out else OFF_K
        uc = hc @ (w_in[l] if ctx_out else w_in[l][:, OFF_K:OFF_HQ])
        fx = fourier_mix(col(ux, OFF_FOURIER, FOURIER_W), w_four[l])
        ax, ac = attention_group(ux, uc, base_c, q_norm[l], k_norm[l], tabs, ctx_out)
        rx, rc = hgrn_group(ux, uc, base_c, lb_all[:, l], hg_norm[l], ctx_out)
        x = x + g_a * (jnp.concatenate([fx, ax, rx], axis=-1) @ w_out[l])
        x = x + g_f * swiglu(rmsnorm(x) * (1 + sc_f) + sh_f, w_gate[l], w_up[l], w_down[l])
        if ctx_out:
            fc = fourier_mix(col(uc, OFF_FOURIER, FOURIER_W), w_four[l])
            ctx = ctx + mc[2] * (jnp.concatenate([fc, ac, rc], axis=-1) @ w_out[l])
            ctx = ctx + mc[5] * swiglu(rmsnorm(ctx) * (1 + mc[4]) + mc[3], w_gate[l], w_up[l], w_down[l])
    return rmsnorm(x, final_norm)
```

```python
import functools

import numpy as np
import jax
import jax.numpy as jnp
from jax import lax
from jax.experimental import pallas as pl
from jax.experimental.pallas import tpu as pltpu

F32 = jnp.float32
BF16 = jnp.bfloat16

EPS = 1e-6
GRID_W = 64
ROPE_THETA = 10000.0
FOURIER_GD = 64
ATT_HEADS = 8
ATT_KV_HEADS = 2
ATT_GROUP = ATT_HEADS // ATT_KV_HEADS
HEAD_DIM = 64
HG_HEADS = 4
HG_CHUNK = 64
HG_LEVELS = (32, 16, 8, 4, 2, 1)
DFT_N2 = 128
NEG_BIG = -1e30
V7X_VMEM_LIMIT = 56 * 1024 * 1024


def _dot(a, b):
    return jnp.dot(a, b, preferred_element_type=F32)


def _dot_nt(a, b):
    return lax.dot_general(a, b, (((1,), (1,)), ((), ())), preferred_element_type=F32)


def _dot_tn(a, b):
    return lax.dot_general(a, b, (((0,), (0,)), ((), ())), preferred_element_type=F32)


def _split3(x):
    hi = x.astype(BF16)
    r1 = x - hi.astype(F32)
    mid = r1.astype(BF16)
    lo = (r1 - mid.astype(F32)).astype(BF16)
    return hi, mid, lo


def _pick_tile(n, cap, mult=128):
    best = None
    for t in range(mult, min(n, cap) + 1, mult):
        if n % t == 0:
            best = t
    if best is None:
        raise ValueError(f"no tile for {n}")
    return best


def _resident(shape):
    nd = len(shape)
    return pl.BlockSpec(shape, lambda *_: (0,) * nd, pipeline_mode=pl.Buffered(1))


def _adaln_kernel(c_ref, w_ref, b_ref, o_ref):
    cv = c_ref[...]
    a = cv * jax.nn.sigmoid(cv)
    w = w_ref[0]
    a_hi = a.astype(BF16)
    a_lo = (a - a_hi.astype(F32)).astype(BF16)
    w_hi = w.astype(BF16)
    w_lo = (w - w_hi.astype(F32)).astype(BF16)
    acc = _dot(a_hi, w_hi) + _dot(a_lo, w_hi) + _dot(a_hi, w_lo)
    o_ref[0] = acc + b_ref[0]


def _adaln(cvec, w_ada, b_ada):
    depth, d, n6 = w_ada.shape
    r8 = cvec.shape[0]
    tn = _pick_tile(n6, 1536)
    return pl.pallas_call(
        _adaln_kernel,
        grid=(depth, n6 // tn),
        in_specs=[
            pl.BlockSpec((r8, d), lambda l, j: (0, 0)),
            pl.BlockSpec((1, d, tn), lambda l, j: (l, 0, j)),
            pl.BlockSpec((1, 1, tn), lambda l, j: (l, 0, j)),
        ],
        out_specs=pl.BlockSpec((1, r8, tn), lambda l, j: (l, 0, j)),
        out_shape=jax.ShapeDtypeStruct((depth, r8, n6), F32),
        name="adaln",
        compiler_params=pltpu.CompilerParams(
            dimension_semantics=("parallel", "parallel"),
            vmem_limit_bytes=V7X_VMEM_LIMIT),
    )(cvec, w_ada, b_ada.reshape(depth, 1, n6))


def _modulated(x, mb, mc, is_ctx, d, k_shift, k_scale):
    r = lax.rsqrt(jnp.mean(x * x, axis=-1, keepdims=True) + EPS)
    sh = jnp.where(is_ctx, mc[:, k_shift * d:(k_shift + 1) * d], mb[:, k_shift * d:(k_shift + 1) * d])
    sc = jnp.where(is_ctx, mc[:, k_scale * d:(k_scale + 1) * d], mb[:, k_scale * d:(k_scale + 1) * d])
    return (x * r) * (1.0 + sc) + sh


def _inproj_kernel(x_ref, mb_ref, mc_ref, wn_ref, wt_ref, gk_ref, ckn_ref, skn_ref,
                   gq_ref, cqt_ref, sqt_ref, lbf_ref, lbb_ref,
                   fz_ref, k_ref, kf_ref, lff_ref, kb_ref, lfb_ref, v_ref, hq_ref, g_ref,
                   qt_ref, vt_ref, *, lc, tm, d, fw, hw):
    i = pl.program_id(1)
    x = x_ref[0]
    row = i * tm + lax.broadcasted_iota(jnp.int32, (tm, 1), 0)
    is_ctx = row < lc
    h = _modulated(x, mb_ref[0, 0], mc_ref[0, 0], is_ctx, d, 0, 1).astype(BF16)

    un = _dot(h, wn_ref[...])
    fz_ref[0] = un[:, 0:fw]

    o = fw
    kw = ATT_KV_HEADS * HEAD_DIM
    uk = un[:, o:o + kw]
    lane = lax.broadcasted_iota(jnp.int32, (1, kw), 1)
    uk2 = uk * uk
    ms = jnp.zeros_like(uk)
    for hh in range(ATT_KV_HEADS):
        sel = (lane // HEAD_DIM) == hh
        ssh = jnp.sum(jnp.where(sel, uk2, 0.0), axis=-1, keepdims=True)
        ms = jnp.where(sel, ssh * (1.0 / HEAD_DIM), ms)
    kn = uk * lax.rsqrt(ms + EPS) * gk_ref[...]
    first = (lane % 32) < 16
    partner = jnp.where(first, pltpu.roll(kn, kw - 16, axis=1), pltpu.roll(kn, 16, axis=1))
    k_ref[0] = (kn * ckn_ref[...] + partner * skn_ref[...]).astype(BF16)
    o += kw

    for lb_ref, kk_ref, lf_ref in ((lbf_ref, kf_ref, lff_ref), (lbb_ref, kb_ref, lfb_ref)):
        z = un[:, o:o + hw]
        lb = lb_ref[...]
        kk_ref[0] = (1.0 - lb) * jax.nn.sigmoid(-z)
        lf_ref[0] = jnp.log(lb + (1.0 - lb) * jax.nn.sigmoid(z))
        o += hw
    v_ref[0] = un[:, o:o + hw].astype(BF16)
    hq_ref[0] = un[:, o + hw:o + 2 * hw]
    g_ref[0] = un[:, o + 2 * hw:o + 3 * hw]

    ut = _dot_nt(wt_ref[...], h)
    qw = ATT_HEADS * HEAD_DIM
    uq = ut[0:qw].reshape(ATT_HEADS, HEAD_DIM, tm)
    msq = jnp.mean(uq * uq, axis=1, keepdims=True)
    qn = uq * lax.rsqrt(msq + EPS) * gq_ref[...][None]
    cr, sr = cqt_ref[0:16][None], sqt_ref[0:16][None]
    cc, sc = cqt_ref[16:32][None], sqt_ref[16:32][None]
    x1r, x2r, x1c, x2c = qn[:, 0:16], qn[:, 16:32], qn[:, 32:48], qn[:, 48:64]
    qr = jnp.concatenate([x1r * cr - x2r * sr, x1r * sr + x2r * cr,
                          x1c * cc - x2c * sc, x1c * sc + x2c * cc], axis=1)
    qt_ref[0] = (qr * (HEAD_DIM ** -0.5)).astype(BF16)

    uv = ut[qw:qw + kw].reshape(ATT_KV_HEADS, HEAD_DIM, tm)
    ones_row = lax.broadcasted_iota(jnp.int32, (ATT_KV_HEADS, 16, tm), 1) == 0
    aug = jnp.where(ones_row, 1.0, 0.0).astype(F32)
    vt_ref[0] = jnp.concatenate([uv, aug], axis=1).astype(BF16)


def _inproj(xs, mod4, l, nb, wn, wt, gk, ckn, skn, gq, cqt, sqt, lbf, lbb, lc, tm):
    b, t, d = xs.shape
    n6 = mod4.shape[-1]
    fw = FOURIER_GD * 4
    hw = lbf.shape[-1]
    kw = ATT_KV_HEADS * HEAD_DIM
    tok = lambda w, dt: jax.ShapeDtypeStruct((b, t, w), dt)
    tok_spec = lambda w: pl.BlockSpec((1, tm, w), lambda bi, i: (bi, i, 0))
    out_shape = [tok(fw, F32), tok(kw, BF16), tok(hw, F32), tok(hw, F32), tok(hw, F32), tok(hw, F32),
                 tok(hw, BF16), tok(hw, F32), tok(hw, F32),
                 jax.ShapeDtypeStruct((b, ATT_HEADS, HEAD_DIM, t), BF16),
                 jax.ShapeDtypeStruct((b, ATT_KV_HEADS, HEAD_DIM + 16, t), BF16)]
    out_specs = [tok_spec(fw), tok_spec(kw)] + [tok_spec(hw)] * 7 + [
        pl.BlockSpec((1, ATT_HEADS, HEAD_DIM, tm), lambda bi, i: (bi, 0, 0, i)),
        pl.BlockSpec((1, ATT_KV_HEADS, HEAD_DIM + 16, tm), lambda bi, i: (bi, 0, 0, i))]
    in_specs = [
        pl.BlockSpec((1, tm, d), lambda bi, i: (bi, i, 0)),
        pl.BlockSpec((1, 1, 1, n6), lambda bi, i: (l, bi, 0, 0)),
        pl.BlockSpec((1, 1, 1, n6), lambda bi, i: (l, nb, 0, 0)),
        _resident(wn.shape), _resident(wt.shape), _resident(gk.shape),
        pl.BlockSpec((tm, kw), lambda bi, i: (i, 0)),
        pl.BlockSpec((tm, kw), lambda bi, i: (i, 0)),
        _resident(gq.shape),
        pl.BlockSpec((32, tm), lambda bi, i: (0, i)),
        pl.BlockSpec((32, tm), lambda bi, i: (0, i)),
        _resident(lbf.shape), _resident(lbb.shape),
    ]
    return pl.pallas_call(
        functools.partial(_inproj_kernel, lc=lc, tm=tm, d=d, fw=fw, hw=hw),
        grid=(b, t // tm),
        in_specs=in_specs, out_specs=out_specs, out_shape=out_shape,
        name="inproj",
        compiler_params=pltpu.CompilerParams(
            dimension_semantics=("parallel", "parallel"),
            vmem_limit_bytes=V7X_VMEM_LIMIT),
    )(xs, mod4, mod4, wn, wt, gk, ckn, skn, gq, cqt, sqt, lbf, lbb)


def _fourier_kernel(z_ref, f1_ref, twc_ref, tws_ref, c2_ref, s2_ref, cs_ref, cl_ref,
                    y_ref, a1_ref, p_ref, *, lc, n1, scale_c, scale_x):
    n2 = DFT_N2
    cs = cs_ref[...]

    zc = z_ref[0, 0:lc, :].astype(BF16)
    pc = _dot(cl_ref[...], zc)
    pcat = jnp.concatenate([pc[0:lc], pc[lc:2 * lc]], axis=1).astype(BF16)
    y_ref[0, 0:lc, :] = (_dot(pcat, cs) * scale_c).astype(y_ref.dtype)

    f1 = f1_ref[...]

    def stage1(j, carry):
        xj = z_ref[0, pl.ds(lc + j, n1, stride=n2), :].astype(BF16)
        a1_ref[pl.ds(pl.multiple_of(j * 2 * n1, 2 * n1), 2 * n1), :] = _dot(f1, xj)
        return carry

    lax.fori_loop(0, n2, stage1, 0)

    c2 = c2_ref[...]
    s2 = s2_ref[...]

    def stage2(k1, carry):
        ar = a1_ref[pl.ds(k1, n2, stride=2 * n1), :]
        ai = a1_ref[pl.ds(n1 + k1, n2, stride=2 * n1), :]
        twc = twc_ref[pl.ds(k1, 1), :]
        tws = tws_ref[pl.ds(k1, 1), :]
        gr = c2 * twc - s2 * tws
        gi = -(s2 * twc + c2 * tws)
        gm = jnp.concatenate([jnp.concatenate([gr, -gi], axis=1),
                              jnp.concatenate([gi, gr], axis=1)], axis=0).astype(BF16)
        rhs = jnp.concatenate([ar, ai], axis=0).astype(BF16)
        pp = _dot(gm, rhs)
        pcat2 = jnp.concatenate([pp[0:n2], pp[n2:2 * n2]], axis=1).astype(BF16)
        p_ref[pl.ds(pl.multiple_of(k1 * n2, n2), n2), :] = _dot(pcat2, cs) * scale_x
        return carry

    lax.fori_loop(0, n1, stage2, 0)

    def stage3(k2, carry):
        blk = p_ref[pl.ds(k2, n1, stride=n2), :]
        y_ref[0, pl.ds(pl.multiple_of(lc + k2 * n1, n1), n1), :] = blk.astype(y_ref.dtype)
        return carry

    lax.fori_loop(0, n2, stage3, 0)


def _fourier_consts(lc, s):
    n2 = DFT_N2
    n1 = s // n2
    f64 = np.float64
    k = np.arange(n1, dtype=f64)
    a1 = 2 * np.pi * np.outer(k, k) / n1
    f1 = np.concatenate([np.cos(a1), -np.sin(a1)], axis=0)
    at = 2 * np.pi * np.outer(np.arange(n1, dtype=f64), np.arange(n2, dtype=f64)) / s
    k2 = np.arange(n2, dtype=f64)
    a2 = 2 * np.pi * np.outer(k2, k2) / n2
    w = 2 * FOURIER_GD
    ch = np.arange(w)
    same = (ch[:, None] // FOURIER_GD) == (ch[None, :] // FOURIER_GD)
    ag = 2 * np.pi * np.outer(ch % FOURIER_GD, ch % FOURIER_GD) / FOURIER_GD
    cs = np.concatenate([np.where(same, np.cos(ag), 0.0), np.where(same, np.sin(ag), 0.0)], axis=0)
    kc = np.arange(lc, dtype=f64)
    al = 2 * np.pi * np.outer(kc, kc) / lc
    cl = np.concatenate([np.cos(al), -np.sin(al)], axis=0)
    return dict(
        f1=jnp.asarray(f1, BF16), twc=jnp.asarray(np.cos(at), F32), tws=jnp.asarray(np.sin(at), F32),
        c2=jnp.asarray(np.cos(a2), F32), s2=jnp.asarray(np.sin(a2), F32),
        cs=jnp.asarray(cs, BF16), cl=jnp.asarray(cl, BF16), n1=n1)


def _fourier(fz, fc, lc):
    b, t, fw = fz.shape
    s = t - lc
    n1 = fc["n1"]
    w = 2 * FOURIER_GD
    consts = [fc[k] for k in ("f1", "twc", "tws", "c2", "s2", "cs", "cl")]
    return pl.pallas_call(
        functools.partial(_fourier_kernel, lc=lc, n1=n1,
                          scale_c=float((lc * FOURIER_GD) ** -0.5),
                          scale_x=float((s * FOURIER_GD) ** -0.5)),
        grid=(b, fw // w),
        in_specs=[pl.BlockSpec((1, t, w), lambda bi, hi: (bi, 0, hi))] +
                 [_resident(c.shape) for c in consts],
        out_specs=pl.BlockSpec((1, t, w), lambda bi, hi: (bi, 0, hi)),
        out_shape=jax.ShapeDtypeStruct((b, t, fw), BF16),
        scratch_shapes=[pltpu.VMEM((DFT_N2 * 2 * n1, w), F32), pltpu.VMEM((n1 * DFT_N2, w), F32)],
        name="fourier",
        compiler_params=pltpu.CompilerParams(
            dimension_semantics=("parallel", "parallel"),
            vmem_limit_bytes=V7X_VMEM_LIMIT),
    )(fz, *consts)


def _attn_kernel(q_ref, k_ref, v_ref, o_ref, qs_ref, m_ref, acc_ref, *, lc, tq, tk, nqc, nkc):
    g = pl.program_id(1)
    qi = pl.program_id(2)
    ki = pl.program_id(3)
    is_ctx = qi < nqc

    @pl.when(ki == 0)
    def _init():
        qcat = jnp.concatenate([q_ref[0, j] for j in range(ATT_GROUP)], axis=1)
        zero = jnp.zeros_like(qcat)
        for gg in range(ATT_KV_HEADS):
            @pl.when(g == gg)
            def _place():
                for hh in range(ATT_KV_HEADS):
                    qs_ref[hh * HEAD_DIM:(hh + 1) * HEAD_DIM, :] = qcat if hh == gg else zero
        m_ref[...] = jnp.full(m_ref.shape, NEG_BIG, F32)
        acc_ref[...] = jnp.zeros(acc_ref.shape, F32)

    def step(masked):
        s = _dot(k_ref[0], qs_ref[...])
        if masked:
            kidx = ki * tk + lax.broadcasted_iota(jnp.int32, (tk, 1), 0)
            s = jnp.where(kidx < lc, s, NEG_BIG)
        m_prev = m_ref[...]
        m_new = jnp.maximum(m_prev, jnp.max(s, axis=0, keepdims=True))
        alpha = jnp.exp(m_prev - m_new)
        p = jnp.exp(s - m_new).astype(BF16)
        acc_ref[...] = alpha * acc_ref[...] + _dot(v_ref[0, 0], p)
        m_ref[...] = m_new

    @pl.when(jnp.logical_and(is_ctx, ki < nkc))
    def _ctx_step():
        step(True)

    @pl.when(jnp.logical_not(is_ctx))
    def _lat_step():
        step(False)

    @pl.when(ki == pl.num_programs(3) - 1)
    def _fin():
        acc = acc_ref[...]
        o = acc[0:HEAD_DIM] / acc[HEAD_DIM:HEAD_DIM + 1]
        o_ref[0] = jnp.concatenate([o[:, j * tq:(j + 1) * tq] for j in range(ATT_GROUP)],
                                   axis=0).astype(o_ref.dtype)


def _attention(qt, kn, vt, lc, tq, tk):
    b, _, _, t = qt.shape
    nq, nk = t // tq, t // tk
    nqc = lc // tq
    nkc = -(-lc // tk)
    kblk = lambda qi, ki: jnp.where(qi < nqc, jnp.minimum(ki, nkc - 1), ki)
    return pl.pallas_call(
        functools.partial(_attn_kernel, lc=lc, tq=tq, tk=tk, nqc=nqc, nkc=nkc),
        grid=(b, ATT_KV_HEADS, nq, nk),
        in_specs=[
            pl.BlockSpec((1, ATT_GROUP, HEAD_DIM, tq), lambda bi, g, qi, ki: (bi, g, 0, qi)),
            pl.BlockSpec((1, tk, ATT_KV_HEADS * HEAD_DIM), lambda bi, g, qi, ki: (bi, kblk(qi, ki), 0)),
            pl.BlockSpec((1, 1, HEAD_DIM + 16, tk), lambda bi, g, qi, ki: (bi, g, 0, kblk(qi, ki))),
        ],
        out_specs=pl.BlockSpec((1, ATT_GROUP * HEAD_DIM, tq), lambda bi, g, qi, ki: (bi, g, qi)),
        out_shape=jax.ShapeDtypeStruct((b, ATT_HEADS * HEAD_DIM, t), BF16),
        scratch_shapes=[pltpu.VMEM((ATT_KV_HEADS * HEAD_DIM, ATT_GROUP * tq), BF16),
                        pltpu.VMEM((1, ATT_GROUP * tq), F32),
                        pltpu.VMEM((HEAD_DIM + 16, ATT_GROUP * tq), F32)],
        name="attn",
        compiler_params=pltpu.CompilerParams(
            dimension_semantics=("parallel", "parallel", "parallel", "arbitrary"),
            vmem_limit_bytes=V7X_VMEM_LIMIT),
    )(qt, kn, vt)


def _hgrn_consts():
    c = HG_CHUNK
    t = np.arange(c)
    tt, rr = t[:, None], t[None, :]
    mats, valids, masks = [], [], []
    for rev in (False, True):
        rows = [(rr >= tt) if rev else (rr <= tt),
                (rr < tt) if rev else (rr > tt),
                np.ones((8, c), bool)]
        valid = [np.ones(c, bool), np.ones(c, bool), np.ones(8, bool)]
        lvl_masks = []
        for bsz in HG_LEVELS:
            blk, pos = t // (2 * bsz), t % (2 * bsz)
            if rev:
                qv, kv = pos < bsz, pos >= bsz
                ref = blk * 2 * bsz + bsz
                mq = (rr >= tt) & (rr < ref[:, None])
                mk = (rr >= ref[:, None]) & (rr < tt)
            else:
                qv, kv = pos >= bsz, pos < bsz
                ref = blk * 2 * bsz + bsz - 1
                mq = (rr > ref[:, None]) & (rr <= tt)
                mk = (rr > tt) & (rr <= ref[:, None])
            rows += [mq & qv[:, None], mk & kv[:, None]]
            valid += [qv, kv]
            lvl_masks.append((blk[:, None] == blk[None, :]) & qv[:, None] & kv[None, :])
        lvl_masks.append(tt == rr)
        mats.append(np.concatenate(rows, axis=0))
        valids.append(np.concatenate(valid)[:, None])
        masks.append(np.stack([np.tile(m, (HG_HEADS, 1)) for m in lvl_masks]))
    return (jnp.asarray(np.stack(mats), BF16), jnp.asarray(np.stack(valids), F32),
            jnp.asarray(np.stack(masks), F32))


def _hgrn_chunk(d, r0, kk_ref, lf_ref, v_ref, q_ref, cm_ref, rv_ref, am_ref, o_ref, s_ref, hw):
    c = HG_CHUNK
    dk = hw // HG_HEADS
    rows = pl.ds(r0, c)
    lf = lf_ref[0, rows, :]
    cm = cm_ref[d]
    hi, mid, lo = _split3(lf)
    e = _dot(cm, hi) + _dot(cm, mid) + _dot(cm, lo)
    w = jnp.where(rv_ref[d] > 0.0, jnp.exp(e), 0.0)
    q = q_ref[0, rows, :]
    k = kk_ref[0, rows, :]
    v = v_ref[0, rows, :]
    qdec = q * w[0:c]
    kdec = k * w[c:2 * c]
    ds = w[2 * c:2 * c + 1]

    lane_head = lax.broadcasted_iota(jnp.int32, (1, hw), 1) // dk
    heads = [lane_head == hh for hh in range(HG_HEADS)]

    att = jnp.zeros((HG_HEADS * c, c), F32)
    base = 2 * c + 8
    for lv in range(len(HG_LEVELS) + 1):
        if lv < len(HG_LEVELS):
            qd = q * w[base + 2 * lv * c:base + (2 * lv + 1) * c]
            kd = k * w[base + (2 * lv + 1) * c:base + (2 * lv + 2) * c]
        else:
            qd, kd = q, k
        q4 = jnp.concatenate([jnp.where(hm, qd, 0.0) for hm in heads], axis=0).astype(BF16)
        att = att + am_ref[d, lv] * _dot_nt(q4, kd.astype(BF16))
    ov = _dot(att.astype(BF16), v)
    o = jnp.zeros((c, hw), F32)
    for hh, hm in enumerate(heads):
        o = o + jnp.where(hm, ov[hh * c:(hh + 1) * c], 0.0)

    st = s_ref[...]
    o = o + _dot_nt(qdec.astype(BF16), st.astype(BF16))
    o_ref[0, rows, :] = o
    u = _dot_tn(v, kdec.astype(BF16))
    rh = lax.broadcasted_iota(jnp.int32, (hw, 1), 0) // dk
    s_ref[...] = st * ds + jnp.where(rh == lane_head, u, 0.0)


def _hgrn_kernel(kf_ref, lff_ref, vf_ref, qf_ref, kb_ref, lfb_ref, vb_ref, qb_ref,
                 cm_ref, rv_ref, am_ref, of_ref, ob_ref, sf_ref, sb_ref, *, nchunk, hw):
    @pl.when(pl.program_id(1) == 0)
    def _init():
        sf_ref[...] = jnp.zeros(sf_ref.shape, F32)
        sb_ref[...] = jnp.zeros(sb_ref.shape, F32)

    for jj in range(nchunk):
        _hgrn_chunk(0, jj * HG_CHUNK, kf_ref, lff_ref, vf_ref, qf_ref,
                    cm_ref, rv_ref, am_ref, of_ref, sf_ref, hw)
        _hgrn_chunk(1, (nchunk - 1 - jj) * HG_CHUNK, kb_ref, lfb_ref, vb_ref, qb_ref,
                    cm_ref, rv_ref, am_ref, ob_ref, sb_ref, hw)


def _hgrn(kf, lff, kb, lfb, v, hq, hc, lc, nchunk):
    b, t, hw = kf.shape
    tb = nchunk * HG_CHUNK
    nblk, ncb = t // tb, lc // tb
    cm, rv, am = hc
    fwd = lambda bi, p: (bi, p, 0)
    bwd = lambda bi, p: (bi, jnp.where(p < ncb, ncb - 1 - p, nblk - 1 - (p - ncb)), 0)
    blk = lambda im: pl.BlockSpec((1, tb, hw), im)
    return pl.pallas_call(
        functools.partial(_hgrn_kernel, nchunk=nchunk, hw=hw),
        grid=(b, nblk),
        in_specs=[blk(fwd)] * 4 + [blk(bwd)] * 4 + [_resident(cm.shape), _resident(rv.shape),
                                                   _resident(am.shape)],
        out_specs=[blk(fwd), blk(bwd)],
        out_shape=[jax.ShapeDtypeStruct((b, t, hw), F32)] * 2,
        scratch_shapes=[pltpu.VMEM((hw, hw), F32), pltpu.VMEM((hw, hw), F32)],
        name="hgrn",
        compiler_params=pltpu.CompilerParams(
            dimension_semantics=("parallel", "arbitrary"),
            vmem_limit_bytes=V7X_VMEM_LIMIT),
    )(kf, lff, v, hq, kb, lfb, v, hq, cm, rv, am)


def _outproj_kernel(x_ref, mb_ref, mc_ref, y_ref, at_ref, of_ref, ob_ref, g_ref, wf_ref, wo_ref,
                    gn_ref, o_ref, *, lc, tm, d, fw, aw, hw):
    i = pl.program_id(1)
    row = i * tm + lax.broadcasted_iota(jnp.int32, (tm, 1), 0)
    is_ctx = row < lc
    mb, mc = mb_ref[0, 0], mc_ref[0, 0]
    gate = jnp.where(is_ctx, mc[:, 2 * d:3 * d], mb[:, 2 * d:3 * d])

    fx = _dot(y_ref[0], wf_ref[...]).astype(BF16)
    mix = _dot(fx, wo_ref[0:fw, :])
    mix = mix + _dot_tn(at_ref[0], wo_ref[fw:fw + aw, :])

    o = of_ref[0] + ob_ref[0]
    dv = hw // HG_HEADS
    lane_head = lax.broadcasted_iota(jnp.int32, (1, hw), 1) // dv
    o2 = o * o
    ms = jnp.zeros_like(o)
    for hh in range(HG_HEADS):
        sel = lane_head == hh
        ssh = jnp.sum(jnp.where(sel, o2, 0.0), axis=-1, keepdims=True)
        ms = jnp.where(sel, ssh * (1.0 / dv), ms)
    gg = g_ref[0]
    rx = (o * lax.rsqrt(ms + EPS) * gn_ref[...]) * (gg * jax.nn.sigmoid(gg))
    mix = mix + _dot(rx.astype(BF16), wo_ref[fw + aw:fw + aw + hw, :])
    o_ref[0] = x_ref[0] + gate * mix


def _outproj(xs, mod4, l, nb, y, at, of, ob, g, wf, wo, gn, lc, tm):
    b, t, d = xs.shape
    n6 = mod4.shape[-1]
    fw, hw = y.shape[-1], of.shape[-1]
    aw = at.shape[1]
    tok = lambda w: pl.BlockSpec((1, tm, w), lambda bi, i: (bi, i, 0))
    return pl.pallas_call(
        functools.partial(_outproj_kernel, lc=lc, tm=tm, d=d, fw=fw, aw=aw, hw=hw),
        grid=(b, t // tm),
        in_specs=[tok(d),
                  pl.BlockSpec((1, 1, 1, n6), lambda bi, i: (l, bi, 0, 0)),
                  pl.BlockSpec((1, 1, 1, n6), lambda bi, i: (l, nb, 0, 0)),
                  tok(fw),
                  pl.BlockSpec((1, aw, tm), lambda bi, i: (bi, 0, i)),
                  tok(hw), tok(hw), tok(hw),
                  _resident(wf.shape), _resident(wo.shape), _resident(gn.shape)],
        out_specs=tok(d),
        out_shape=jax.ShapeDtypeStruct((b, t, d), F32),
        name="outproj",
        compiler_params=pltpu.CompilerParams(
            dimension_semantics=("parallel", "parallel"),
            vmem_limit_bytes=V7X_VMEM_LIMIT),
    )(xs, mod4, mod4, y, at, of, ob, g, wf, wo, gn)


def _ffn_kernel(x_ref, mb_ref, mc_ref, wg_ref, wu_ref, wd_ref, o_ref, *, lc, tm, d, fchunk):
    i = pl.program_id(1)
    x = x_ref[0]
    row = i * tm + lax.broadcasted_iota(jnp.int32, (tm, 1), 0)
    is_ctx = row < lc
    mb, mc = mb_ref[0, 0], mc_ref[0, 0]
    h = _modulated(x, mb, mc, is_ctx, d, 3, 4).astype(BF16)
    gate = jnp.where(is_ctx, mc[:, 5 * d:6 * d], mb[:, 5 * d:6 * d])
    dff = wg_ref.shape[1]
    acc = jnp.zeros((tm, d), F32)
    for c0 in range(0, dff, fchunk):
        a = _dot(h, wg_ref[:, c0:c0 + fchunk])
        u = _dot(h, wu_ref[:, c0:c0 + fchunk])
        act = (a * jax.nn.sigmoid(a) * u).astype(BF16)
        acc = acc + _dot(act, wd_ref[c0:c0 + fchunk, :])
    o_ref[0] = x + gate * acc


def _ffn(xs, mod4, l, nb, wg, wu, wd, lc, tm):
    b, t, d = xs.shape
    n6 = mod4.shape[-1]
    dff = wg.shape[1]
    fchunk = _pick_tile(dff, 768)
    tok = pl.BlockSpec((1, tm, d), lambda bi, i: (bi, i, 0))
    return pl.pallas_call(
        functools.partial(_ffn_kernel, lc=lc, tm=tm, d=d, fchunk=fchunk),
        grid=(b, t // tm),
        in_specs=[tok,
                  pl.BlockSpec((1, 1, 1, n6), lambda bi, i: (l, bi, 0, 0)),
                  pl.BlockSpec((1, 1, 1, n6), lambda bi, i: (l, nb, 0, 0)),
                  _resident(wg.shape), _resident(wu.shape), _resident(wd.shape)],
        out_specs=tok,
        out_shape=jax.ShapeDtypeStruct((b, t, d), F32),
        name="ffn",
        compiler_params=pltpu.CompilerParams(
            dimension_semantics=("parallel", "parallel"),
            vmem_limit_bytes=V7X_VMEM_LIMIT),
    )(xs, mod4, mod4, wg, wu, wd)


def _final_kernel(x_ref, g_ref, o_ref):
    x = x_ref[0]
    o_ref[0] = x * lax.rsqrt(jnp.mean(x * x, axis=-1, keepdims=True) + EPS) * g_ref[...]


def _final_norm(xs, gain, lc, tm):
    b, t, d = xs.shape
    s = t - lc
    off = lc // tm
    return pl.pallas_call(
        _final_kernel,
        grid=(b, s // tm),
        in_specs=[pl.BlockSpec((1, tm, d), lambda bi, i: (bi, i + off, 0)),
                  pl.BlockSpec((1, d), lambda bi, i: (0, 0))],
        out_specs=pl.BlockSpec((1, tm, d), lambda bi, i: (bi, i, 0)),
        out_shape=jax.ShapeDtypeStruct((b, s, d), F32),
        name="final_norm",
        compiler_params=pltpu.CompilerParams(dimension_semantics=("parallel", "parallel")),
    )(xs, gain.reshape(1, d))


def _rope_tables(lc, s):
    half = HEAD_DIM // 2
    n = jnp.arange(s)
    freqs = ROPE_THETA ** (-jnp.arange(0, half, 2, dtype=F32) / half)
    ang_r = (n // GRID_W).astype(F32)[:, None] * freqs
    ang_c = (n % GRID_W).astype(F32)[:, None] * freqs
    pad = lambda a, v: jnp.concatenate([jnp.full((lc, a.shape[1]), v, F32), a], axis=0)
    cr, sr = pad(jnp.cos(ang_r), 1.0), pad(jnp.sin(ang_r), 0.0)
    cc, sc = pad(jnp.cos(ang_c), 1.0), pad(jnp.sin(ang_c), 0.0)
    cos_h = jnp.concatenate([cr, cr, cc, cc], axis=1)
    sin_h = jnp.concatenate([-sr, sr, -sc, sc], axis=1)
    ckn = jnp.tile(cos_h, (1, ATT_KV_HEADS))
    skn = jnp.tile(sin_h, (1, ATT_KV_HEADS))
    cqt = jnp.concatenate([cr, cc], axis=1).T
    sqt = jnp.concatenate([sr, sc], axis=1).T
    return ckn, skn, cqt, sqt


def kernel(x, c, ctx, c_ctx, w_ada, b_ada, w_in, w_four, q_norm, k_norm, hg_lb_logits, hg_norm,
           w_out, w_gate, w_up, w_down, final_norm):
    b, s, d = x.shape
    lc = ctx.shape[1]
    t = lc + s
    depth = w_ada.shape[0]
    fw = w_four.shape[-1]
    hw = hg_lb_logits.shape[-1]
    qw = ATT_HEADS * HEAD_DIM
    kw = ATT_KV_HEADS * HEAD_DIM
    assert w_in.shape[-1] == fw + qw + 2 * kw + 5 * hw
    assert s % (DFT_N2 * 8) == 0 and lc % 128 == 0 and s % GRID_W == 0

    tm = _pick_tile(t, 768)
    tq = 256 if (lc % 256 == 0 and t % 256 == 0) else 128
    tk = _pick_tile(t, 768)
    nchunk = max(g for g in (4, 2, 1) if lc % (g * HG_CHUNK) == 0 and t % (g * HG_CHUNK) == 0)
    tfin = 256 if (lc % 256 == 0 and s % 256 == 0) else 128

    xs = jnp.concatenate([ctx, x], axis=1)
    r8 = -(-(b + 1) // 8) * 8
    cvec = jnp.concatenate([c, c_ctx[None], jnp.zeros((r8 - b - 1, d), F32)], axis=0)
    mod = _adaln(cvec, w_ada, b_ada)
    mod4 = mod.reshape(depth, r8, 1, mod.shape[-1])

    lb_sm = jax.nn.softmax(hg_lb_logits.astype(F32), axis=1)
    lb_all = jnp.cumsum(lb_sm, axis=1) - lb_sm[:, :1]

    ckn, skn, cqt, sqt = _rope_tables(lc, s)
    fc = _fourier_consts(lc, s)
    hc = _hgrn_consts()

    o_q, o_k, o_v, o_h = fw, fw + qw, fw + qw + kw, fw + qw + 2 * kw
    for l in range(depth):
        wl = w_in[l]
        wn = jnp.concatenate([wl[:, 0:fw], wl[:, o_k:o_v], wl[:, o_h:]], axis=1).astype(BF16)
        wt = jnp.concatenate([wl[:, o_q:o_k], wl[:, o_v:o_h]], axis=1).T.astype(BF16)
        gk = jnp.tile(k_norm[l], ATT_KV_HEADS)[None]
        gq = q_norm[l][:, None]
        gn = jnp.tile(hg_norm[l], HG_HEADS)[None]
        (fz, kn, kf, lff, kb, lfb, v, hq, g, qt, vt) = _inproj(
            xs, mod4, l, b, wn, wt, gk, ckn, skn, gq, cqt, sqt,
            lb_all[0, l][None], lb_all[1, l][None], lc, tm)
        y = _fourier(fz, fc, lc)
        at = _attention(qt, kn, vt, lc, tq, tk)
        of, ob = _hgrn(kf, lff, kb, lfb, v, hq, hc, lc, nchunk)
        xs = _outproj(xs, mod4, l, b, y, at, of, ob, g, w_four[l].astype(BF16),
                      w_out[l].astype(BF16), gn, lc, tm)
        xs = _ffn(xs, mod4, l, b, w_gate[l].astype(BF16), w_up[l].astype(BF16),
                  w_down[l].astype(BF16), lc, tm)
    return _final_norm(xs, final_norm, lc, tfin)
```

```python
import functools

import numpy as np
import jax
import jax.numpy as jnp
from jax import lax
from jax.experimental import pallas as pl
from jax.experimental.pallas import tpu as pltpu

F32 = jnp.float32
BF16 = jnp.bfloat16

EPS = 1e-6
GRID_W = 64
ROPE_THETA = 10000.0
FOURIER_GD = 64
ATT_HEADS = 8
ATT_KV_HEADS = 2
ATT_GROUP = ATT_HEADS // ATT_KV_HEADS
HEAD_DIM = 64
HG_HEADS = 4
HG_CHUNK = 64
HG_LEVELS = (32, 16, 8, 4, 2, 1)
DFT_N2 = 128
NEG_BIG = -1e30
LOG2E = 1.4426950408889634
BOUND_SLACK = 1.05
MAX_SAFE_LOG2_SCORE = 100.0
ATTN_BOUNDED_KEYS = 2816
V7X_VMEM_LIMIT = 56 * 1024 * 1024


def _dot(a, b):
    return jnp.dot(a, b, preferred_element_type=F32)


def _dot_nt(a, b):
    return lax.dot_general(a, b, (((1,), (1,)), ((), ())), preferred_element_type=F32)


def _dot_tn(a, b):
    return lax.dot_general(a, b, (((0,), (0,)), ((), ())), preferred_element_type=F32)


def _split3(x):
    hi = x.astype(BF16)
    r1 = x - hi.astype(F32)
    mid = r1.astype(BF16)
    lo = (r1 - mid.astype(F32)).astype(BF16)
    return hi, mid, lo


def _pick_tile(n, cap, mult=128):
    best = None
    for t in range(mult, min(n, cap) + 1, mult):
        if n % t == 0:
            best = t
    if best is None:
        raise ValueError(f"no tile for {n}")
    return best


def _resident(shape):
    nd = len(shape)
    return pl.BlockSpec(shape, lambda *_: (0,) * nd, pipeline_mode=pl.Buffered(1))


def _adaln_kernel(c_ref, w_ref, b_ref, o_ref):
    cv = c_ref[...]
    a = cv * jax.nn.sigmoid(cv)
    w = w_ref[0]
    a_hi = a.astype(BF16)
    a_lo = (a - a_hi.astype(F32)).astype(BF16)
    w_hi = w.astype(BF16)
    w_lo = (w - w_hi.astype(F32)).astype(BF16)
    acc = _dot(a_hi, w_hi) + _dot(a_lo, w_hi) + _dot(a_hi, w_lo)
    o_ref[0] = acc + b_ref[0]


def _adaln(cvec, w_ada, b_ada):
    depth, d, n6 = w_ada.shape
    r8 = cvec.shape[0]
    tn = _pick_tile(n6, 1536)
    return pl.pallas_call(
        _adaln_kernel,
        grid=(depth, n6 // tn),
        in_specs=[
            pl.BlockSpec((r8, d), lambda l, j: (0, 0)),
            pl.BlockSpec((1, d, tn), lambda l, j: (l, 0, j)),
            pl.BlockSpec((1, 1, tn), lambda l, j: (l, 0, j)),
        ],
        out_specs=pl.BlockSpec((1, r8, tn), lambda l, j: (l, 0, j)),
        out_shape=jax.ShapeDtypeStruct((depth, r8, n6), F32),
        name="adaln",
        compiler_params=pltpu.CompilerParams(
            dimension_semantics=("parallel", "parallel"),
            vmem_limit_bytes=V7X_VMEM_LIMIT),
    )(cvec, w_ada, b_ada.reshape(depth, 1, n6))


def _modulated(x, mb, mc, is_ctx, d, k_shift, k_scale):
    r = lax.rsqrt(jnp.mean(x * x, axis=-1, keepdims=True) + EPS)
    sh = jnp.where(is_ctx, mc[:, k_shift * d:(k_shift + 1) * d], mb[:, k_shift * d:(k_shift + 1) * d])
    sc = jnp.where(is_ctx, mc[:, k_scale * d:(k_scale + 1) * d], mb[:, k_scale * d:(k_scale + 1) * d])
    return (x * r) * (1.0 + sc) + sh


def _inproj_kernel(x_ref, mb_ref, mc_ref, wn_ref, wt_ref, gk_ref, ckn_ref, skn_ref,
                   gq_ref, cqt_ref, sqt_ref, lbf_ref, lbb_ref,
                   fz_ref, k_ref, kf_ref, lff_ref, kb_ref, lfb_ref, v_ref, hq_ref, g_ref,
                   qt_ref, vt_ref, *, lc, tm, tkv, d, fw, hw):
    i = pl.program_id(1)
    x = x_ref[0]
    row = i * tm + lax.broadcasted_iota(jnp.int32, (tm, 1), 0)
    is_ctx = row < lc
    h = _modulated(x, mb_ref[0, 0], mc_ref[0, 0], is_ctx, d, 0, 1).astype(BF16)

    un = _dot(h, wn_ref[...])
    fz_ref[0] = un[:, 0:fw]

    o = fw
    kw = ATT_KV_HEADS * HEAD_DIM
    uk = un[:, o:o + kw]
    lane = lax.broadcasted_iota(jnp.int32, (1, kw), 1)
    uk2 = uk * uk
    ms = jnp.zeros_like(uk)
    for hh in range(ATT_KV_HEADS):
        sel = (lane // HEAD_DIM) == hh
        ssh = jnp.sum(jnp.where(sel, uk2, 0.0), axis=-1, keepdims=True)
        ms = jnp.where(sel, ssh * (1.0 / HEAD_DIM), ms)
    kn = uk * lax.rsqrt(ms + EPS) * gk_ref[...]
    first = (lane % 32) < 16
    partner = jnp.where(first, pltpu.roll(kn, kw - 16, axis=1), pltpu.roll(kn, 16, axis=1))
    k_ref[0] = (kn * ckn_ref[...] + partner * skn_ref[...]).astype(BF16)
    o += kw

    for lb_ref, kk_ref, lf_ref in ((lbf_ref, kf_ref, lff_ref), (lbb_ref, kb_ref, lfb_ref)):
        z = un[:, o:o + hw]
        lb = lb_ref[...]
        kk_ref[0] = (1.0 - lb) * jax.nn.sigmoid(-z)
        lf_ref[0] = jnp.log(lb + (1.0 - lb) * jax.nn.sigmoid(z))
        o += hw
    v_ref[0] = un[:, o:o + hw].astype(BF16)
    hq_ref[0] = un[:, o + hw:o + 2 * hw]
    g_ref[0] = un[:, o + 2 * hw:o + 3 * hw]

    ut = _dot_nt(wt_ref[...], h)
    qw = ATT_HEADS * HEAD_DIM
    uq = ut[0:qw].reshape(ATT_HEADS, HEAD_DIM, tm)
    msq = jnp.mean(uq * uq, axis=1, keepdims=True)
    qn = uq * lax.rsqrt(msq + EPS) * gq_ref[...][None]
    cr, sr = cqt_ref[0:16][None], sqt_ref[0:16][None]
    cc, sc = cqt_ref[16:32][None], sqt_ref[16:32][None]
    x1r, x2r, x1c, x2c = qn[:, 0:16], qn[:, 16:32], qn[:, 32:48], qn[:, 48:64]
    qr = jnp.concatenate([x1r * cr - x2r * sr, x1r * sr + x2r * cr,
                          x1c * cc - x2c * sc, x1c * sc + x2c * cc], axis=1)
    qt_ref[0] = (qr * (HEAD_DIM ** -0.5 * LOG2E)).astype(BF16)

    uv = ut[qw:qw + kw].reshape(ATT_KV_HEADS, HEAD_DIM, tm)
    ones_row = lax.broadcasted_iota(jnp.int32, (ATT_KV_HEADS, 16, tm), 1) == 0
    aug = jnp.where(ones_row, 1.0, 0.0).astype(F32)
    vaug = jnp.concatenate([uv, aug], axis=1).astype(BF16)
    for cb in range(tm // tkv):
        vt_ref[0, :, cb] = vaug[:, :, cb * tkv:(cb + 1) * tkv]


def _inproj(xs, mod4, l, nb, wn, wt, gk, ckn, skn, gq, cqt, sqt, lbf, lbb, lc, tm, tkv):
    b, t, d = xs.shape
    n6 = mod4.shape[-1]
    fw = FOURIER_GD * 4
    hw = lbf.shape[-1]
    kw = ATT_KV_HEADS * HEAD_DIM
    tok = lambda w, dt: jax.ShapeDtypeStruct((b, t, w), dt)
    tok_spec = lambda w: pl.BlockSpec((1, tm, w), lambda bi, i: (bi, i, 0))
    out_shape = [tok(fw, F32), tok(kw, BF16), tok(hw, F32), tok(hw, F32), tok(hw, F32), tok(hw, F32),
                 tok(hw, BF16), tok(hw, F32), tok(hw, F32),
                 jax.ShapeDtypeStruct((b, ATT_HEADS, HEAD_DIM, t), BF16),
                 jax.ShapeDtypeStruct((b, ATT_KV_HEADS, t // tkv, HEAD_DIM + 16, tkv), BF16)]
    out_specs = [tok_spec(fw), tok_spec(kw)] + [tok_spec(hw)] * 7 + [
        pl.BlockSpec((1, ATT_HEADS, HEAD_DIM, tm), lambda bi, i: (bi, 0, 0, i)),
        pl.BlockSpec((1, ATT_KV_HEADS, tm // tkv, HEAD_DIM + 16, tkv), lambda bi, i: (bi, 0, i, 0, 0))]
    in_specs = [
        pl.BlockSpec((1, tm, d), lambda bi, i: (bi, i, 0)),
        pl.BlockSpec((1, 1, 1, n6), lambda bi, i: (l, bi, 0, 0)),
        pl.BlockSpec((1, 1, 1, n6), lambda bi, i: (l, nb, 0, 0)),
        _resident(wn.shape), _resident(wt.shape), _resident(gk.shape),
        pl.BlockSpec((tm, kw), lambda bi, i: (i, 0)),
        pl.BlockSpec((tm, kw), lambda bi, i: (i, 0)),
        _resident(gq.shape),
        pl.BlockSpec((32, tm), lambda bi, i: (0, i)),
        pl.BlockSpec((32, tm), lambda bi, i: (0, i)),
        _resident(lbf.shape), _resident(lbb.shape),
    ]
    return pl.pallas_call(
        functools.partial(_inproj_kernel, lc=lc, tm=tm, tkv=tkv, d=d, fw=fw, hw=hw),
        grid=(b, t // tm),
        in_specs=in_specs, out_specs=out_specs, out_shape=out_shape,
        name="inproj",
        compiler_params=pltpu.CompilerParams(
            dimension_semantics=("parallel", "parallel"),
            vmem_limit_bytes=V7X_VMEM_LIMIT),
    )(xs, mod4, mod4, wn, wt, gk, ckn, skn, gq, cqt, sqt, lbf, lbb)


def _fourier_kernel(z_ref, f1_ref, twc_ref, tws_ref, c2_ref, s2_ref, cs_ref, cl_ref,
                    y_ref, a1_ref, p_ref, *, lc, n1, scale_c, scale_x):
    n2 = DFT_N2
    cs = cs_ref[...]

    zc = z_ref[0, 0:lc, :].astype(BF16)
    pc = _dot(cl_ref[...], zc)
    pcat = jnp.concatenate([pc[0:lc], pc[lc:2 * lc]], axis=1).astype(BF16)
    y_ref[0, 0:lc, :] = (_dot(pcat, cs) * scale_c).astype(y_ref.dtype)

    f1 = f1_ref[...]

    def stage1(j, carry):
        xj = z_ref[0, pl.ds(lc + j, n1, stride=n2), :].astype(BF16)
        a1_ref[pl.ds(pl.multiple_of(j * 2 * n1, 2 * n1), 2 * n1), :] = _dot(f1, xj)
        return carry

    lax.fori_loop(0, n2, stage1, 0)

    c2 = c2_ref[...]
    s2 = s2_ref[...]

    def stage2(k1, carry):
        ar = a1_ref[pl.ds(k1, n2, stride=2 * n1), :]
        ai = a1_ref[pl.ds(n1 + k1, n2, stride=2 * n1), :]
        twc = twc_ref[pl.ds(k1, 1), :]
        tws = tws_ref[pl.ds(k1, 1), :]
        gr = c2 * twc - s2 * tws
        gi = -(s2 * twc + c2 * tws)
        gm = jnp.concatenate([jnp.concatenate([gr, -gi], axis=1),
                              jnp.concatenate([gi, gr], axis=1)], axis=0).astype(BF16)
        rhs = jnp.concatenate([ar, ai], axis=0).astype(BF16)
        pp = _dot(gm, rhs)
        pcat2 = jnp.concatenate([pp[0:n2], pp[n2:2 * n2]], axis=1).astype(BF16)
        p_ref[pl.ds(pl.multiple_of(k1 * n2, n2), n2), :] = _dot(pcat2, cs) * scale_x
        return carry

    lax.fori_loop(0, n1, stage2, 0)

    def stage3(k2, carry):
        blk = p_ref[pl.ds(k2, n1, stride=n2), :]
        y_ref[0, pl.ds(pl.multiple_of(lc + k2 * n1, n1), n1), :] = blk.astype(y_ref.dtype)
        return carry

    lax.fori_loop(0, n2, stage3, 0)


def _fourier_consts(lc, s):
    n2 = DFT_N2
    n1 = s // n2
    f64 = np.float64
    k = np.arange(n1, dtype=f64)
    a1 = 2 * np.pi * np.outer(k, k) / n1
    f1 = np.concatenate([np.cos(a1), -np.sin(a1)], axis=0)
    at = 2 * np.pi * np.outer(np.arange(n1, dtype=f64), np.arange(n2, dtype=f64)) / s
    k2 = np.arange(n2, dtype=f64)
    a2 = 2 * np.pi * np.outer(k2, k2) / n2
    w = 2 * FOURIER_GD
    ch = np.arange(w)
    same = (ch[:, None] // FOURIER_GD) == (ch[None, :] // FOURIER_GD)
    ag = 2 * np.pi * np.outer(ch % FOURIER_GD, ch % FOURIER_GD) / FOURIER_GD
    cs = np.concatenate([np.where(same, np.cos(ag), 0.0), np.where(same, np.sin(ag), 0.0)], axis=0)
    kc = np.arange(lc, dtype=f64)
    al = 2 * np.pi * np.outer(kc, kc) / lc
    cl = np.concatenate([np.cos(al), -np.sin(al)], axis=0)
    return dict(
        f1=jnp.asarray(f1, BF16), twc=jnp.asarray(np.cos(at), F32), tws=jnp.asarray(np.sin(at), F32),
        c2=jnp.asarray(np.cos(a2), F32), s2=jnp.asarray(np.sin(a2), F32),
        cs=jnp.asarray(cs, BF16), cl=jnp.asarray(cl, BF16), n1=n1)


def _fourier(fz, fc, lc):
    b, t, fw = fz.shape
    s = t - lc
    n1 = fc["n1"]
    w = 2 * FOURIER_GD
    consts = [fc[k] for k in ("f1", "twc", "tws", "c2", "s2", "cs", "cl")]
    return pl.pallas_call(
        functools.partial(_fourier_kernel, lc=lc, n1=n1,
                          scale_c=float((lc * FOURIER_GD) ** -0.5),
                          scale_x=float((s * FOURIER_GD) ** -0.5)),
        grid=(b, fw // w),
        in_specs=[pl.BlockSpec((1, t, w), lambda bi, hi: (bi, 0, hi))] +
                 [_resident(c.shape) for c in consts],
        out_specs=pl.BlockSpec((1, t, w), lambda bi, hi: (bi, 0, hi)),
        out_shape=jax.ShapeDtypeStruct((b, t, fw), BF16),
        scratch_shapes=[pltpu.VMEM((DFT_N2 * 2 * n1, w), F32), pltpu.VMEM((n1 * DFT_N2, w), F32)],
        name="fourier",
        compiler_params=pltpu.CompilerParams(
            dimension_semantics=("parallel", "parallel"),
            vmem_limit_bytes=V7X_VMEM_LIMIT),
    )(fz, *consts)


def _place_queries(q_ref, qs_ref):
    g = pl.program_id(1)
    qcat = jnp.concatenate([q_ref[0, j] for j in range(ATT_GROUP)], axis=1)
    zero = jnp.zeros_like(qcat)
    for gg in range(ATT_KV_HEADS):
        @pl.when(g == gg)
        def _place():
            for hh in range(ATT_KV_HEADS):
                qs_ref[hh * HEAD_DIM:(hh + 1) * HEAD_DIM, :] = qcat if hh == gg else zero


def _score_block(k_ref, qs_ref, blk, tk, lc, masked):
    kb = k_ref[0, pl.ds(pl.multiple_of(blk * tk, tk), tk), :]
    s = _dot(kb, qs_ref[...])
    if masked:
        kidx = blk * tk + lax.broadcasted_iota(jnp.int32, (tk, 1), 0)
        s = jnp.where(kidx < lc, s, NEG_BIG)
    return s


def _attn_finish(acc_ref, o_ref, tq):
    acc = acc_ref[...]
    o = acc[0:HEAD_DIM] / acc[HEAD_DIM:HEAD_DIM + 1]
    o_ref[0] = jnp.concatenate([o[:, j * tq:(j + 1) * tq] for j in range(ATT_GROUP)],
                               axis=0).astype(o_ref.dtype)


def _attn_bounded_kernel(q_ref, k_ref, v_ref, o_ref, qs_ref, acc_ref, *, lc, tq, tk, nqc, nkc, nk):
    qi = pl.program_id(2)
    _place_queries(q_ref, qs_ref)
    acc_ref[...] = jnp.zeros(acc_ref.shape, F32)

    tkv = v_ref.shape[-1]
    nsub = tk // tkv

    def block(blk, masked=False):
        p = jnp.exp2(_score_block(k_ref, qs_ref, blk, tk, lc, masked)).astype(BF16)
        pv = _dot(v_ref[0, 0, blk * nsub], p[0:tkv])
        for c in range(1, nsub):
            pv = pv + _dot(v_ref[0, 0, blk * nsub + c], p[c * tkv:(c + 1) * tkv])
        acc_ref[...] += pv

    @pl.when(qi < nqc)
    def _ctx():
        for blk in range(nkc):
            block(blk, masked=True)

    @pl.when(qi >= nqc)
    def _lat():
        def step(i, carry):
            block(i)
            return carry

        lax.fori_loop(0, nk, step, 0)

    _attn_finish(acc_ref, o_ref, tq)


def _attn_kernel(q_ref, k_ref, v_ref, o_ref, qs_ref, s0_ref, s1_ref, mb0_ref, mb1_ref, p0_ref, p1_ref,
                 al0_ref, al1_ref, m_ref, acc_ref, *, lc, tq, tk, nqc, nkc, nk):
    s_ref, mb_ref = (s0_ref, s1_ref), (mb0_ref, mb1_ref)
    p_ref, al_ref = (p0_ref, p1_ref), (al0_ref, al1_ref)
    qi = pl.program_id(2)

    _place_queries(q_ref, qs_ref)
    m_ref[...] = jnp.full(m_ref.shape, NEG_BIG, F32)
    acc_ref[...] = jnp.zeros(acc_ref.shape, F32)

    def scores(blk, slot, masked=False):
        s = _score_block(k_ref, qs_ref, blk, tk, lc, masked)
        s_ref[slot][...] = s
        mb_ref[slot][...] = jnp.max(s, axis=0, keepdims=True)

    def probs(slot):
        m_prev = m_ref[...]
        m_new = jnp.maximum(m_prev, mb_ref[slot][...])
        al_ref[slot][...] = jnp.exp2(m_prev - m_new)
        p_ref[slot][...] = jnp.exp2(s_ref[slot][...] - m_new).astype(BF16)
        m_ref[...] = m_new

    def weighted(blk, slot):
        acc_ref[...] = al_ref[slot][...] * acc_ref[...] + _dot(v_ref[0, 0, blk], p_ref[slot][...])

    @pl.when(qi < nqc)
    def _ctx():
        for blk in range(nkc):
            scores(blk, 0, masked=True)
            probs(0)
            weighted(blk, 0)

    @pl.when(qi >= nqc)
    def _lat():
        def steady(t, slot):
            scores(t + 2, slot)
            weighted(t, slot)
            probs(1 - slot)

        scores(0, 0)
        scores(1, 1)
        probs(0)
        npairs = (nk - 2) // 2

        def pair(i, carry):
            steady(2 * i, 0)
            steady(2 * i + 1, 1)
            return carry

        lax.fori_loop(0, npairs, pair, 0)
        if (nk - 2) % 2:
            steady(2 * npairs, 0)
        weighted(nk - 2, (nk - 2) % 2)
        probs((nk - 1) % 2)
        weighted(nk - 1, (nk - 1) % 2)

    _attn_finish(acc_ref, o_ref, tq)


def _attention(qt, kn, vt, lc, tq, bounded):
    b, _, _, t = qt.shape
    nkv, tkv = vt.shape[2], vt.shape[4]
    tk = _pick_tile(t, ATTN_BOUNDED_KEYS, mult=tkv) if bounded else tkv
    nk = t // tk
    nq = t // tq
    nqc = lc // tq
    nkc = -(-lc // tk)
    assert nk >= 2
    wide = ATT_GROUP * tq
    qs_scratch = pltpu.VMEM((ATT_KV_HEADS * HEAD_DIM, wide), BF16)
    acc_scratch = pltpu.VMEM((HEAD_DIM + 16, wide), F32)
    if bounded:
        body, scratch = _attn_bounded_kernel, [qs_scratch, acc_scratch]
    else:
        body = _attn_kernel
        scratch = [qs_scratch,
                   pltpu.VMEM((tk, wide), F32), pltpu.VMEM((tk, wide), F32),
                   pltpu.VMEM((1, wide), F32), pltpu.VMEM((1, wide), F32),
                   pltpu.VMEM((tk, wide), BF16), pltpu.VMEM((tk, wide), BF16),
                   pltpu.VMEM((1, wide), F32), pltpu.VMEM((1, wide), F32),
                   pltpu.VMEM((1, wide), F32),
                   acc_scratch]
    return pl.pallas_call(
        functools.partial(body, lc=lc, tq=tq, tk=tk, nqc=nqc, nkc=nkc, nk=nk),
        grid=(b, ATT_KV_HEADS, nq),
        in_specs=[
            pl.BlockSpec((1, ATT_GROUP, HEAD_DIM, tq), lambda bi, g, qi: (bi, g, 0, qi)),
            pl.BlockSpec((1, t, ATT_KV_HEADS * HEAD_DIM), lambda bi, g, qi: (bi, 0, 0)),
            pl.BlockSpec((1, 1, nkv, HEAD_DIM + 16, tkv), lambda bi, g, qi: (bi, g, 0, 0, 0)),
        ],
        out_specs=pl.BlockSpec((1, ATT_GROUP * HEAD_DIM, tq), lambda bi, g, qi: (bi, g, qi)),
        out_shape=jax.ShapeDtypeStruct((b, ATT_HEADS * HEAD_DIM, t), BF16),
        scratch_shapes=scratch,
        name="attn_bounded" if bounded else "attn_online",
        compiler_params=pltpu.CompilerParams(
            dimension_semantics=("parallel", "parallel", "arbitrary"),
            vmem_limit_bytes=V7X_VMEM_LIMIT),
    )(qt, kn, vt)


def _hgrn_consts():
    c = HG_CHUNK
    t = np.arange(c)
    tt, rr = t[:, None], t[None, :]
    mats, valids, masks = [], [], []
    for rev in (False, True):
        rows = [(rr >= tt) if rev else (rr <= tt),
                (rr < tt) if rev else (rr > tt),
                np.ones((8, c), bool)]
        valid = [np.ones(c, bool), np.ones(c, bool), np.ones(8, bool)]
        lvl_masks = []
        for bsz in HG_LEVELS:
            blk, pos = t // (2 * bsz), t % (2 * bsz)
            if rev:
                qv, kv = pos < bsz, pos >= bsz
                ref = blk * 2 * bsz + bsz
                mq = (rr >= tt) & (rr < ref[:, None])
                mk = (rr >= ref[:, None]) & (rr < tt)
            else:
                qv, kv = pos >= bsz, pos < bsz
                ref = blk * 2 * bsz + bsz - 1
                mq = (rr > ref[:, None]) & (rr <= tt)
                mk = (rr > tt) & (rr <= ref[:, None])
            rows += [mq & qv[:, None], mk & kv[:, None]]
            valid += [qv, kv]
            lvl_masks.append((blk[:, None] == blk[None, :]) & qv[:, None] & kv[None, :])
        lvl_masks.append(tt == rr)
        mats.append(np.concatenate(rows, axis=0))
        valids.append(np.concatenate(valid)[:, None])
        masks.append(np.stack([np.tile(m, (HG_HEADS, 1)) for m in lvl_masks]))
    return (jnp.asarray(np.stack(mats), BF16), jnp.asarray(np.stack(valids), F32),
            jnp.asarray(np.stack(masks), F32))


def _hgrn_chunk(d, r0, kk_ref, lf_ref, v_ref, q_ref, cm_ref, rv_ref, am_ref, o_ref, s_ref, hw):
    c = HG_CHUNK
    dk = hw // HG_HEADS
    rows = pl.ds(r0, c)
    lf = lf_ref[0, rows, :]
    cm = cm_ref[d]
    hi, mid, lo = _split3(lf)
    e = _dot(cm, hi) + _dot(cm, mid) + _dot(cm, lo)
    w = jnp.where(rv_ref[d] > 0.0, jnp.exp(e), 0.0)
    q = q_ref[0, rows, :]
    k = kk_ref[0, rows, :]
    v = v_ref[0, rows, :]
    qdec = q * w[0:c]
    kdec = k * w[c:2 * c]
    ds = w[2 * c:2 * c + 1]

    lane_head = lax.broadcasted_iota(jnp.int32, (1, hw), 1) // dk
    heads = [lane_head == hh for hh in range(HG_HEADS)]

    att = jnp.zeros((HG_HEADS * c, c), F32)
    base = 2 * c + 8
    for lv in range(len(HG_LEVELS) + 1):
        if lv < len(HG_LEVELS):
            qd = q * w[base + 2 * lv * c:base + (2 * lv + 1) * c]
            kd = k * w[base + (2 * lv + 1) * c:base + (2 * lv + 2) * c]
        else:
            qd, kd = q, k
        q4 = jnp.concatenate([jnp.where(hm, qd, 0.0) for hm in heads], axis=0).astype(BF16)
        att = att + am_ref[d, lv] * _dot_nt(q4, kd.astype(BF16))
    ov = _dot(att.astype(BF16), v)
    o = jnp.zeros((c, hw), F32)
    for hh, hm in enumerate(heads):
        o = o + jnp.where(hm, ov[hh * c:(hh + 1) * c], 0.0)

    st = s_ref[...]
    o = o + _dot_nt(qdec.astype(BF16), st.astype(BF16))
    o_ref[0, rows, :] = o
    u = _dot_tn(v, kdec.astype(BF16))
    rh = lax.broadcasted_iota(jnp.int32, (hw, 1), 0) // dk
    s_ref[...] = st * ds + jnp.where(rh == lane_head, u, 0.0)


def _hgrn_kernel(kf_ref, lff_ref, vf_ref, qf_ref, kb_ref, lfb_ref, vb_ref, qb_ref,
                 cm_ref, rv_ref, am_ref, of_ref, ob_ref, sf_ref, sb_ref, *, nchunk, hw):
    @pl.when(pl.program_id(1) == 0)
    def _init():
        sf_ref[...] = jnp.zeros(sf_ref.shape, F32)
        sb_ref[...] = jnp.zeros(sb_ref.shape, F32)

    for jj in range(nchunk):
        _hgrn_chunk(0, jj * HG_CHUNK, kf_ref, lff_ref, vf_ref, qf_ref,
                    cm_ref, rv_ref, am_ref, of_ref, sf_ref, hw)
        _hgrn_chunk(1, (nchunk - 1 - jj) * HG_CHUNK, kb_ref, lfb_ref, vb_ref, qb_ref,
                    cm_ref, rv_ref, am_ref, ob_ref, sb_ref, hw)


def _hgrn(kf, lff, kb, lfb, v, hq, hc, lc, nchunk):
    b, t, hw = kf.shape
    tb = nchunk * HG_CHUNK
    nblk, ncb = t // tb, lc // tb
    cm, rv, am = hc
    fwd = lambda bi, p: (bi, p, 0)
    bwd = lambda bi, p: (bi, jnp.where(p < ncb, ncb - 1 - p, nblk - 1 - (p - ncb)), 0)
    blk = lambda im: pl.BlockSpec((1, tb, hw), im)
    return pl.pallas_call(
        functools.partial(_hgrn_kernel, nchunk=nchunk, hw=hw),
        grid=(b, nblk),
        in_specs=[blk(fwd)] * 4 + [blk(bwd)] * 4 + [_resident(cm.shape), _resident(rv.shape),
                                                   _resident(am.shape)],
        out_specs=[blk(fwd), blk(bwd)],
        out_shape=[jax.ShapeDtypeStruct((b, t, hw), F32)] * 2,
        scratch_shapes=[pltpu.VMEM((hw, hw), F32), pltpu.VMEM((hw, hw), F32)],
        name="hgrn",
        compiler_params=pltpu.CompilerParams(
            dimension_semantics=("parallel", "arbitrary"),
            vmem_limit_bytes=V7X_VMEM_LIMIT),
    )(kf, lff, v, hq, kb, lfb, v, hq, cm, rv, am)


def _outproj_kernel(x_ref, mb_ref, mc_ref, y_ref, at_ref, of_ref, ob_ref, g_ref, wf_ref, wo_ref,
                    gn_ref, o_ref, *, lc, tm, d, fw, aw, hw):
    i = pl.program_id(1)
    row = i * tm + lax.broadcasted_iota(jnp.int32, (tm, 1), 0)
    is_ctx = row < lc
    mb, mc = mb_ref[0, 0], mc_ref[0, 0]
    gate = jnp.where(is_ctx, mc[:, 2 * d:3 * d], mb[:, 2 * d:3 * d])

    fx = _dot(y_ref[0], wf_ref[...]).astype(BF16)
    mix = _dot(fx, wo_ref[0:fw, :])
    mix = mix + _dot_tn(at_ref[0], wo_ref[fw:fw + aw, :])

    o = of_ref[0] + ob_ref[0]
    dv = hw // HG_HEADS
    lane_head = lax.broadcasted_iota(jnp.int32, (1, hw), 1) // dv
    o2 = o * o
    ms = jnp.zeros_like(o)
    for hh in range(HG_HEADS):
        sel = lane_head == hh
        ssh = jnp.sum(jnp.where(sel, o2, 0.0), axis=-1, keepdims=True)
        ms = jnp.where(sel, ssh * (1.0 / dv), ms)
    gg = g_ref[0]
    rx = (o * lax.rsqrt(ms + EPS) * gn_ref[...]) * (gg * jax.nn.sigmoid(gg))
    mix = mix + _dot(rx.astype(BF16), wo_ref[fw + aw:fw + aw + hw, :])
    o_ref[0] = x_ref[0] + gate * mix


def _outproj(xs, mod4, l, nb, y, at, of, ob, g, wf, wo, gn, lc, tm):
    b, t, d = xs.shape
    n6 = mod4.shape[-1]
    fw, hw = y.shape[-1], of.shape[-1]
    aw = at.shape[1]
    tok = lambda w: pl.BlockSpec((1, tm, w), lambda bi, i: (bi, i, 0))
    return pl.pallas_call(
        functools.partial(_outproj_kernel, lc=lc, tm=tm, d=d, fw=fw, aw=aw, hw=hw),
        grid=(b, t // tm),
        in_specs=[tok(d),
                  pl.BlockSpec((1, 1, 1, n6), lambda bi, i: (l, bi, 0, 0)),
                  pl.BlockSpec((1, 1, 1, n6), lambda bi, i: (l, nb, 0, 0)),
                  tok(fw),
                  pl.BlockSpec((1, aw, tm), lambda bi, i: (bi, 0, i)),
                  tok(hw), tok(hw), tok(hw),
                  _resident(wf.shape), _resident(wo.shape), _resident(gn.shape)],
        out_specs=tok(d),
        out_shape=jax.ShapeDtypeStruct((b, t, d), F32),
        name="outproj",
        compiler_params=pltpu.CompilerParams(
            dimension_semantics=("parallel", "parallel"),
            vmem_limit_bytes=V7X_VMEM_LIMIT),
    )(xs, mod4, mod4, y, at, of, ob, g, wf, wo, gn)


def _ffn_kernel(x_ref, mb_ref, mc_ref, wg_ref, wu_ref, wd_ref, o_ref, *, lc, tm, d, fchunk):
    i = pl.program_id(1)
    x = x_ref[0]
    row = i * tm + lax.broadcasted_iota(jnp.int32, (tm, 1), 0)
    is_ctx = row < lc
    mb, mc = mb_ref[0, 0], mc_ref[0, 0]
    h = _modulated(x, mb, mc, is_ctx, d, 3, 4).astype(BF16)
    gate = jnp.where(is_ctx, mc[:, 5 * d:6 * d], mb[:, 5 * d:6 * d])
    dff = wg_ref.shape[1]
    acc = jnp.zeros((tm, d), F32)
    for c0 in range(0, dff, fchunk):
        a = _dot(h, wg_ref[:, c0:c0 + fchunk])
        u = _dot(h, wu_ref[:, c0:c0 + fchunk])
        act = (a * jax.nn.sigmoid(a) * u).astype(BF16)
        acc = acc + _dot(act, wd_ref[c0:c0 + fchunk, :])
    o_ref[0] = x + gate * acc


def _ffn(xs, mod4, l, nb, wg, wu, wd, lc, tm):
    b, t, d = xs.shape
    n6 = mod4.shape[-1]
    dff = wg.shape[1]
    fchunk = _pick_tile(dff, 768)
    tok = pl.BlockSpec((1, tm, d), lambda bi, i: (bi, i, 0))
    return pl.pallas_call(
        functools.partial(_ffn_kernel, lc=lc, tm=tm, d=d, fchunk=fchunk),
        grid=(b, t // tm),
        in_specs=[tok,
                  pl.BlockSpec((1, 1, 1, n6), lambda bi, i: (l, bi, 0, 0)),
                  pl.BlockSpec((1, 1, 1, n6), lambda bi, i: (l, nb, 0, 0)),
                  _resident(wg.shape), _resident(wu.shape), _resident(wd.shape)],
        out_specs=tok,
        out_shape=jax.ShapeDtypeStruct((b, t, d), F32),
        name="ffn",
        compiler_params=pltpu.CompilerParams(
            dimension_semantics=("parallel", "parallel"),
            vmem_limit_bytes=V7X_VMEM_LIMIT),
    )(xs, mod4, mod4, wg, wu, wd)


def _final_kernel(x_ref, g_ref, o_ref):
    x = x_ref[0]
    o_ref[0] = x * lax.rsqrt(jnp.mean(x * x, axis=-1, keepdims=True) + EPS) * g_ref[...]


def _final_norm(xs, gain, lc, tm):
    b, t, d = xs.shape
    s = t - lc
    off = lc // tm
    return pl.pallas_call(
        _final_kernel,
        grid=(b, s // tm),
        in_specs=[pl.BlockSpec((1, tm, d), lambda bi, i: (bi, i + off, 0)),
                  pl.BlockSpec((1, d), lambda bi, i: (0, 0))],
        out_specs=pl.BlockSpec((1, tm, d), lambda bi, i: (bi, i, 0)),
        out_shape=jax.ShapeDtypeStruct((b, s, d), F32),
        name="final_norm",
        compiler_params=pltpu.CompilerParams(dimension_semantics=("parallel", "parallel")),
    )(xs, gain.reshape(1, d))


def _rope_tables(lc, s):
    half = HEAD_DIM // 2
    n = jnp.arange(s)
    freqs = ROPE_THETA ** (-jnp.arange(0, half, 2, dtype=F32) / half)
    ang_r = (n // GRID_W).astype(F32)[:, None] * freqs
    ang_c = (n % GRID_W).astype(F32)[:, None] * freqs
    pad = lambda a, v: jnp.concatenate([jnp.full((lc, a.shape[1]), v, F32), a], axis=0)
    cr, sr = pad(jnp.cos(ang_r), 1.0), pad(jnp.sin(ang_r), 0.0)
    cc, sc = pad(jnp.cos(ang_c), 1.0), pad(jnp.sin(ang_c), 0.0)
    cos_h = jnp.concatenate([cr, cr, cc, cc], axis=1)
    sin_h = jnp.concatenate([-sr, sr, -sc, sc], axis=1)
    ckn = jnp.tile(cos_h, (1, ATT_KV_HEADS))
    skn = jnp.tile(sin_h, (1, ATT_KV_HEADS))
    cqt = jnp.concatenate([cr, cc], axis=1).T
    sqt = jnp.concatenate([sr, sc], axis=1).T
    return ckn, skn, cqt, sqt


def kernel(x, c, ctx, c_ctx, w_ada, b_ada, w_in, w_four, q_norm, k_norm, hg_lb_logits, hg_norm,
           w_out, w_gate, w_up, w_down, final_norm):
    b, s, d = x.shape
    lc = ctx.shape[1]
    t = lc + s
    depth = w_ada.shape[0]
    fw = w_four.shape[-1]
    hw = hg_lb_logits.shape[-1]
    qw = ATT_HEADS * HEAD_DIM
    kw = ATT_KV_HEADS * HEAD_DIM
    assert w_in.shape[-1] == fw + qw + 2 * kw + 5 * hw
    assert s % (DFT_N2 * 8) == 0 and lc % 128 == 0 and s % GRID_W == 0

    tq = 256 if (lc % 256 == 0 and t % 256 == 0) else 128
    tkv = tq
    tm = _pick_tile(t, 768, mult=tkv)
    nchunk = max(g for g in (4, 2, 1) if lc % (g * HG_CHUNK) == 0 and t % (g * HG_CHUNK) == 0)
    tfin = 256 if (lc % 256 == 0 and s % 256 == 0) else 128

    xs = jnp.concatenate([ctx, x], axis=1)
    r8 = -(-(b + 1) // 8) * 8
    cvec = jnp.concatenate([c, c_ctx[None], jnp.zeros((r8 - b - 1, d), F32)], axis=0)
    mod = _adaln(cvec, w_ada, b_ada)
    mod4 = mod.reshape(depth, r8, 1, mod.shape[-1])

    lb_sm = jax.nn.softmax(hg_lb_logits.astype(F32), axis=1)
    lb_all = jnp.cumsum(lb_sm, axis=1) - lb_sm[:, :1]

    ckn, skn, cqt, sqt = _rope_tables(lc, s)
    fc = _fourier_consts(lc, s)
    hc = _hgrn_consts()

    o_q, o_k, o_v, o_h = fw, fw + qw, fw + qw + kw, fw + qw + 2 * kw
    for l in range(depth):
        wl = w_in[l]
        wn = jnp.concatenate([wl[:, 0:fw], wl[:, o_k:o_v], wl[:, o_h:]], axis=1).astype(BF16)
        wt = jnp.concatenate([wl[:, o_q:o_k], wl[:, o_v:o_h]], axis=1).T.astype(BF16)
        gk = jnp.tile(k_norm[l], ATT_KV_HEADS)[None]
        gq = q_norm[l][:, None]
        gn = jnp.tile(hg_norm[l], HG_HEADS)[None]
        (fz, kn, kf, lff, kb, lfb, v, hq, g, qt, vt) = _inproj(
            xs, mod4, l, b, wn, wt, gk, ckn, skn, gq, cqt, sqt,
            lb_all[0, l][None], lb_all[1, l][None], lc, tm, tkv)
        y = _fourier(fz, fc, lc)
        bound = (HEAD_DIM ** 0.5 * LOG2E * BOUND_SLACK) * jnp.max(jnp.abs(q_norm[l])) * jnp.max(jnp.abs(k_norm[l]))
        at = lax.cond(bound < MAX_SAFE_LOG2_SCORE,
                      functools.partial(_attention, lc=lc, tq=tq, bounded=True),
                      functools.partial(_attention, lc=lc, tq=tq, bounded=False),
                      qt, kn, vt)
        of, ob = _hgrn(kf, lff, kb, lfb, v, hq, hc, lc, nchunk)
        xs = _outproj(xs, mod4, l, b, y, at, of, ob, g, w_four[l].astype(BF16),
                      w_out[l].astype(BF16), gn, lc, tm)
        xs = _ffn(xs, mod4, l, b, w_gate[l].astype(BF16), w_up[l].astype(BF16),
                  w_down[l].astype(BF16), lc, tm)
    return _final_norm(xs, final_norm, lc, tfin)
```

```python
import functools

import numpy as np
import jax
import jax.numpy as jnp
from jax import lax
from jax.experimental import pallas as pl
from jax.experimental.pallas import tpu as pltpu

F32 = jnp.float32
BF16 = jnp.bfloat16

EPS = 1e-6
GRID_W = 64
ROPE_THETA = 10000.0
FOURIER_GD = 64
ATT_HEADS = 8
ATT_KV_HEADS = 2
ATT_GROUP = ATT_HEADS // ATT_KV_HEADS
HEAD_DIM = 64
HG_HEADS = 4
HG_CHUNK = 64
HG_LEVELS = (32, 16, 8, 4, 2, 1)
DFT_N2 = 128
DFT_ROW_PAD = 8
NEG_BIG = -1e30
LOG2E = 1.4426950408889634
BOUND_SLACK = 1.05
MAX_SAFE_LOG2_SCORE = 100.0
ATTN_BOUNDED_KEYS = 2816
V7X_VMEM_LIMIT = 56 * 1024 * 1024


def _dot(a, b):
    return jnp.dot(a, b, preferred_element_type=F32)


def _dot_nt(a, b):
    return lax.dot_general(a, b, (((1,), (1,)), ((), ())), preferred_element_type=F32)


def _dot_tn(a, b):
    return lax.dot_general(a, b, (((0,), (0,)), ((), ())), preferred_element_type=F32)


def _split3(x):
    hi = x.astype(BF16)
    r1 = x - hi.astype(F32)
    mid = r1.astype(BF16)
    lo = (r1 - mid.astype(F32)).astype(BF16)
    return hi, mid, lo


def _pick_tile(n, cap, mult=128):
    best = None
    for t in range(mult, min(n, cap) + 1, mult):
        if n % t == 0:
            best = t
    if best is None:
        raise ValueError(f"no tile for {n}")
    return best


def _resident(shape):
    nd = len(shape)
    return pl.BlockSpec(shape, lambda *_: (0,) * nd, pipeline_mode=pl.Buffered(1))


def _adaln_kernel(c_ref, w_ref, b_ref, o_ref):
    cv = c_ref[...]
    a = cv * jax.nn.sigmoid(cv)
    w = w_ref[0]
    a_hi = a.astype(BF16)
    a_lo = (a - a_hi.astype(F32)).astype(BF16)
    w_hi = w.astype(BF16)
    w_lo = (w - w_hi.astype(F32)).astype(BF16)
    acc = _dot(a_hi, w_hi) + _dot(a_lo, w_hi) + _dot(a_hi, w_lo)
    o_ref[0] = acc + b_ref[0]


def _adaln(cvec, w_ada, b_ada):
    depth, d, n6 = w_ada.shape
    r8 = cvec.shape[0]
    tn = _pick_tile(n6, 1536)
    return pl.pallas_call(
        _adaln_kernel,
        grid=(depth, n6 // tn),
        in_specs=[
            pl.BlockSpec((r8, d), lambda l, j: (0, 0)),
            pl.BlockSpec((1, d, tn), lambda l, j: (l, 0, j)),
            pl.BlockSpec((1, 1, tn), lambda l, j: (l, 0, j)),
        ],
        out_specs=pl.BlockSpec((1, r8, tn), lambda l, j: (l, 0, j)),
        out_shape=jax.ShapeDtypeStruct((depth, r8, n6), F32),
        name="adaln",
        compiler_params=pltpu.CompilerParams(
            dimension_semantics=("parallel", "parallel"),
            vmem_limit_bytes=V7X_VMEM_LIMIT),
    )(cvec, w_ada, b_ada.reshape(depth, 1, n6))


def _modulated(x, mb, mc, is_ctx, d, k_shift, k_scale):
    r = lax.rsqrt(jnp.mean(x * x, axis=-1, keepdims=True) + EPS)
    sh = jnp.where(is_ctx, mc[:, k_shift * d:(k_shift + 1) * d], mb[:, k_shift * d:(k_shift + 1) * d])
    sc = jnp.where(is_ctx, mc[:, k_scale * d:(k_scale + 1) * d], mb[:, k_scale * d:(k_scale + 1) * d])
    return (x * r) * (1.0 + sc) + sh


def _inproj_kernel(x_ref, mb_ref, mc_ref, wn_ref, wt_ref, gk_ref, ckn_ref, skn_ref,
                   gq_ref, cqt_ref, sqt_ref, lbf_ref, lbb_ref,
                   fz_ref, k_ref, kf_ref, lff_ref, kb_ref, lfb_ref, v_ref, hq_ref, g_ref,
                   qt_ref, vt_ref, *, lc, tm, tkv, d, fw, hw):
    i = pl.program_id(1)
    x = x_ref[0]
    row = i * tm + lax.broadcasted_iota(jnp.int32, (tm, 1), 0)
    is_ctx = row < lc
    h = _modulated(x, mb_ref[0, 0], mc_ref[0, 0], is_ctx, d, 0, 1).astype(BF16)

    un = _dot(h, wn_ref[...])
    fz_ref[0] = un[:, 0:fw]

    o = fw
    kw = ATT_KV_HEADS * HEAD_DIM
    uk = un[:, o:o + kw]
    lane = lax.broadcasted_iota(jnp.int32, (1, kw), 1)
    uk2 = uk * uk
    ms = jnp.zeros_like(uk)
    for hh in range(ATT_KV_HEADS):
        sel = (lane // HEAD_DIM) == hh
        ssh = jnp.sum(jnp.where(sel, uk2, 0.0), axis=-1, keepdims=True)
        ms = jnp.where(sel, ssh * (1.0 / HEAD_DIM), ms)
    kn = uk * lax.rsqrt(ms + EPS) * gk_ref[...]
    first = (lane % 32) < 16
    partner = jnp.where(first, pltpu.roll(kn, kw - 16, axis=1), pltpu.roll(kn, 16, axis=1))
    k_ref[0] = (kn * ckn_ref[...] + partner * skn_ref[...]).astype(BF16)
    o += kw

    for lb_ref, kk_ref, lf_ref in ((lbf_ref, kf_ref, lff_ref), (lbb_ref, kb_ref, lfb_ref)):
        z = un[:, o:o + hw]
        lb = lb_ref[...]
        kk_ref[0] = (1.0 - lb) * jax.nn.sigmoid(-z)
        lf_ref[0] = jnp.log(lb + (1.0 - lb) * jax.nn.sigmoid(z))
        o += hw
    v_ref[0] = un[:, o:o + hw].astype(BF16)
    hq_ref[0] = un[:, o + hw:o + 2 * hw]
    g_ref[0] = un[:, o + 2 * hw:o + 3 * hw]

    ut = _dot_nt(wt_ref[...], h)
    qw = ATT_HEADS * HEAD_DIM
    uq = ut[0:qw].reshape(ATT_HEADS, HEAD_DIM, tm)
    msq = jnp.mean(uq * uq, axis=1, keepdims=True)
    qn = uq * lax.rsqrt(msq + EPS) * gq_ref[...][None]
    cr, sr = cqt_ref[0:16][None], sqt_ref[0:16][None]
    cc, sc = cqt_ref[16:32][None], sqt_ref[16:32][None]
    x1r, x2r, x1c, x2c = qn[:, 0:16], qn[:, 16:32], qn[:, 32:48], qn[:, 48:64]
    qr = jnp.concatenate([x1r * cr - x2r * sr, x1r * sr + x2r * cr,
                          x1c * cc - x2c * sc, x1c * sc + x2c * cc], axis=1)
    qt_ref[0] = (qr * (HEAD_DIM ** -0.5 * LOG2E)).astype(BF16)

    uv = ut[qw:qw + kw].reshape(ATT_KV_HEADS, HEAD_DIM, tm)
    ones_row = lax.broadcasted_iota(jnp.int32, (ATT_KV_HEADS, 16, tm), 1) == 0
    aug = jnp.where(ones_row, 1.0, 0.0).astype(F32)
    vaug = jnp.concatenate([uv, aug], axis=1).astype(BF16)
    for cb in range(tm // tkv):
        vt_ref[0, :, cb] = vaug[:, :, cb * tkv:(cb + 1) * tkv]


def _inproj(xs, mod4, l, nb, wn, wt, gk, ckn, skn, gq, cqt, sqt, lbf, lbb, lc, tm, tkv):
    b, t, d = xs.shape
    n6 = mod4.shape[-1]
    fw = FOURIER_GD * 4
    hw = lbf.shape[-1]
    kw = ATT_KV_HEADS * HEAD_DIM
    tok = lambda w, dt: jax.ShapeDtypeStruct((b, t, w), dt)
    tok_spec = lambda w: pl.BlockSpec((1, tm, w), lambda bi, i: (bi, i, 0))
    out_shape = [tok(fw, F32), tok(kw, BF16), tok(hw, F32), tok(hw, F32), tok(hw, F32), tok(hw, F32),
                 tok(hw, BF16), tok(hw, F32), tok(hw, F32),
                 jax.ShapeDtypeStruct((b, ATT_HEADS, HEAD_DIM, t), BF16),
                 jax.ShapeDtypeStruct((b, ATT_KV_HEADS, t // tkv, HEAD_DIM + 16, tkv), BF16)]
    out_specs = [tok_spec(fw), tok_spec(kw)] + [tok_spec(hw)] * 7 + [
        pl.BlockSpec((1, ATT_HEADS, HEAD_DIM, tm), lambda bi, i: (bi, 0, 0, i)),
        pl.BlockSpec((1, ATT_KV_HEADS, tm // tkv, HEAD_DIM + 16, tkv), lambda bi, i: (bi, 0, i, 0, 0))]
    in_specs = [
        pl.BlockSpec((1, tm, d), lambda bi, i: (bi, i, 0)),
        pl.BlockSpec((1, 1, 1, n6), lambda bi, i: (l, bi, 0, 0)),
        pl.BlockSpec((1, 1, 1, n6), lambda bi, i: (l, nb, 0, 0)),
        _resident(wn.shape), _resident(wt.shape), _resident(gk.shape),
        pl.BlockSpec((tm, kw), lambda bi, i: (i, 0)),
        pl.BlockSpec((tm, kw), lambda bi, i: (i, 0)),
        _resident(gq.shape),
        pl.BlockSpec((32, tm), lambda bi, i: (0, i)),
        pl.BlockSpec((32, tm), lambda bi, i: (0, i)),
        _resident(lbf.shape), _resident(lbb.shape),
    ]
    return pl.pallas_call(
        functools.partial(_inproj_kernel, lc=lc, tm=tm, tkv=tkv, d=d, fw=fw, hw=hw),
        grid=(b, t // tm),
        in_specs=in_specs, out_specs=out_specs, out_shape=out_shape,
        name="inproj",
        compiler_params=pltpu.CompilerParams(
            dimension_semantics=("parallel", "parallel"),
            vmem_limit_bytes=V7X_VMEM_LIMIT),
    )(xs, mod4, mod4, wn, wt, gk, ckn, skn, gq, cqt, sqt, lbf, lbb)


def _fourier_kernel(z_ref, f1_ref, twc_ref, tws_ref, c2_ref, s2_ref, cs_ref, cl_ref,
                    y_ref, a1_ref, p_ref, *, lc, n1, scale_c, scale_x):
    n2 = DFT_N2
    cs = cs_ref[...]
    pa, pp = 2 * n1 + DFT_ROW_PAD, n2 + DFT_ROW_PAD

    zc = z_ref[0, 0:lc, :].astype(BF16)
    pc = _dot(cl_ref[...], zc)
    pcat = jnp.concatenate([pc[0:lc], pc[lc:2 * lc]], axis=1).astype(BF16)
    y_ref[0, 0:lc, :] = (_dot(pcat, cs) * scale_c).astype(y_ref.dtype)

    f1 = f1_ref[...]

    def stage1(j, carry):
        xj = z_ref[0, pl.ds(lc + j, n1, stride=n2), :].astype(BF16)
        a1_ref[pl.ds(pl.multiple_of(j * pa, 8), 2 * n1), :] = _dot(f1, xj)
        return carry

    lax.fori_loop(0, n2, stage1, 0, unroll=8)

    c2 = c2_ref[...]
    s2 = s2_ref[...]

    def stage2(k1, carry):
        ar = a1_ref[pl.ds(k1, n2, stride=pa), :]
        ai = a1_ref[pl.ds(n1 + k1, n2, stride=pa), :]
        twc = twc_ref[pl.ds(k1, 1), :]
        tws = tws_ref[pl.ds(k1, 1), :]
        gr = c2 * twc - s2 * tws
        gi = -(s2 * twc + c2 * tws)
        gm = jnp.concatenate([jnp.concatenate([gr, -gi], axis=1),
                              jnp.concatenate([gi, gr], axis=1)], axis=0).astype(BF16)
        rhs = jnp.concatenate([ar, ai], axis=0).astype(BF16)
        pri = _dot(gm, rhs)
        pcat2 = jnp.concatenate([pri[0:n2], pri[n2:2 * n2]], axis=1).astype(BF16)
        p_ref[pl.ds(pl.multiple_of(k1 * pp, 8), n2), :] = _dot(pcat2, cs) * scale_x
        return carry

    lax.fori_loop(0, n1, stage2, 0, unroll=8)

    def stage3(k2, carry):
        blk = p_ref[pl.ds(k2, n1, stride=pp), :]
        y_ref[0, pl.ds(pl.multiple_of(lc + k2 * n1, n1), n1), :] = blk.astype(y_ref.dtype)
        return carry

    lax.fori_loop(0, n2, stage3, 0, unroll=8)


def _fourier_consts(lc, s):
    n2 = DFT_N2
    n1 = s // n2
    f64 = np.float64
    k = np.arange(n1, dtype=f64)
    a1 = 2 * np.pi * np.outer(k, k) / n1
    f1 = np.concatenate([np.cos(a1), -np.sin(a1)], axis=0)
    at = 2 * np.pi * np.outer(np.arange(n1, dtype=f64), np.arange(n2, dtype=f64)) / s
    k2 = np.arange(n2, dtype=f64)
    a2 = 2 * np.pi * np.outer(k2, k2) / n2
    w = 2 * FOURIER_GD
    ch = np.arange(w)
    same = (ch[:, None] // FOURIER_GD) == (ch[None, :] // FOURIER_GD)
    ag = 2 * np.pi * np.outer(ch % FOURIER_GD, ch % FOURIER_GD) / FOURIER_GD
    cs = np.concatenate([np.where(same, np.cos(ag), 0.0), np.where(same, np.sin(ag), 0.0)], axis=0)
    kc = np.arange(lc, dtype=f64)
    al = 2 * np.pi * np.outer(kc, kc) / lc
    cl = np.concatenate([np.cos(al), -np.sin(al)], axis=0)
    return dict(
        f1=jnp.asarray(f1, BF16), twc=jnp.asarray(np.cos(at), F32), tws=jnp.asarray(np.sin(at), F32),
        c2=jnp.asarray(np.cos(a2), F32), s2=jnp.asarray(np.sin(a2), F32),
        cs=jnp.asarray(cs, BF16), cl=jnp.asarray(cl, BF16), n1=n1)


def _fourier(fz, fc, lc):
    b, t, fw = fz.shape
    s = t - lc
    n1 = fc["n1"]
    w = 2 * FOURIER_GD
    consts = [fc[k] for k in ("f1", "twc", "tws", "c2", "s2", "cs", "cl")]
    return pl.pallas_call(
        functools.partial(_fourier_kernel, lc=lc, n1=n1,
                          scale_c=float((lc * FOURIER_GD) ** -0.5),
                          scale_x=float((s * FOURIER_GD) ** -0.5)),
        grid=(b, fw // w),
        in_specs=[pl.BlockSpec((1, t, w), lambda bi, hi: (bi, 0, hi))] +
                 [_resident(c.shape) for c in consts],
        out_specs=pl.BlockSpec((1, t, w), lambda bi, hi: (bi, 0, hi)),
        out_shape=jax.ShapeDtypeStruct((b, t, fw), BF16),
        scratch_shapes=[pltpu.VMEM((DFT_N2 * (2 * n1 + DFT_ROW_PAD), w), F32),
                        pltpu.VMEM((n1 * (DFT_N2 + DFT_ROW_PAD), w), F32)],
        name="fourier",
        compiler_params=pltpu.CompilerParams(
            dimension_semantics=("parallel", "parallel"),
            vmem_limit_bytes=V7X_VMEM_LIMIT),
    )(fz, *consts)


def _place_queries(q_ref, qs_ref):
    g = pl.program_id(1)
    qcat = jnp.concatenate([q_ref[0, j] for j in range(ATT_GROUP)], axis=1)
    zero = jnp.zeros_like(qcat)
    for gg in range(ATT_KV_HEADS):
        @pl.when(g == gg)
        def _place():
            for hh in range(ATT_KV_HEADS):
                qs_ref[hh * HEAD_DIM:(hh + 1) * HEAD_DIM, :] = qcat if hh == gg else zero


def _score_block(k_ref, qs_ref, blk, tk, lc, masked):
    kb = k_ref[0, pl.ds(pl.multiple_of(blk * tk, tk), tk), :]
    s = _dot(kb, qs_ref[...])
    if masked:
        kidx = blk * tk + lax.broadcasted_iota(jnp.int32, (tk, 1), 0)
        s = jnp.where(kidx < lc, s, NEG_BIG)
    return s


def _attn_finish(acc_ref, o_ref, tq):
    acc = acc_ref[...]
    o = acc[0:HEAD_DIM] / acc[HEAD_DIM:HEAD_DIM + 1]
    o_ref[0] = jnp.concatenate([o[:, j * tq:(j + 1) * tq] for j in range(ATT_GROUP)],
                               axis=0).astype(o_ref.dtype)


def _attn_bounded_kernel(q_ref, k_ref, v_ref, o_ref, qs_ref, acc_ref, *, lc, tq, tk, nqc, nkc, nk):
    qi = pl.program_id(2)
    _place_queries(q_ref, qs_ref)
    acc_ref[...] = jnp.zeros(acc_ref.shape, F32)

    tkv = v_ref.shape[-1]
    nsub = tk // tkv

    def block(blk, masked=False):
        p = jnp.exp2(_score_block(k_ref, qs_ref, blk, tk, lc, masked)).astype(BF16)
        pv = _dot(v_ref[0, 0, blk * nsub], p[0:tkv])
        for c in range(1, nsub):
            pv = pv + _dot(v_ref[0, 0, blk * nsub + c], p[c * tkv:(c + 1) * tkv])
        acc_ref[...] += pv

    @pl.when(qi < nqc)
    def _ctx():
        for blk in range(nkc):
            block(blk, masked=True)

    @pl.when(qi >= nqc)
    def _lat():
        def step(i, carry):
            block(i)
            return carry

        lax.fori_loop(0, nk, step, 0)

    _attn_finish(acc_ref, o_ref, tq)


def _attn_kernel(q_ref, k_ref, v_ref, o_ref, qs_ref, s0_ref, s1_ref, mb0_ref, mb1_ref, p0_ref, p1_ref,
                 al0_ref, al1_ref, m_ref, acc_ref, *, lc, tq, tk, nqc, nkc, nk):
    s_ref, mb_ref = (s0_ref, s1_ref), (mb0_ref, mb1_ref)
    p_ref, al_ref = (p0_ref, p1_ref), (al0_ref, al1_ref)
    qi = pl.program_id(2)

    _place_queries(q_ref, qs_ref)
    m_ref[...] = jnp.full(m_ref.shape, NEG_BIG, F32)
    acc_ref[...] = jnp.zeros(acc_ref.shape, F32)

    def scores(blk, slot, masked=False):
        s = _score_block(k_ref, qs_ref, blk, tk, lc, masked)
        s_ref[slot][...] = s
        mb_ref[slot][...] = jnp.max(s, axis=0, keepdims=True)

    def probs(slot):
        m_prev = m_ref[...]
        m_new = jnp.maximum(m_prev, mb_ref[slot][...])
        al_ref[slot][...] = jnp.exp2(m_prev - m_new)
        p_ref[slot][...] = jnp.exp2(s_ref[slot][...] - m_new).astype(BF16)
        m_ref[...] = m_new

    def weighted(blk, slot):
        acc_ref[...] = al_ref[slot][...] * acc_ref[...] + _dot(v_ref[0, 0, blk], p_ref[slot][...])

    @pl.when(qi < nqc)
    def _ctx():
        for blk in range(nkc):
            scores(blk, 0, masked=True)
            probs(0)
            weighted(blk, 0)

    @pl.when(qi >= nqc)
    def _lat():
        def steady(t, slot):
            scores(t + 2, slot)
            weighted(t, slot)
            probs(1 - slot)

        scores(0, 0)
        scores(1, 1)
        probs(0)
        npairs = (nk - 2) // 2

        def pair(i, carry):
            steady(2 * i, 0)
            steady(2 * i + 1, 1)
            return carry

        lax.fori_loop(0, npairs, pair, 0)
        if (nk - 2) % 2:
            steady(2 * npairs, 0)
        weighted(nk - 2, (nk - 2) % 2)
        probs((nk - 1) % 2)
        weighted(nk - 1, (nk - 1) % 2)

    _attn_finish(acc_ref, o_ref, tq)


def _attention(qt, kn, vt, lc, tq, bounded):
    b, _, _, t = qt.shape
    nkv, tkv = vt.shape[2], vt.shape[4]
    tk = _pick_tile(t, ATTN_BOUNDED_KEYS, mult=tkv) if bounded else tkv
    nk = t // tk
    nq = t // tq
    nqc = lc // tq
    nkc = -(-lc // tk)
    wide = ATT_GROUP * tq
    qs_scratch = pltpu.VMEM((ATT_KV_HEADS * HEAD_DIM, wide), BF16)
    acc_scratch = pltpu.VMEM((HEAD_DIM + 16, wide), F32)
    if bounded:
        body, scratch = _attn_bounded_kernel, [qs_scratch, acc_scratch]
    else:
        assert nk >= 2
        body = _attn_kernel
        scratch = [qs_scratch,
                   pltpu.VMEM((tk, wide), F32), pltpu.VMEM((tk, wide), F32),
                   pltpu.VMEM((1, wide), F32), pltpu.VMEM((1, wide), F32),
                   pltpu.VMEM((tk, wide), BF16), pltpu.VMEM((tk, wide), BF16),
                   pltpu.VMEM((1, wide), F32), pltpu.VMEM((1, wide), F32),
                   pltpu.VMEM((1, wide), F32),
                   acc_scratch]
    return pl.pallas_call(
        functools.partial(body, lc=lc, tq=tq, tk=tk, nqc=nqc, nkc=nkc, nk=nk),
        grid=(b, ATT_KV_HEADS, nq),
        in_specs=[
            pl.BlockSpec((1, ATT_GROUP, HEAD_DIM, tq), lambda bi, g, qi: (bi, g, 0, qi)),
            pl.BlockSpec((1, t, ATT_KV_HEADS * HEAD_DIM), lambda bi, g, qi: (bi, 0, 0)),
            pl.BlockSpec((1, 1, nkv, HEAD_DIM + 16, tkv), lambda bi, g, qi: (bi, g, 0, 0, 0)),
        ],
        out_specs=pl.BlockSpec((1, ATT_GROUP * HEAD_DIM, tq), lambda bi, g, qi: (bi, g, qi)),
        out_shape=jax.ShapeDtypeStruct((b, ATT_HEADS * HEAD_DIM, t), BF16),
        scratch_shapes=scratch,
        name="attn_bounded" if bounded else "attn_online",
        compiler_params=pltpu.CompilerParams(
            dimension_semantics=("parallel", "parallel", "arbitrary"),
            vmem_limit_bytes=V7X_VMEM_LIMIT),
    )(qt, kn, vt)


def _hgrn_consts():
    c = HG_CHUNK
    t = np.arange(c)
    tt, rr = t[:, None], t[None, :]
    mats, masks = [], []
    for rev in (False, True):
        mats.append(np.concatenate([(rr >= tt) if rev else (rr <= tt), np.ones((8, c), bool)], axis=0))
        lvl_masks = []
        for bsz in HG_LEVELS:
            blk, pos = t // (2 * bsz), t % (2 * bsz)
            qv = (pos < bsz) if rev else (pos >= bsz)
            lvl_masks.append((blk[:, None] == blk[None, :]) & qv[:, None] & ~qv[None, :])
        lvl_masks.append(tt == rr)
        masks.append(np.stack([np.tile(m, (HG_HEADS, 1)) for m in lvl_masks]))
    return jnp.asarray(np.stack(mats), BF16), jnp.asarray(np.stack(masks), F32)


def _level_reference(cum, bsz, rev):
    c, hw = cum.shape
    i = bsz if rev else bsz - 1
    if 2 * bsz >= 8:
        c3 = cum.reshape(c // (2 * bsz), 2 * bsz, hw)
        return jnp.broadcast_to(c3[:, i:i + 1, :], c3.shape).reshape(c, hw)
    c3 = cum.reshape(c // 8, 8, hw)
    sub = lax.broadcasted_iota(jnp.int32, (1, 8, 1), 1) // (2 * bsz)
    ref = jnp.broadcast_to(c3[:, i:i + 1, :], c3.shape)
    for gidx in range(1, 8 // (2 * bsz)):
        r = gidx * 2 * bsz + i
        ref = jnp.where(sub == gidx, jnp.broadcast_to(c3[:, r:r + 1, :], c3.shape), ref)
    return ref.reshape(c, hw)


def _hgrn_chunk(d, r0, kk_ref, lf_ref, v_ref, q_ref, cm_ref, am_ref, o_ref, s_ref, hw):
    c = HG_CHUNK
    dk = hw // HG_HEADS
    rev = d == 1
    rows = pl.ds(r0, c)
    lf = lf_ref[0, rows, :]
    cm = cm_ref[d]
    hi, mid, lo = _split3(lf)
    e = _dot(cm, hi) + _dot(cm, mid) + _dot(cm, lo)
    cum, tot = e[0:c], e[c:c + 1]
    q = q_ref[0, rows, :]
    k = kk_ref[0, rows, :]
    v = v_ref[0, rows, :]
    qdec = q * jnp.exp(cum)
    kdec = k * jnp.exp(tot - cum)
    ds = jnp.exp(tot)

    lane_head = lax.broadcasted_iota(jnp.int32, (1, hw), 1) // dk
    heads = [lane_head == hh for hh in range(HG_HEADS)]
    tpos = lax.broadcasted_iota(jnp.int32, (c, 1), 0)

    att = jnp.zeros((HG_HEADS * c, c), F32)
    for lv, bsz in enumerate(HG_LEVELS + (0,)):
        if bsz:
            pos = tpos % (2 * bsz)
            is_q = (pos < bsz) if rev else (pos >= bsz)
            diff = cum - _level_reference(cum, bsz, rev)
            w = jnp.exp(jnp.where(is_q, diff, -diff))
            qd = jnp.where(is_q, q * w, 0.0)
            kd = jnp.where(is_q, 0.0, k * w)
        else:
            qd, kd = q, k
        q4 = jnp.concatenate([jnp.where(hm, qd, 0.0) for hm in heads], axis=0).astype(BF16)
        att = att + am_ref[d, lv] * _dot_nt(q4, kd.astype(BF16))
    ov = _dot(att.astype(BF16), v)
    o = jnp.zeros((c, hw), F32)
    for hh, hm in enumerate(heads):
        o = o + jnp.where(hm, ov[hh * c:(hh + 1) * c], 0.0)

    st = s_ref[...]
    o = o + _dot_nt(qdec.astype(BF16), st.astype(BF16))
    o_ref[0, rows, :] = o
    u = _dot_tn(v, kdec.astype(BF16))
    rh = lax.broadcasted_iota(jnp.int32, (hw, 1), 0) // dk
    s_ref[...] = st * ds + jnp.where(rh == lane_head, u, 0.0)


def _hgrn_kernel(kf_ref, lff_ref, vf_ref, qf_ref, kb_ref, lfb_ref, vb_ref, qb_ref,
                 cm_ref, am_ref, of_ref, ob_ref, sf_ref, sb_ref, *, nchunk, hw):
    @pl.when(pl.program_id(1) == 0)
    def _init():
        sf_ref[...] = jnp.zeros(sf_ref.shape, F32)
        sb_ref[...] = jnp.zeros(sb_ref.shape, F32)

    for jj in range(nchunk):
        _hgrn_chunk(0, jj * HG_CHUNK, kf_ref, lff_ref, vf_ref, qf_ref,
                    cm_ref, am_ref, of_ref, sf_ref, hw)
        _hgrn_chunk(1, (nchunk - 1 - jj) * HG_CHUNK, kb_ref, lfb_ref, vb_ref, qb_ref,
                    cm_ref, am_ref, ob_ref, sb_ref, hw)


def _hgrn(kf, lff, kb, lfb, v, hq, hc, lc, nchunk):
    b, t, hw = kf.shape
    tb = nchunk * HG_CHUNK
    nblk, ncb = t // tb, lc // tb
    cm, am = hc
    fwd = lambda bi, p: (bi, p, 0)
    bwd = lambda bi, p: (bi, jnp.where(p < ncb, ncb - 1 - p, nblk - 1 - (p - ncb)), 0)
    blk = lambda im: pl.BlockSpec((1, tb, hw), im)
    return pl.pallas_call(
        functools.partial(_hgrn_kernel, nchunk=nchunk, hw=hw),
        grid=(b, nblk),
        in_specs=[blk(fwd)] * 4 + [blk(bwd)] * 4 + [_resident(cm.shape), _resident(am.shape)],
        out_specs=[blk(fwd), blk(bwd)],
        out_shape=[jax.ShapeDtypeStruct((b, t, hw), F32)] * 2,
        scratch_shapes=[pltpu.VMEM((hw, hw), F32), pltpu.VMEM((hw, hw), F32)],
        name="hgrn",
        compiler_params=pltpu.CompilerParams(
            dimension_semantics=("parallel", "arbitrary"),
            vmem_limit_bytes=V7X_VMEM_LIMIT),
    )(kf, lff, v, hq, kb, lfb, v, hq, cm, am)


def _outproj_kernel(x_ref, mb_ref, mc_ref, y_ref, at_ref, of_ref, ob_ref, g_ref, wf_ref, wo_ref,
                    gn_ref, o_ref, *, lc, tm, d, fw, aw, hw):
    i = pl.program_id(1)
    row = i * tm + lax.broadcasted_iota(jnp.int32, (tm, 1), 0)
    is_ctx = row < lc
    mb, mc = mb_ref[0, 0], mc_ref[0, 0]
    gate = jnp.where(is_ctx, mc[:, 2 * d:3 * d], mb[:, 2 * d:3 * d])

    fx = _dot(y_ref[0], wf_ref[...]).astype(BF16)
    mix = _dot(fx, wo_ref[0:fw, :])
    mix = mix + _dot_tn(at_ref[0], wo_ref[fw:fw + aw, :])

    o = of_ref[0] + ob_ref[0]
    dv = hw // HG_HEADS
    lane_head = lax.broadcasted_iota(jnp.int32, (1, hw), 1) // dv
    o2 = o * o
    ms = jnp.zeros_like(o)
    for hh in range(HG_HEADS):
        sel = lane_head == hh
        ssh = jnp.sum(jnp.where(sel, o2, 0.0), axis=-1, keepdims=True)
        ms = jnp.where(sel, ssh * (1.0 / dv), ms)
    gg = g_ref[0]
    rx = (o * lax.rsqrt(ms + EPS) * gn_ref[...]) * (gg * jax.nn.sigmoid(gg))
    mix = mix + _dot(rx.astype(BF16), wo_ref[fw + aw:fw + aw + hw, :])
    o_ref[0] = x_ref[0] + gate * mix


def _outproj(xs, mod4, l, nb, y, at, of, ob, g, wf, wo, gn, lc, tm):
    b, t, d = xs.shape
    n6 = mod4.shape[-1]
    fw, hw = y.shape[-1], of.shape[-1]
    aw = at.shape[1]
    tok = lambda w: pl.BlockSpec((1, tm, w), lambda bi, i: (bi, i, 0))
    return pl.pallas_call(
        functools.partial(_outproj_kernel, lc=lc, tm=tm, d=d, fw=fw, aw=aw, hw=hw),
        grid=(b, t // tm),
        in_specs=[tok(d),
                  pl.BlockSpec((1, 1, 1, n6), lambda bi, i: (l, bi, 0, 0)),
                  pl.BlockSpec((1, 1, 1, n6), lambda bi, i: (l, nb, 0, 0)),
                  tok(fw),
                  pl.BlockSpec((1, aw, tm), lambda bi, i: (bi, 0, i)),
                  tok(hw), tok(hw), tok(hw),
                  _resident(wf.shape), _resident(wo.shape), _resident(gn.shape)],
        out_specs=tok(d),
        out_shape=jax.ShapeDtypeStruct((b, t, d), F32),
        name="outproj",
        compiler_params=pltpu.CompilerParams(
            dimension_semantics=("parallel", "parallel"),
            vmem_limit_bytes=V7X_VMEM_LIMIT),
    )(xs, mod4, mod4, y, at, of, ob, g, wf, wo, gn)


def _ffn_kernel(x_ref, mb_ref, mc_ref, wg_ref, wu_ref, wd_ref, o_ref, *, lc, tm, d, fchunk):
    i = pl.program_id(1)
    x = x_ref[0]
    row = i * tm + lax.broadcasted_iota(jnp.int32, (tm, 1), 0)
    is_ctx = row < lc
    mb, mc = mb_ref[0, 0], mc_ref[0, 0]
    h = _modulated(x, mb, mc, is_ctx, d, 3, 4).astype(BF16)
    gate = jnp.where(is_ctx, mc[:, 5 * d:6 * d], mb[:, 5 * d:6 * d])
    dff = wg_ref.shape[1]
    acc = jnp.zeros((tm, d), F32)
    for c0 in range(0, dff, fchunk):
        a = _dot(h, wg_ref[:, c0:c0 + fchunk])
        u = _dot(h, wu_ref[:, c0:c0 + fchunk])
        act = (a * jax.nn.sigmoid(a) * u).astype(BF16)
        acc = acc + _dot(act, wd_ref[c0:c0 + fchunk, :])
    o_ref[0] = x + gate * acc


def _ffn(xs, mod4, l, nb, wg, wu, wd, lc, tm):
    b, t, d = xs.shape
    n6 = mod4.shape[-1]
    dff = wg.shape[1]
    fchunk = _pick_tile(dff, 768)
    tok = pl.BlockSpec((1, tm, d), lambda bi, i: (bi, i, 0))
    return pl.pallas_call(
        functools.partial(_ffn_kernel, lc=lc, tm=tm, d=d, fchunk=fchunk),
        grid=(b, t // tm),
        in_specs=[tok,
                  pl.BlockSpec((1, 1, 1, n6), lambda bi, i: (l, bi, 0, 0)),
                  pl.BlockSpec((1, 1, 1, n6), lambda bi, i: (l, nb, 0, 0)),
                  _resident(wg.shape), _resident(wu.shape), _resident(wd.shape)],
        out_specs=tok,
        out_shape=jax.ShapeDtypeStruct((b, t, d), F32),
        name="ffn",
        compiler_params=pltpu.CompilerParams(
            dimension_semantics=("parallel", "parallel"),
            vmem_limit_bytes=V7X_VMEM_LIMIT),
    )(xs, mod4, mod4, wg, wu, wd)


def _final_kernel(x_ref, g_ref, o_ref):
    x = x_ref[0]
    o_ref[0] = x * lax.rsqrt(jnp.mean(x * x, axis=-1, keepdims=True) + EPS) * g_ref[...]


def _final_norm(xs, gain, lc, tm):
    b, t, d = xs.shape
    s = t - lc
    off = lc // tm
    return pl.pallas_call(
        _final_kernel,
        grid=(b, s // tm),
        in_specs=[pl.BlockSpec((1, tm, d), lambda bi, i: (bi, i + off, 0)),
                  pl.BlockSpec((1, d), lambda bi, i: (0, 0))],
        out_specs=pl.BlockSpec((1, tm, d), lambda bi, i: (bi, i, 0)),
        out_shape=jax.ShapeDtypeStruct((b, s, d), F32),
        name="final_norm",
        compiler_params=pltpu.CompilerParams(dimension_semantics=("parallel", "parallel")),
    )(xs, gain.reshape(1, d))


def _rope_tables(lc, s):
    half = HEAD_DIM // 2
    n = jnp.arange(s)
    freqs = ROPE_THETA ** (-jnp.arange(0, half, 2, dtype=F32) / half)
    ang_r = (n // GRID_W).astype(F32)[:, None] * freqs
    ang_c = (n % GRID_W).astype(F32)[:, None] * freqs
    pad = lambda a, v: jnp.concatenate([jnp.full((lc, a.shape[1]), v, F32), a], axis=0)
    cr, sr = pad(jnp.cos(ang_r), 1.0), pad(jnp.sin(ang_r), 0.0)
    cc, sc = pad(jnp.cos(ang_c), 1.0), pad(jnp.sin(ang_c), 0.0)
    cos_h = jnp.concatenate([cr, cr, cc, cc], axis=1)
    sin_h = jnp.concatenate([-sr, sr, -sc, sc], axis=1)
    ckn = jnp.tile(cos_h, (1, ATT_KV_HEADS))
    skn = jnp.tile(sin_h, (1, ATT_KV_HEADS))
    cqt = jnp.concatenate([cr, cc], axis=1).T
    sqt = jnp.concatenate([sr, sc], axis=1).T
    return ckn, skn, cqt, sqt


def kernel(x, c, ctx, c_ctx, w_ada, b_ada, w_in, w_four, q_norm, k_norm, hg_lb_logits, hg_norm,
           w_out, w_gate, w_up, w_down, final_norm):
    b, s, d = x.shape
    lc = ctx.shape[1]
    t = lc + s
    depth = w_ada.shape[0]
    fw = w_four.shape[-1]
    hw = hg_lb_logits.shape[-1]
    qw = ATT_HEADS * HEAD_DIM
    kw = ATT_KV_HEADS * HEAD_DIM
    assert w_in.shape[-1] == fw + qw + 2 * kw + 5 * hw
    assert s % (DFT_N2 * 8) == 0 and lc % 128 == 0 and s % GRID_W == 0

    tq = 256 if (lc % 256 == 0 and t % 256 == 0) else 128
    tkv = tq
    tm = _pick_tile(t, 768, mult=tkv)
    nchunk = max(g for g in (4, 2, 1) if lc % (g * HG_CHUNK) == 0 and t % (g * HG_CHUNK) == 0)
    tfin = 256 if (lc % 256 == 0 and s % 256 == 0) else 128

    xs = jnp.concatenate([ctx, x], axis=1)
    r8 = -(-(b + 1) // 8) * 8
    cvec = jnp.concatenate([c, c_ctx[None], jnp.zeros((r8 - b - 1, d), F32)], axis=0)
    mod = _adaln(cvec, w_ada, b_ada)
    mod4 = mod.reshape(depth, r8, 1, mod.shape[-1])

    lb_sm = jax.nn.softmax(hg_lb_logits.astype(F32), axis=1)
    lb_all = jnp.cumsum(lb_sm, axis=1) - lb_sm[:, :1]

    ckn, skn, cqt, sqt = _rope_tables(lc, s)
    fc = _fourier_consts(lc, s)
    hc = _hgrn_consts()

    o_q, o_k, o_v, o_h = fw, fw + qw, fw + qw + kw, fw + qw + 2 * kw
    for l in range(depth):
        wl = w_in[l]
        wn = jnp.concatenate([wl[:, 0:fw], wl[:, o_k:o_v], wl[:, o_h:]], axis=1).astype(BF16)
        wt = jnp.concatenate([wl[:, o_q:o_k], wl[:, o_v:o_h]], axis=1).T.astype(BF16)
        gk = jnp.tile(k_norm[l], ATT_KV_HEADS)[None]
        gq = q_norm[l][:, None]
        gn = jnp.tile(hg_norm[l], HG_HEADS)[None]
        (fz, kn, kf, lff, kb, lfb, v, hq, g, qt, vt) = _inproj(
            xs, mod4, l, b, wn, wt, gk, ckn, skn, gq, cqt, sqt,
            lb_all[0, l][None], lb_all[1, l][None], lc, tm, tkv)
        y = _fourier(fz, fc, lc)
        bound = (HEAD_DIM ** 0.5 * LOG2E * BOUND_SLACK) * jnp.max(jnp.abs(q_norm[l])) * jnp.max(jnp.abs(k_norm[l]))
        at = lax.cond(bound < MAX_SAFE_LOG2_SCORE,
                      functools.partial(_attention, lc=lc, tq=tq, bounded=True),
                      functools.partial(_attention, lc=lc, tq=tq, bounded=False),
                      qt, kn, vt)
        of, ob = _hgrn(kf, lff, kb, lfb, v, hq, hc, lc, nchunk)
        xs = _outproj(xs, mod4, l, b, y, at, of, ob, g, w_four[l].astype(BF16),
                      w_out[l].astype(BF16), gn, lc, tm)
        xs = _ffn(xs, mod4, l, b, w_gate[l].astype(BF16), w_up[l].astype(BF16),
                  w_down[l].astype(BF16), lc, tm)
    return _final_norm(xs, final_norm, lc, tfin)
```

```python
import functools

import numpy as np
import jax
import jax.numpy as jnp
from jax import lax
from jax.experimental import pallas as pl
from jax.experimental.pallas import tpu as pltpu

F32 = jnp.float32
BF16 = jnp.bfloat16

EPS = 1e-6
GRID_W = 64
ROPE_THETA = 10000.0
FOURIER_GD = 64
ATT_HEADS = 8
ATT_KV_HEADS = 2
ATT_GROUP = ATT_HEADS // ATT_KV_HEADS
HEAD_DIM = 64
HG_HEADS = 4
HG_CHUNK = 64
HG_LEVELS = (32, 16, 8, 4, 2, 1)
DFT_N2 = 128
DFT_ROW_PAD = 8
NEG_BIG = -1e30
LOG2E = 1.4426950408889634
BOUND_SLACK = 1.05
MAX_SAFE_LOG2_SCORE = 100.0
ATTN_BOUNDED_KEYS = 2816
V7X_VMEM_LIMIT = 56 * 1024 * 1024


def _dot(a, b):
    return jnp.dot(a, b, preferred_element_type=F32)


def _dot_nt(a, b):
    return lax.dot_general(a, b, (((1,), (1,)), ((), ())), preferred_element_type=F32)


def _dot_tn(a, b):
    return lax.dot_general(a, b, (((0,), (0,)), ((), ())), preferred_element_type=F32)


def _split3(x):
    hi = x.astype(BF16)
    r1 = x - hi.astype(F32)
    mid = r1.astype(BF16)
    lo = (r1 - mid.astype(F32)).astype(BF16)
    return hi, mid, lo


def _pick_tile(n, cap, mult=128):
    best = None
    for t in range(mult, min(n, cap) + 1, mult):
        if n % t == 0:
            best = t
    if best is None:
        raise ValueError(f"no tile for {n}")
    return best


def _resident(shape):
    nd = len(shape)
    return pl.BlockSpec(shape, lambda *_: (0,) * nd, pipeline_mode=pl.Buffered(1))


def _adaln_kernel(c_ref, w_ref, b_ref, o_ref):
    cv = c_ref[...]
    a = cv * jax.nn.sigmoid(cv)
    w = w_ref[0]
    a_hi = a.astype(BF16)
    a_lo = (a - a_hi.astype(F32)).astype(BF16)
    w_hi = w.astype(BF16)
    w_lo = (w - w_hi.astype(F32)).astype(BF16)
    acc = _dot(a_hi, w_hi) + _dot(a_lo, w_hi) + _dot(a_hi, w_lo)
    o_ref[0] = acc + b_ref[0]


def _adaln(cvec, w_ada, b_ada):
    depth, d, n6 = w_ada.shape
    r8 = cvec.shape[0]
    tn = _pick_tile(n6, 1536)
    return pl.pallas_call(
        _adaln_kernel,
        grid=(depth, n6 // tn),
        in_specs=[
            pl.BlockSpec((r8, d), lambda l, j: (0, 0)),
            pl.BlockSpec((1, d, tn), lambda l, j: (l, 0, j)),
            pl.BlockSpec((1, 1, tn), lambda l, j: (l, 0, j)),
        ],
        out_specs=pl.BlockSpec((1, r8, tn), lambda l, j: (l, 0, j)),
        out_shape=jax.ShapeDtypeStruct((depth, r8, n6), F32),
        name="adaln",
        compiler_params=pltpu.CompilerParams(
            dimension_semantics=("parallel", "parallel"),
            vmem_limit_bytes=V7X_VMEM_LIMIT),
    )(cvec, w_ada, b_ada.reshape(depth, 1, n6))


def _modulated(x, mb, mc, is_ctx, d, k_shift, k_scale):
    r = lax.rsqrt(jnp.mean(x * x, axis=-1, keepdims=True) + EPS)
    sh = jnp.where(is_ctx, mc[:, k_shift * d:(k_shift + 1) * d], mb[:, k_shift * d:(k_shift + 1) * d])
    sc = jnp.where(is_ctx, mc[:, k_scale * d:(k_scale + 1) * d], mb[:, k_scale * d:(k_scale + 1) * d])
    return (x * r) * (1.0 + sc) + sh


def _inproj_kernel(x_ref, mb_ref, mc_ref, wn_ref, wt_ref, gk_ref, ckn_ref, skn_ref,
                   gq_ref, cqt_ref, sqt_ref, lbf_ref, lbb_ref,
                   fz_ref, k_ref, kf_ref, lff_ref, kb_ref, lfb_ref, v_ref, hq_ref, g_ref,
                   qt_ref, vt_ref, *, lc, tm, tkv, d, fw, hw):
    i = pl.program_id(1)
    x = x_ref[0]
    row = i * tm + lax.broadcasted_iota(jnp.int32, (tm, 1), 0)
    is_ctx = row < lc
    h = _modulated(x, mb_ref[0, 0], mc_ref[0, 0], is_ctx, d, 0, 1).astype(BF16)

    un = _dot(h, wn_ref[...])
    fz_ref[0] = un[:, 0:fw]

    o = fw
    kw = ATT_KV_HEADS * HEAD_DIM
    uk = un[:, o:o + kw]
    lane = lax.broadcasted_iota(jnp.int32, (1, kw), 1)
    uk2 = uk * uk
    ms = jnp.zeros_like(uk)
    for hh in range(ATT_KV_HEADS):
        sel = (lane // HEAD_DIM) == hh
        ssh = jnp.sum(jnp.where(sel, uk2, 0.0), axis=-1, keepdims=True)
        ms = jnp.where(sel, ssh * (1.0 / HEAD_DIM), ms)
    kn = uk * lax.rsqrt(ms + EPS) * gk_ref[...]
    first = (lane % 32) < 16
    partner = jnp.where(first, pltpu.roll(kn, kw - 16, axis=1), pltpu.roll(kn, 16, axis=1))
    k_ref[0] = (kn * ckn_ref[...] + partner * skn_ref[...]).astype(BF16)
    o += kw

    for lb_ref, kk_ref, lf_ref in ((lbf_ref, kf_ref, lff_ref), (lbb_ref, kb_ref, lfb_ref)):
        z = un[:, o:o + hw]
        lb = lb_ref[...]
        kk_ref[0] = (1.0 - lb) * jax.nn.sigmoid(-z)
        lf_ref[0] = jnp.log(lb + (1.0 - lb) * jax.nn.sigmoid(z))
        o += hw
    v_ref[0] = un[:, o:o + hw].astype(BF16)
    hq_ref[0] = un[:, o + hw:o + 2 * hw]
    g_ref[0] = un[:, o + 2 * hw:o + 3 * hw]

    ut = _dot_nt(wt_ref[...], h)
    qw = ATT_HEADS * HEAD_DIM
    uq = ut[0:qw].reshape(ATT_HEADS, HEAD_DIM, tm)
    msq = jnp.mean(uq * uq, axis=1, keepdims=True)
    qn = uq * lax.rsqrt(msq + EPS) * gq_ref[...][None]
    cr, sr = cqt_ref[0:16][None], sqt_ref[0:16][None]
    cc, sc = cqt_ref[16:32][None], sqt_ref[16:32][None]
    x1r, x2r, x1c, x2c = qn[:, 0:16], qn[:, 16:32], qn[:, 32:48], qn[:, 48:64]
    qr = jnp.concatenate([x1r * cr - x2r * sr, x1r * sr + x2r * cr,
                          x1c * cc - x2c * sc, x1c * sc + x2c * cc], axis=1)
    qt_ref[0] = (qr * (HEAD_DIM ** -0.5 * LOG2E)).astype(BF16)

    uv = ut[qw:qw + kw].reshape(ATT_KV_HEADS, HEAD_DIM, tm)
    ones_row = lax.broadcasted_iota(jnp.int32, (ATT_KV_HEADS, 16, tm), 1) == 0
    aug = jnp.where(ones_row, 1.0, 0.0).astype(F32)
    vaug = jnp.concatenate([uv, aug], axis=1).astype(BF16)
    for cb in range(tm // tkv):
        vt_ref[0, :, cb] = vaug[:, :, cb * tkv:(cb + 1) * tkv]


def _inproj(xs, mod4, l, nb, wn, wt, gk, ckn, skn, gq, cqt, sqt, lbf, lbb, lc, tm, tkv):
    b, t, d = xs.shape
    n6 = mod4.shape[-1]
    fw = FOURIER_GD * 4
    hw = lbf.shape[-1]
    kw = ATT_KV_HEADS * HEAD_DIM
    tok = lambda w, dt: jax.ShapeDtypeStruct((b, t, w), dt)
    tok_spec = lambda w: pl.BlockSpec((1, tm, w), lambda bi, i: (bi, i, 0))
    out_shape = [tok(fw, F32), tok(kw, BF16), tok(hw, F32), tok(hw, F32), tok(hw, F32), tok(hw, F32),
                 tok(hw, BF16), tok(hw, F32), tok(hw, F32),
                 jax.ShapeDtypeStruct((b, ATT_HEADS, HEAD_DIM, t), BF16),
                 jax.ShapeDtypeStruct((b, ATT_KV_HEADS, t // tkv, HEAD_DIM + 16, tkv), BF16)]
    out_specs = [tok_spec(fw), tok_spec(kw)] + [tok_spec(hw)] * 7 + [
        pl.BlockSpec((1, ATT_HEADS, HEAD_DIM, tm), lambda bi, i: (bi, 0, 0, i)),
        pl.BlockSpec((1, ATT_KV_HEADS, tm // tkv, HEAD_DIM + 16, tkv), lambda bi, i: (bi, 0, i, 0, 0))]
    in_specs = [
        pl.BlockSpec((1, tm, d), lambda bi, i: (bi, i, 0)),
        pl.BlockSpec((1, 1, 1, n6), lambda bi, i: (l, bi, 0, 0)),
        pl.BlockSpec((1, 1, 1, n6), lambda bi, i: (l, nb, 0, 0)),
        _resident(wn.shape), _resident(wt.shape), _resident(gk.shape),
        pl.BlockSpec((tm, kw), lambda bi, i: (i, 0)),
        pl.BlockSpec((tm, kw), lambda bi, i: (i, 0)),
        _resident(gq.shape),
        pl.BlockSpec((32, tm), lambda bi, i: (0, i)),
        pl.BlockSpec((32, tm), lambda bi, i: (0, i)),
        _resident(lbf.shape), _resident(lbb.shape),
    ]
    return pl.pallas_call(
        functools.partial(_inproj_kernel, lc=lc, tm=tm, tkv=tkv, d=d, fw=fw, hw=hw),
        grid=(b, t // tm),
        in_specs=in_specs, out_specs=out_specs, out_shape=out_shape,
        name="inproj",
        compiler_params=pltpu.CompilerParams(
            dimension_semantics=("parallel", "parallel"),
            vmem_limit_bytes=V7X_VMEM_LIMIT),
    )(xs, mod4, mod4, wn, wt, gk, ckn, skn, gq, cqt, sqt, lbf, lbb)


def _fourier_kernel(z_ref, f1_ref, twc_ref, tws_ref, c2_ref, s2_ref, cs_ref, cl_ref,
                    y_ref, a1_ref, p_ref, *, lc, n1, scale_c, scale_x):
    n2 = DFT_N2
    cs = cs_ref[...]
    pa, pp = 2 * n1 + DFT_ROW_PAD, n2 + DFT_ROW_PAD

    zc = z_ref[0, 0:lc, :].astype(BF16)
    pc = _dot(cl_ref[...], zc)
    pcat = jnp.concatenate([pc[0:lc], pc[lc:2 * lc]], axis=1).astype(BF16)
    y_ref[0, 0:lc, :] = (_dot(pcat, cs) * scale_c).astype(y_ref.dtype)

    f1 = f1_ref[...]

    def stage1(j, carry):
        xj = z_ref[0, pl.ds(lc + j, n1, stride=n2), :].astype(BF16)
        a1_ref[pl.ds(pl.multiple_of(j * pa, 8), 2 * n1), :] = _dot(f1, xj)
        return carry

    lax.fori_loop(0, n2, stage1, 0, unroll=8)

    c2 = c2_ref[...]
    s2 = s2_ref[...]

    def stage2(k1, carry):
        ar = a1_ref[pl.ds(k1, n2, stride=pa), :]
        ai = a1_ref[pl.ds(n1 + k1, n2, stride=pa), :]
        twc = twc_ref[pl.ds(k1, 1), :]
        tws = tws_ref[pl.ds(k1, 1), :]
        gr = c2 * twc - s2 * tws
        gi = -(s2 * twc + c2 * tws)
        gm = jnp.concatenate([jnp.concatenate([gr, -gi], axis=1),
                              jnp.concatenate([gi, gr], axis=1)], axis=0).astype(BF16)
        rhs = jnp.concatenate([ar, ai], axis=0).astype(BF16)
        pri = _dot(gm, rhs)
        pcat2 = jnp.concatenate([pri[0:n2], pri[n2:2 * n2]], axis=1).astype(BF16)
        p_ref[pl.ds(pl.multiple_of(k1 * pp, 8), n2), :] = _dot(pcat2, cs) * scale_x
        return carry

    lax.fori_loop(0, n1, stage2, 0, unroll=8)

    def stage3(k2, carry):
        blk = p_ref[pl.ds(k2, n1, stride=pp), :]
        y_ref[0, pl.ds(pl.multiple_of(lc + k2 * n1, n1), n1), :] = blk.astype(y_ref.dtype)
        return carry

    lax.fori_loop(0, n2, stage3, 0, unroll=8)


def _fourier_consts(lc, s):
    n2 = DFT_N2
    n1 = s // n2
    f64 = np.float64
    k = np.arange(n1, dtype=f64)
    a1 = 2 * np.pi * np.outer(k, k) / n1
    f1 = np.concatenate([np.cos(a1), -np.sin(a1)], axis=0)
    at = 2 * np.pi * np.outer(np.arange(n1, dtype=f64), np.arange(n2, dtype=f64)) / s
    k2 = np.arange(n2, dtype=f64)
    a2 = 2 * np.pi * np.outer(k2, k2) / n2
    w = 2 * FOURIER_GD
    ch = np.arange(w)
    same = (ch[:, None] // FOURIER_GD) == (ch[None, :] // FOURIER_GD)
    ag = 2 * np.pi * np.outer(ch % FOURIER_GD, ch % FOURIER_GD) / FOURIER_GD
    cs = np.concatenate([np.where(same, np.cos(ag), 0.0), np.where(same, np.sin(ag), 0.0)], axis=0)
    kc = np.arange(lc, dtype=f64)
    al = 2 * np.pi * np.outer(kc, kc) / lc
    cl = np.concatenate([np.cos(al), -np.sin(al)], axis=0)
    return dict(
        f1=jnp.asarray(f1, BF16), twc=jnp.asarray(np.cos(at), F32), tws=jnp.asarray(np.sin(at), F32),
        c2=jnp.asarray(np.cos(a2), F32), s2=jnp.asarray(np.sin(a2), F32),
        cs=jnp.asarray(cs, BF16), cl=jnp.asarray(cl, BF16), n1=n1)


def _fourier(fz, fc, lc):
    b, t, fw = fz.shape
    s = t - lc
    n1 = fc["n1"]
    w = 2 * FOURIER_GD
    consts = [fc[k] for k in ("f1", "twc", "tws", "c2", "s2", "cs", "cl")]
    return pl.pallas_call(
        functools.partial(_fourier_kernel, lc=lc, n1=n1,
                          scale_c=float((lc * FOURIER_GD) ** -0.5),
                          scale_x=float((s * FOURIER_GD) ** -0.5)),
        grid=(b, fw // w),
        in_specs=[pl.BlockSpec((1, t, w), lambda bi, hi: (bi, 0, hi))] +
                 [_resident(c.shape) for c in consts],
        out_specs=pl.BlockSpec((1, t, w), lambda bi, hi: (bi, 0, hi)),
        out_shape=jax.ShapeDtypeStruct((b, t, fw), BF16),
        scratch_shapes=[pltpu.VMEM((DFT_N2 * (2 * n1 + DFT_ROW_PAD), w), F32),
                        pltpu.VMEM((n1 * (DFT_N2 + DFT_ROW_PAD), w), F32)],
        name="fourier",
        compiler_params=pltpu.CompilerParams(
            dimension_semantics=("parallel", "parallel"),
            vmem_limit_bytes=V7X_VMEM_LIMIT),
    )(fz, *consts)


def _place_queries(q_ref, qs_ref):
    g = pl.program_id(1)
    qcat = jnp.concatenate([q_ref[0, j] for j in range(ATT_GROUP)], axis=1)
    zero = jnp.zeros_like(qcat)
    for gg in range(ATT_KV_HEADS):
        @pl.when(g == gg)
        def _place():
            for hh in range(ATT_KV_HEADS):
                qs_ref[hh * HEAD_DIM:(hh + 1) * HEAD_DIM, :] = qcat if hh == gg else zero


def _score_block(k_ref, qs_ref, blk, tk, lc, masked):
    kb = k_ref[0, pl.ds(pl.multiple_of(blk * tk, tk), tk), :]
    s = _dot(kb, qs_ref[...])
    if masked:
        kidx = blk * tk + lax.broadcasted_iota(jnp.int32, (tk, 1), 0)
        s = jnp.where(kidx < lc, s, NEG_BIG)
    return s


def _attn_finish(acc_ref, o_ref, tq):
    acc = acc_ref[...]
    o = acc[0:HEAD_DIM] / acc[HEAD_DIM:HEAD_DIM + 1]
    o_ref[0] = jnp.concatenate([o[:, j * tq:(j + 1) * tq] for j in range(ATT_GROUP)],
                               axis=0).astype(o_ref.dtype)


def _attn_bounded_kernel(q_ref, k_ref, v_ref, o_ref, qs_ref, p0_ref, p1_ref, acc_ref,
                         *, lc, tq, tk, nqc, nkc, nk):
    qi = pl.program_id(2)
    _place_queries(q_ref, qs_ref)
    acc_ref[...] = jnp.zeros(acc_ref.shape, F32)

    p_ref = (p0_ref, p1_ref)
    tkv = v_ref.shape[-1]
    nsub = tk // tkv

    def probs(blk, slot, masked=False):
        p_ref[slot][...] = jnp.exp2(_score_block(k_ref, qs_ref, blk, tk, lc, masked)).astype(BF16)

    def weighted(blk, slot):
        pv = _dot(v_ref[0, 0, blk * nsub], p_ref[slot][0:tkv, :])
        for c in range(1, nsub):
            pv = pv + _dot(v_ref[0, 0, blk * nsub + c], p_ref[slot][c * tkv:(c + 1) * tkv, :])
        acc_ref[...] += pv

    @pl.when(qi < nqc)
    def _ctx():
        for blk in range(nkc):
            probs(blk, 0, masked=True)
            weighted(blk, 0)

    @pl.when(qi >= nqc)
    def _lat():
        def steady(t, slot):
            probs(t + 1, 1 - slot)
            weighted(t, slot)

        probs(0, 0)
        npairs = (nk - 1) // 2

        def pair(i, carry):
            steady(2 * i, 0)
            steady(2 * i + 1, 1)
            return carry

        lax.fori_loop(0, npairs, pair, 0)
        if (nk - 1) % 2:
            steady(nk - 2, (nk - 2) % 2)
        weighted(nk - 1, (nk - 1) % 2)

    _attn_finish(acc_ref, o_ref, tq)


def _attn_kernel(q_ref, k_ref, v_ref, o_ref, qs_ref, s0_ref, s1_ref, mb0_ref, mb1_ref, p0_ref, p1_ref,
                 al0_ref, al1_ref, m_ref, acc_ref, *, lc, tq, tk, nqc, nkc, nk):
    s_ref, mb_ref = (s0_ref, s1_ref), (mb0_ref, mb1_ref)
    p_ref, al_ref = (p0_ref, p1_ref), (al0_ref, al1_ref)
    qi = pl.program_id(2)

    _place_queries(q_ref, qs_ref)
    m_ref[...] = jnp.full(m_ref.shape, NEG_BIG, F32)
    acc_ref[...] = jnp.zeros(acc_ref.shape, F32)

    def scores(blk, slot, masked=False):
        s = _score_block(k_ref, qs_ref, blk, tk, lc, masked)
        s_ref[slot][...] = s
        mb_ref[slot][...] = jnp.max(s, axis=0, keepdims=True)

    def probs(slot):
        m_prev = m_ref[...]
        m_new = jnp.maximum(m_prev, mb_ref[slot][...])
        al_ref[slot][...] = jnp.exp2(m_prev - m_new)
        p_ref[slot][...] = jnp.exp2(s_ref[slot][...] - m_new).astype(BF16)
        m_ref[...] = m_new

    def weighted(blk, slot):
        acc_ref[...] = al_ref[slot][...] * acc_ref[...] + _dot(v_ref[0, 0, blk], p_ref[slot][...])

    @pl.when(qi < nqc)
    def _ctx():
        for blk in range(nkc):
            scores(blk, 0, masked=True)
            probs(0)
            weighted(blk, 0)

    @pl.when(qi >= nqc)
    def _lat():
        def steady(t, slot):
            scores(t + 2, slot)
            weighted(t, slot)
            probs(1 - slot)

        scores(0, 0)
        scores(1, 1)
        probs(0)
        npairs = (nk - 2) // 2

        def pair(i, carry):
            steady(2 * i, 0)
            steady(2 * i + 1, 1)
            return carry

        lax.fori_loop(0, npairs, pair, 0)
        if (nk - 2) % 2:
            steady(2 * npairs, 0)
        weighted(nk - 2, (nk - 2) % 2)
        probs((nk - 1) % 2)
        weighted(nk - 1, (nk - 1) % 2)

    _attn_finish(acc_ref, o_ref, tq)


def _attention(qt, kn, vt, lc, tq, bounded):
    b, _, _, t = qt.shape
    nkv, tkv = vt.shape[2], vt.shape[4]
    tk = _pick_tile(t, ATTN_BOUNDED_KEYS, mult=tkv) if bounded else tkv
    nk = t // tk
    nq = t // tq
    nqc = lc // tq
    nkc = -(-lc // tk)
    wide = ATT_GROUP * tq
    qs_scratch = pltpu.VMEM((ATT_KV_HEADS * HEAD_DIM, wide), BF16)
    acc_scratch = pltpu.VMEM((HEAD_DIM + 16, wide), F32)
    if bounded:
        body = _attn_bounded_kernel
        scratch = [qs_scratch, pltpu.VMEM((tk, wide), BF16), pltpu.VMEM((tk, wide), BF16), acc_scratch]
    else:
        assert nk >= 2
        body = _attn_kernel
        scratch = [qs_scratch,
                   pltpu.VMEM((tk, wide), F32), pltpu.VMEM((tk, wide), F32),
                   pltpu.VMEM((1, wide), F32), pltpu.VMEM((1, wide), F32),
                   pltpu.VMEM((tk, wide), BF16), pltpu.VMEM((tk, wide), BF16),
                   pltpu.VMEM((1, wide), F32), pltpu.VMEM((1, wide), F32),
                   pltpu.VMEM((1, wide), F32),
                   acc_scratch]
    return pl.pallas_call(
        functools.partial(body, lc=lc, tq=tq, tk=tk, nqc=nqc, nkc=nkc, nk=nk),
        grid=(b, ATT_KV_HEADS, nq),
        in_specs=[
            pl.BlockSpec((1, ATT_GROUP, HEAD_DIM, tq), lambda bi, g, qi: (bi, g, 0, qi)),
            pl.BlockSpec((1, t, ATT_KV_HEADS * HEAD_DIM), lambda bi, g, qi: (bi, 0, 0)),
            pl.BlockSpec((1, 1, nkv, HEAD_DIM + 16, tkv), lambda bi, g, qi: (bi, g, 0, 0, 0)),
        ],
        out_specs=pl.BlockSpec((1, ATT_GROUP * HEAD_DIM, tq), lambda bi, g, qi: (bi, g, qi)),
        out_shape=jax.ShapeDtypeStruct((b, ATT_HEADS * HEAD_DIM, t), BF16),
        scratch_shapes=scratch,
        name="attn_bounded" if bounded else "attn_online",
        compiler_params=pltpu.CompilerParams(
            dimension_semantics=("parallel", "parallel", "arbitrary"),
            vmem_limit_bytes=V7X_VMEM_LIMIT),
    )(qt, kn, vt)


def _hgrn_consts():
    c = HG_CHUNK
    t = np.arange(c)
    tt, rr = t[:, None], t[None, :]
    mats, masks = [], []
    for rev in (False, True):
        mats.append(np.concatenate([(rr >= tt) if rev else (rr <= tt), np.ones((8, c), bool)], axis=0))
        lvl_masks = []
        for bsz in HG_LEVELS:
            blk, pos = t // (2 * bsz), t % (2 * bsz)
            qv = (pos < bsz) if rev else (pos >= bsz)
            lvl_masks.append((blk[:, None] == blk[None, :]) & qv[:, None] & ~qv[None, :])
        lvl_masks.append(tt == rr)
        masks.append(np.stack([np.tile(m, (HG_HEADS, 1)) for m in lvl_masks]))
    return jnp.asarray(np.stack(mats), BF16), jnp.asarray(np.stack(masks), F32)


def _level_reference(cum, bsz, rev):
    c, hw = cum.shape
    i = bsz if rev else bsz - 1
    if 2 * bsz >= 8:
        c3 = cum.reshape(c // (2 * bsz), 2 * bsz, hw)
        return jnp.broadcast_to(c3[:, i:i + 1, :], c3.shape).reshape(c, hw)
    c3 = cum.reshape(c // 8, 8, hw)
    sub = lax.broadcasted_iota(jnp.int32, (1, 8, 1), 1) // (2 * bsz)
    ref = jnp.broadcast_to(c3[:, i:i + 1, :], c3.shape)
    for gidx in range(1, 8 // (2 * bsz)):
        r = gidx * 2 * bsz + i
        ref = jnp.where(sub == gidx, jnp.broadcast_to(c3[:, r:r + 1, :], c3.shape), ref)
    return ref.reshape(c, hw)


def _hgrn_chunk(d, r0, kk_ref, lf_ref, v_ref, q_ref, cm_ref, am_ref, o_ref, s_ref, hw):
    c = HG_CHUNK
    dk = hw // HG_HEADS
    rev = d == 1
    rows = pl.ds(r0, c)
    lf = lf_ref[0, rows, :]
    cm = cm_ref[d]
    hi, mid, lo = _split3(lf)
    e = _dot(cm, hi) + _dot(cm, mid) + _dot(cm, lo)
    cum, tot = e[0:c], e[c:c + 1]
    q = q_ref[0, rows, :]
    k = kk_ref[0, rows, :]
    v = v_ref[0, rows, :]
    qdec = q * jnp.exp(cum)
    kdec = k * jnp.exp(tot - cum)
    ds = jnp.exp(tot)

    lane_head = lax.broadcasted_iota(jnp.int32, (1, hw), 1) // dk
    heads = [lane_head == hh for hh in range(HG_HEADS)]
    tpos = lax.broadcasted_iota(jnp.int32, (c, 1), 0)

    att = jnp.zeros((HG_HEADS * c, c), F32)
    for lv, bsz in enumerate(HG_LEVELS + (0,)):
        if bsz:
            pos = tpos % (2 * bsz)
            is_q = (pos < bsz) if rev else (pos >= bsz)
            diff = cum - _level_reference(cum, bsz, rev)
            w = jnp.exp(jnp.where(is_q, diff, -diff))
            qd = jnp.where(is_q, q * w, 0.0)
            kd = jnp.where(is_q, 0.0, k * w)
        else:
            qd, kd = q, k
        q4 = jnp.concatenate([jnp.where(hm, qd, 0.0) for hm in heads], axis=0).astype(BF16)
        att = att + am_ref[d, lv] * _dot_nt(q4, kd.astype(BF16))
    ov = _dot(att.astype(BF16), v)
    o = jnp.zeros((c, hw), F32)
    for hh, hm in enumerate(heads):
        o = o + jnp.where(hm, ov[hh * c:(hh + 1) * c], 0.0)

    st = s_ref[...]
    o = o + _dot_nt(qdec.astype(BF16), st.astype(BF16))
    o_ref[0, rows, :] = o
    u = _dot_tn(v, kdec.astype(BF16))
    rh = lax.broadcasted_iota(jnp.int32, (hw, 1), 0) // dk
    s_ref[...] = st * ds + jnp.where(rh == lane_head, u, 0.0)


def _hgrn_kernel(kf_ref, lff_ref, vf_ref, qf_ref, kb_ref, lfb_ref, vb_ref, qb_ref,
                 cm_ref, am_ref, of_ref, ob_ref, sf_ref, sb_ref, *, nchunk, hw):
    @pl.when(pl.program_id(1) == 0)
    def _init():
        sf_ref[...] = jnp.zeros(sf_ref.shape, F32)
        sb_ref[...] = jnp.zeros(sb_ref.shape, F32)

    for jj in range(nchunk):
        _hgrn_chunk(0, jj * HG_CHUNK, kf_ref, lff_ref, vf_ref, qf_ref,
                    cm_ref, am_ref, of_ref, sf_ref, hw)
        _hgrn_chunk(1, (nchunk - 1 - jj) * HG_CHUNK, kb_ref, lfb_ref, vb_ref, qb_ref,
                    cm_ref, am_ref, ob_ref, sb_ref, hw)


def _hgrn(kf, lff, kb, lfb, v, hq, hc, lc, nchunk):
    b, t, hw = kf.shape
    tb = nchunk * HG_CHUNK
    nblk, ncb = t // tb, lc // tb
    cm, am = hc
    fwd = lambda bi, p: (bi, p, 0)
    bwd = lambda bi, p: (bi, jnp.where(p < ncb, ncb - 1 - p, nblk - 1 - (p - ncb)), 0)
    blk = lambda im: pl.BlockSpec((1, tb, hw), im)
    return pl.pallas_call(
        functools.partial(_hgrn_kernel, nchunk=nchunk, hw=hw),
        grid=(b, nblk),
        in_specs=[blk(fwd)] * 4 + [blk(bwd)] * 4 + [_resident(cm.shape), _resident(am.shape)],
        out_specs=[blk(fwd), blk(bwd)],
        out_shape=[jax.ShapeDtypeStruct((b, t, hw), F32)] * 2,
        scratch_shapes=[pltpu.VMEM((hw, hw), F32), pltpu.VMEM((hw, hw), F32)],
        name="hgrn",
        compiler_params=pltpu.CompilerParams(
            dimension_semantics=("parallel", "arbitrary"),
            vmem_limit_bytes=V7X_VMEM_LIMIT),
    )(kf, lff, v, hq, kb, lfb, v, hq, cm, am)


def _outproj_kernel(x_ref, mb_ref, mc_ref, y_ref, at_ref, of_ref, ob_ref, g_ref, wf_ref, wo_ref,
                    gn_ref, o_ref, *, lc, tm, d, fw, aw, hw):
    i = pl.program_id(1)
    row = i * tm + lax.broadcasted_iota(jnp.int32, (tm, 1), 0)
    is_ctx = row < lc
    mb, mc = mb_ref[0, 0], mc_ref[0, 0]
    gate = jnp.where(is_ctx, mc[:, 2 * d:3 * d], mb[:, 2 * d:3 * d])

    fx = _dot(y_ref[0], wf_ref[...]).astype(BF16)
    mix = _dot(fx, wo_ref[0:fw, :])
    mix = mix + _dot_tn(at_ref[0], wo_ref[fw:fw + aw, :])

    o = of_ref[0] + ob_ref[0]
    dv = hw // HG_HEADS
    lane_head = lax.broadcasted_iota(jnp.int32, (1, hw), 1) // dv
    o2 = o * o
    ms = jnp.zeros_like(o)
    for hh in range(HG_HEADS):
        sel = lane_head == hh
        ssh = jnp.sum(jnp.where(sel, o2, 0.0), axis=-1, keepdims=True)
        ms = jnp.where(sel, ssh * (1.0 / dv), ms)
    gg = g_ref[0]
    rx = (o * lax.rsqrt(ms + EPS) * gn_ref[...]) * (gg * jax.nn.sigmoid(gg))
    mix = mix + _dot(rx.astype(BF16), wo_ref[fw + aw:fw + aw + hw, :])
    o_ref[0] = x_ref[0] + gate * mix


def _outproj(xs, mod4, l, nb, y, at, of, ob, g, wf, wo, gn, lc, tm):
    b, t, d = xs.shape
    n6 = mod4.shape[-1]
    fw, hw = y.shape[-1], of.shape[-1]
    aw = at.shape[1]
    tok = lambda w: pl.BlockSpec((1, tm, w), lambda bi, i: (bi, i, 0))
    return pl.pallas_call(
        functools.partial(_outproj_kernel, lc=lc, tm=tm, d=d, fw=fw, aw=aw, hw=hw),
        grid=(b, t // tm),
        in_specs=[tok(d),
                  pl.BlockSpec((1, 1, 1, n6), lambda bi, i: (l, bi, 0, 0)),
                  pl.BlockSpec((1, 1, 1, n6), lambda bi, i: (l, nb, 0, 0)),
                  tok(fw),
                  pl.BlockSpec((1, aw, tm), lambda bi, i: (bi, 0, i)),
                  tok(hw), tok(hw), tok(hw),
                  _resident(wf.shape), _resident(wo.shape), _resident(gn.shape)],
        out_specs=tok(d),
        out_shape=jax.ShapeDtypeStruct((b, t, d), F32),
        name="outproj",
        compiler_params=pltpu.CompilerParams(
            dimension_semantics=("parallel", "parallel"),
            vmem_limit_bytes=V7X_VMEM_LIMIT),
    )(xs, mod4, mod4, y, at, of, ob, g, wf, wo, gn)


def _ffn_kernel(x_ref, mb_ref, mc_ref, wg_ref, wu_ref, wd_ref, o_ref, *, lc, tm, d, fchunk):
    i = pl.program_id(1)
    x = x_ref[0]
    row = i * tm + lax.broadcasted_iota(jnp.int32, (tm, 1), 0)
    is_ctx = row < lc
    mb, mc = mb_ref[0, 0], mc_ref[0, 0]
    h = _modulated(x, mb, mc, is_ctx, d, 3, 4).astype(BF16)
    gate = jnp.where(is_ctx, mc[:, 5 * d:6 * d], mb[:, 5 * d:6 * d])
    dff = wg_ref.shape[1]
    acc = jnp.zeros((tm, d), F32)
    for c0 in range(0, dff, fchunk):
        a = _dot(h, wg_ref[:, c0:c0 + fchunk])
        u = _dot(h, wu_ref[:, c0:c0 + fchunk])
        act = (a * jax.nn.sigmoid(a) * u).astype(BF16)
        acc = acc + _dot(act, wd_ref[c0:c0 + fchunk, :])
    o_ref[0] = x + gate * acc


def _ffn(xs, mod4, l, nb, wg, wu, wd, lc, tm):
    b, t, d = xs.shape
    n6 = mod4.shape[-1]
    dff = wg.shape[1]
    fchunk = _pick_tile(dff, 768)
    tok = pl.BlockSpec((1, tm, d), lambda bi, i: (bi, i, 0))
    return pl.pallas_call(
        functools.partial(_ffn_kernel, lc=lc, tm=tm, d=d, fchunk=fchunk),
        grid=(b, t // tm),
        in_specs=[tok,
                  pl.BlockSpec((1, 1, 1, n6), lambda bi, i: (l, bi, 0, 0)),
                  pl.BlockSpec((1, 1, 1, n6), lambda bi, i: (l, nb, 0, 0)),
                  _resident(wg.shape), _resident(wu.shape), _resident(wd.shape)],
        out_specs=tok,
        out_shape=jax.ShapeDtypeStruct((b, t, d), F32),
        name="ffn",
        compiler_params=pltpu.CompilerParams(
            dimension_semantics=("parallel", "parallel"),
            vmem_limit_bytes=V7X_VMEM_LIMIT),
    )(xs, mod4, mod4, wg, wu, wd)


def _final_kernel(x_ref, g_ref, o_ref):
    x = x_ref[0]
    o_ref[0] = x * lax.rsqrt(jnp.mean(x * x, axis=-1, keepdims=True) + EPS) * g_ref[...]


def _final_norm(xs, gain, lc, tm):
    b, t, d = xs.shape
    s = t - lc
    off = lc // tm
    return pl.pallas_call(
        _final_kernel,
        grid=(b, s // tm),
        in_specs=[pl.BlockSpec((1, tm, d), lambda bi, i: (bi, i + off, 0)),
                  pl.BlockSpec((1, d), lambda bi, i: (0, 0))],
        out_specs=pl.BlockSpec((1, tm, d), lambda bi, i: (bi, i, 0)),
        out_shape=jax.ShapeDtypeStruct((b, s, d), F32),
        name="final_norm",
        compiler_params=pltpu.CompilerParams(dimension_semantics=("parallel", "parallel")),
    )(xs, gain.reshape(1, d))


def _rope_tables(lc, s):
    half = HEAD_DIM // 2
    n = jnp.arange(s)
    freqs = ROPE_THETA ** (-jnp.arange(0, half, 2, dtype=F32) / half)
    ang_r = (n // GRID_W).astype(F32)[:, None] * freqs
    ang_c = (n % GRID_W).astype(F32)[:, None] * freqs
    pad = lambda a, v: jnp.concatenate([jnp.full((lc, a.shape[1]), v, F32), a], axis=0)
    cr, sr = pad(jnp.cos(ang_r), 1.0), pad(jnp.sin(ang_r), 0.0)
    cc, sc = pad(jnp.cos(ang_c), 1.0), pad(jnp.sin(ang_c), 0.0)
    cos_h = jnp.concatenate([cr, cr, cc, cc], axis=1)
    sin_h = jnp.concatenate([-sr, sr, -sc, sc], axis=1)
    ckn = jnp.tile(cos_h, (1, ATT_KV_HEADS))
    skn = jnp.tile(sin_h, (1, ATT_KV_HEADS))
    cqt = jnp.concatenate([cr, cc], axis=1).T
    sqt = jnp.concatenate([sr, sc], axis=1).T
    return ckn, skn, cqt, sqt


def kernel(x, c, ctx, c_ctx, w_ada, b_ada, w_in, w_four, q_norm, k_norm, hg_lb_logits, hg_norm,
           w_out, w_gate, w_up, w_down, final_norm):
    b, s, d = x.shape
    lc = ctx.shape[1]
    t = lc + s
    depth = w_ada.shape[0]
    fw = w_four.shape[-1]
    hw = hg_lb_logits.shape[-1]
    qw = ATT_HEADS * HEAD_DIM
    kw = ATT_KV_HEADS * HEAD_DIM
    assert w_in.shape[-1] == fw + qw + 2 * kw + 5 * hw
    assert s % (DFT_N2 * 8) == 0 and lc % 128 == 0 and s % GRID_W == 0

    tq = 256 if (lc % 256 == 0 and t % 256 == 0) else 128
    tkv = tq
    tm = _pick_tile(t, 768, mult=tkv)
    nchunk = max(g for g in (4, 2, 1) if lc % (g * HG_CHUNK) == 0 and t % (g * HG_CHUNK) == 0)
    tfin = 256 if (lc % 256 == 0 and s % 256 == 0) else 128

    xs = jnp.concatenate([ctx, x], axis=1)
    r8 = -(-(b + 1) // 8) * 8
    cvec = jnp.concatenate([c, c_ctx[None], jnp.zeros((r8 - b - 1, d), F32)], axis=0)
    mod = _adaln(cvec, w_ada, b_ada)
    mod4 = mod.reshape(depth, r8, 1, mod.shape[-1])

    lb_sm = jax.nn.softmax(hg_lb_logits.astype(F32), axis=1)
    lb_all = jnp.cumsum(lb_sm, axis=1) - lb_sm[:, :1]

    ckn, skn, cqt, sqt = _rope_tables(lc, s)
    fc = _fourier_consts(lc, s)
    hc = _hgrn_consts()

    o_q, o_k, o_v, o_h = fw, fw + qw, fw + qw + kw, fw + qw + 2 * kw
    for l in range(depth):
        wl = w_in[l]
        wn = jnp.concatenate([wl[:, 0:fw], wl[:, o_k:o_v], wl[:, o_h:]], axis=1).astype(BF16)
        wt = jnp.concatenate([wl[:, o_q:o_k], wl[:, o_v:o_h]], axis=1).T.astype(BF16)
        gk = jnp.tile(k_norm[l], ATT_KV_HEADS)[None]
        gq = q_norm[l][:, None]
        gn = jnp.tile(hg_norm[l], HG_HEADS)[None]
        (fz, kn, kf, lff, kb, lfb, v, hq, g, qt, vt) = _inproj(
            xs, mod4, l, b, wn, wt, gk, ckn, skn, gq, cqt, sqt,
            lb_all[0, l][None], lb_all[1, l][None], lc, tm, tkv)
        y = _fourier(fz, fc, lc)
        bound = (HEAD_DIM ** 0.5 * LOG2E * BOUND_SLACK) * jnp.max(jnp.abs(q_norm[l])) * jnp.max(jnp.abs(k_norm[l]))
        at = lax.cond(bound < MAX_SAFE_LOG2_SCORE,
                      functools.partial(_attention, lc=lc, tq=tq, bounded=True),
                      functools.partial(_attention, lc=lc, tq=tq, bounded=False),
                      qt, kn, vt)
        of, ob = _hgrn(kf, lff, kb, lfb, v, hq, hc, lc, nchunk)
        xs = _outproj(xs, mod4, l, b, y, at, of, ob, g, w_four[l].astype(BF16),
                      w_out[l].astype(BF16), gn, lc, tm)
        xs = _ffn(xs, mod4, l, b, w_gate[l].astype(BF16), w_up[l].astype(BF16),
                  w_down[l].astype(BF16), lc, tm)
    return _final_norm(xs, final_norm, lc, tfin)
```

```python
import functools

import numpy as np
import jax
import jax.numpy as jnp
from jax import lax
from jax.experimental import pallas as pl
from jax.experimental.pallas import tpu as pltpu

F32 = jnp.float32
BF16 = jnp.bfloat16

EPS = 1e-6
GRID_W = 64
ROPE_THETA = 10000.0
FOURIER_GD = 64
ATT_HEADS = 8
ATT_KV_HEADS = 2
ATT_GROUP = ATT_HEADS // ATT_KV_HEADS
HEAD_DIM = 64
HG_HEADS = 4
HG_CHUNK = 64
HG_LEVELS = (32, 16, 8, 4, 2, 1)
HG_WITHIN = (1, 16, 64)
HG_MAX_BLOCK_DECAY = 80.0
DFT_N2 = 128
DFT_ROW_PAD = 8
NEG_BIG = -1e30
LOG2E = 1.4426950408889634
BOUND_SLACK = 1.05
MAX_SAFE_LOG2_SCORE = 100.0
ATTN_BOUNDED_KEYS = 2816
V7X_VMEM_LIMIT = 56 * 1024 * 1024


def _dot(a, b):
    return jnp.dot(a, b, preferred_element_type=F32)


def _dot_nt(a, b):
    return lax.dot_general(a, b, (((1,), (1,)), ((), ())), preferred_element_type=F32)


def _dot_tn(a, b):
    return lax.dot_general(a, b, (((0,), (0,)), ((), ())), preferred_element_type=F32)


def _split3(x):
    hi = x.astype(BF16)
    r1 = x - hi.astype(F32)
    mid = r1.astype(BF16)
    lo = (r1 - mid.astype(F32)).astype(BF16)
    return hi, mid, lo


def _pick_tile(n, cap, mult=128):
    best = None
    for t in range(mult, min(n, cap) + 1, mult):
        if n % t == 0:
            best = t
    if best is None:
        raise ValueError(f"no tile for {n}")
    return best


def _resident(shape):
    nd = len(shape)
    return pl.BlockSpec(shape, lambda *_: (0,) * nd, pipeline_mode=pl.Buffered(1))


def _adaln_kernel(c_ref, w_ref, b_ref, o_ref):
    cv = c_ref[...]
    a = cv * jax.nn.sigmoid(cv)
    w = w_ref[0]
    a_hi = a.astype(BF16)
    a_lo = (a - a_hi.astype(F32)).astype(BF16)
    w_hi = w.astype(BF16)
    w_lo = (w - w_hi.astype(F32)).astype(BF16)
    acc = _dot(a_hi, w_hi) + _dot(a_lo, w_hi) + _dot(a_hi, w_lo)
    o_ref[0] = acc + b_ref[0]


def _adaln(cvec, w_ada, b_ada):
    depth, d, n6 = w_ada.shape
    r8 = cvec.shape[0]
    tn = _pick_tile(n6, 1536)
    return pl.pallas_call(
        _adaln_kernel,
        grid=(depth, n6 // tn),
        in_specs=[
            pl.BlockSpec((r8, d), lambda l, j: (0, 0)),
            pl.BlockSpec((1, d, tn), lambda l, j: (l, 0, j)),
            pl.BlockSpec((1, 1, tn), lambda l, j: (l, 0, j)),
        ],
        out_specs=pl.BlockSpec((1, r8, tn), lambda l, j: (l, 0, j)),
        out_shape=jax.ShapeDtypeStruct((depth, r8, n6), F32),
        name="adaln",
        compiler_params=pltpu.CompilerParams(
            dimension_semantics=("parallel", "parallel"),
            vmem_limit_bytes=V7X_VMEM_LIMIT),
    )(cvec, w_ada, b_ada.reshape(depth, 1, n6))


def _modulated(x, mb, mc, is_ctx, d, k_shift, k_scale):
    r = lax.rsqrt(jnp.mean(x * x, axis=-1, keepdims=True) + EPS)
    sh = jnp.where(is_ctx, mc[:, k_shift * d:(k_shift + 1) * d], mb[:, k_shift * d:(k_shift + 1) * d])
    sc = jnp.where(is_ctx, mc[:, k_scale * d:(k_scale + 1) * d], mb[:, k_scale * d:(k_scale + 1) * d])
    return (x * r) * (1.0 + sc) + sh


def _inproj_kernel(x_ref, mb_ref, mc_ref, wn_ref, wt_ref, gk_ref, ckn_ref, skn_ref,
                   gq_ref, cqt_ref, sqt_ref, lbf_ref, lbb_ref,
                   fz_ref, k_ref, kf_ref, lff_ref, kb_ref, lfb_ref, v_ref, hq_ref, g_ref,
                   qt_ref, vt_ref, gd_ref, *, lc, tm, tkv, d, fw, hw):
    i = pl.program_id(1)
    x = x_ref[0]
    row = i * tm + lax.broadcasted_iota(jnp.int32, (tm, 1), 0)
    is_ctx = row < lc
    h = _modulated(x, mb_ref[0, 0], mc_ref[0, 0], is_ctx, d, 0, 1).astype(BF16)

    un = _dot(h, wn_ref[...])
    fz_ref[0] = un[:, 0:fw]

    o = fw
    kw = ATT_KV_HEADS * HEAD_DIM
    uk = un[:, o:o + kw]
    lane = lax.broadcasted_iota(jnp.int32, (1, kw), 1)
    uk2 = uk * uk
    ms = jnp.zeros_like(uk)
    for hh in range(ATT_KV_HEADS):
        sel = (lane // HEAD_DIM) == hh
        ssh = jnp.sum(jnp.where(sel, uk2, 0.0), axis=-1, keepdims=True)
        ms = jnp.where(sel, ssh * (1.0 / HEAD_DIM), ms)
    kn = uk * lax.rsqrt(ms + EPS) * gk_ref[...]
    first = (lane % 32) < 16
    partner = jnp.where(first, pltpu.roll(kn, kw - 16, axis=1), pltpu.roll(kn, 16, axis=1))
    k_ref[0] = (kn * ckn_ref[...] + partner * skn_ref[...]).astype(BF16)
    o += kw

    block_decay = [jnp.zeros((1, 1), F32) for _ in HG_WITHIN[1:]]
    for lb_ref, kk_ref, lf_ref in ((lbf_ref, kf_ref, lff_ref), (lbb_ref, kb_ref, lfb_ref)):
        z = un[:, o:o + hw]
        lb = lb_ref[...]
        kk_ref[0] = (1.0 - lb) * jax.nn.sigmoid(-z)
        lf = jnp.log(lb + (1.0 - lb) * jax.nn.sigmoid(z))
        lf_ref[0] = lf
        for gi, wsz in enumerate(HG_WITHIN[1:]):
            tot = -jnp.sum(lf.reshape(tm // wsz, wsz, hw), axis=1)
            top = jnp.max(jnp.max(tot, axis=0, keepdims=True), axis=1, keepdims=True)
            block_decay[gi] = jnp.maximum(block_decay[gi], top)
        o += hw
    grow = lax.broadcasted_iota(jnp.int32, (8, 128), 0)
    gd_ref[0, 0] = jnp.where(grow == 0, block_decay[0], jnp.where(grow == 1, block_decay[1], 0.0))
    v_ref[0] = un[:, o:o + hw].astype(BF16)
    hq_ref[0] = un[:, o + hw:o + 2 * hw]
    g_ref[0] = un[:, o + 2 * hw:o + 3 * hw]

    ut = _dot_nt(wt_ref[...], h)
    qw = ATT_HEADS * HEAD_DIM
    uq = ut[0:qw].reshape(ATT_HEADS, HEAD_DIM, tm)
    msq = jnp.mean(uq * uq, axis=1, keepdims=True)
    qn = uq * lax.rsqrt(msq + EPS) * gq_ref[...][None]
    cr, sr = cqt_ref[0:16][None], sqt_ref[0:16][None]
    cc, sc = cqt_ref[16:32][None], sqt_ref[16:32][None]
    x1r, x2r, x1c, x2c = qn[:, 0:16], qn[:, 16:32], qn[:, 32:48], qn[:, 48:64]
    qr = jnp.concatenate([x1r * cr - x2r * sr, x1r * sr + x2r * cr,
                          x1c * cc - x2c * sc, x1c * sc + x2c * cc], axis=1)
    qt_ref[0] = (qr * (HEAD_DIM ** -0.5 * LOG2E)).astype(BF16)

    uv = ut[qw:qw + kw].reshape(ATT_KV_HEADS, HEAD_DIM, tm)
    ones_row = lax.broadcasted_iota(jnp.int32, (ATT_KV_HEADS, 16, tm), 1) == 0
    aug = jnp.where(ones_row, 1.0, 0.0).astype(F32)
    vaug = jnp.concatenate([uv, aug], axis=1).astype(BF16)
    for cb in range(tm // tkv):
        vt_ref[0, :, cb] = vaug[:, :, cb * tkv:(cb + 1) * tkv]


def _inproj(xs, mod4, l, nb, wn, wt, gk, ckn, skn, gq, cqt, sqt, lbf, lbb, lc, tm, tkv):
    b, t, d = xs.shape
    n6 = mod4.shape[-1]
    fw = FOURIER_GD * 4
    hw = lbf.shape[-1]
    kw = ATT_KV_HEADS * HEAD_DIM
    tok = lambda w, dt: jax.ShapeDtypeStruct((b, t, w), dt)
    tok_spec = lambda w: pl.BlockSpec((1, tm, w), lambda bi, i: (bi, i, 0))
    out_shape = [tok(fw, F32), tok(kw, BF16), tok(hw, F32), tok(hw, F32), tok(hw, F32), tok(hw, F32),
                 tok(hw, BF16), tok(hw, F32), tok(hw, F32),
                 jax.ShapeDtypeStruct((b, ATT_HEADS, HEAD_DIM, t), BF16),
                 jax.ShapeDtypeStruct((b, ATT_KV_HEADS, t // tkv, HEAD_DIM + 16, tkv), BF16),
                 jax.ShapeDtypeStruct((b, t // tm, 8, 128), F32)]
    out_specs = [tok_spec(fw), tok_spec(kw)] + [tok_spec(hw)] * 7 + [
        pl.BlockSpec((1, ATT_HEADS, HEAD_DIM, tm), lambda bi, i: (bi, 0, 0, i)),
        pl.BlockSpec((1, ATT_KV_HEADS, tm // tkv, HEAD_DIM + 16, tkv), lambda bi, i: (bi, 0, i, 0, 0)),
        pl.BlockSpec((1, 1, 8, 128), lambda bi, i: (bi, i, 0, 0))]
    in_specs = [
        pl.BlockSpec((1, tm, d), lambda bi, i: (bi, i, 0)),
        pl.BlockSpec((1, 1, 1, n6), lambda bi, i: (l, bi, 0, 0)),
        pl.BlockSpec((1, 1, 1, n6), lambda bi, i: (l, nb, 0, 0)),
        _resident(wn.shape), _resident(wt.shape), _resident(gk.shape),
        pl.BlockSpec((tm, kw), lambda bi, i: (i, 0)),
        pl.BlockSpec((tm, kw), lambda bi, i: (i, 0)),
        _resident(gq.shape),
        pl.BlockSpec((32, tm), lambda bi, i: (0, i)),
        pl.BlockSpec((32, tm), lambda bi, i: (0, i)),
        _resident(lbf.shape), _resident(lbb.shape),
    ]
    return pl.pallas_call(
        functools.partial(_inproj_kernel, lc=lc, tm=tm, tkv=tkv, d=d, fw=fw, hw=hw),
        grid=(b, t // tm),
        in_specs=in_specs, out_specs=out_specs, out_shape=out_shape,
        name="inproj",
        compiler_params=pltpu.CompilerParams(
            dimension_semantics=("parallel", "parallel"),
            vmem_limit_bytes=V7X_VMEM_LIMIT),
    )(xs, mod4, mod4, wn, wt, gk, ckn, skn, gq, cqt, sqt, lbf, lbb)


def _fourier_kernel(z_ref, f1_ref, twc_ref, tws_ref, c2_ref, s2_ref, cs_ref, cl_ref,
                    y_ref, a1_ref, p_ref, *, lc, n1, scale_c, scale_x):
    n2 = DFT_N2
    cs = cs_ref[...]
    pa, pp = 2 * n1 + DFT_ROW_PAD, n2 + DFT_ROW_PAD

    zc = z_ref[0, 0:lc, :].astype(BF16)
    pc = _dot(cl_ref[...], zc)
    pcat = jnp.concatenate([pc[0:lc], pc[lc:2 * lc]], axis=1).astype(BF16)
    y_ref[0, 0:lc, :] = (_dot(pcat, cs) * scale_c).astype(y_ref.dtype)

    f1 = f1_ref[...]

    def stage1(j, carry):
        xj = z_ref[0, pl.ds(lc + j, n1, stride=n2), :].astype(BF16)
        a1_ref[pl.ds(pl.multiple_of(j * pa, 8), 2 * n1), :] = _dot(f1, xj)
        return carry

    lax.fori_loop(0, n2, stage1, 0, unroll=8)

    c2 = c2_ref[...]
    s2 = s2_ref[...]

    def stage2(k1, carry):
        ar = a1_ref[pl.ds(k1, n2, stride=pa), :]
        ai = a1_ref[pl.ds(n1 + k1, n2, stride=pa), :]
        twc = twc_ref[pl.ds(k1, 1), :]
        tws = tws_ref[pl.ds(k1, 1), :]
        gr = c2 * twc - s2 * tws
        gi = -(s2 * twc + c2 * tws)
        gm = jnp.concatenate([jnp.concatenate([gr, -gi], axis=1),
                              jnp.concatenate([gi, gr], axis=1)], axis=0).astype(BF16)
        rhs = jnp.concatenate([ar, ai], axis=0).astype(BF16)
        pri = _dot(gm, rhs)
        pcat2 = jnp.concatenate([pri[0:n2], pri[n2:2 * n2]], axis=1).astype(BF16)
        p_ref[pl.ds(pl.multiple_of(k1 * pp, 8), n2), :] = _dot(pcat2, cs) * scale_x
        return carry

    lax.fori_loop(0, n1, stage2, 0, unroll=8)

    def stage3(k2, carry):
        blk = p_ref[pl.ds(k2, n1, stride=pp), :]
        y_ref[0, pl.ds(pl.multiple_of(lc + k2 * n1, n1), n1), :] = blk.astype(y_ref.dtype)
        return carry

    lax.fori_loop(0, n2, stage3, 0, unroll=8)


def _fourier_consts(lc, s):
    n2 = DFT_N2
    n1 = s // n2
    f64 = np.float64
    k = np.arange(n1, dtype=f64)
    a1 = 2 * np.pi * np.outer(k, k) / n1
    f1 = np.concatenate([np.cos(a1), -np.sin(a1)], axis=0)
    at = 2 * np.pi * np.outer(np.arange(n1, dtype=f64), np.arange(n2, dtype=f64)) / s
    k2 = np.arange(n2, dtype=f64)
    a2 = 2 * np.pi * np.outer(k2, k2) / n2
    w = 2 * FOURIER_GD
    ch = np.arange(w)
    same = (ch[:, None] // FOURIER_GD) == (ch[None, :] // FOURIER_GD)
    ag = 2 * np.pi * np.outer(ch % FOURIER_GD, ch % FOURIER_GD) / FOURIER_GD
    cs = np.concatenate([np.where(same, np.cos(ag), 0.0), np.where(same, np.sin(ag), 0.0)], axis=0)
    kc = np.arange(lc, dtype=f64)
    al = 2 * np.pi * np.outer(kc, kc) / lc
    cl = np.concatenate([np.cos(al), -np.sin(al)], axis=0)
    return dict(
        f1=jnp.asarray(f1, BF16), twc=jnp.asarray(np.cos(at), F32), tws=jnp.asarray(np.sin(at), F32),
        c2=jnp.asarray(np.cos(a2), F32), s2=jnp.asarray(np.sin(a2), F32),
        cs=jnp.asarray(cs, BF16), cl=jnp.asarray(cl, BF16), n1=n1)


def _fourier(fz, fc, lc):
    b, t, fw = fz.shape
    s = t - lc
    n1 = fc["n1"]
    w = 2 * FOURIER_GD
    consts = [fc[k] for k in ("f1", "twc", "tws", "c2", "s2", "cs", "cl")]
    return pl.pallas_call(
        functools.partial(_fourier_kernel, lc=lc, n1=n1,
                          scale_c=float((lc * FOURIER_GD) ** -0.5),
                          scale_x=float((s * FOURIER_GD) ** -0.5)),
        grid=(b, fw // w),
        in_specs=[pl.BlockSpec((1, t, w), lambda bi, hi: (bi, 0, hi))] +
                 [_resident(c.shape) for c in consts],
        out_specs=pl.BlockSpec((1, t, w), lambda bi, hi: (bi, 0, hi)),
        out_shape=jax.ShapeDtypeStruct((b, t, fw), BF16),
        scratch_shapes=[pltpu.VMEM((DFT_N2 * (2 * n1 + DFT_ROW_PAD), w), F32),
                        pltpu.VMEM((n1 * (DFT_N2 + DFT_ROW_PAD), w), F32)],
        name="fourier",
        compiler_params=pltpu.CompilerParams(
            dimension_semantics=("parallel", "parallel"),
            vmem_limit_bytes=V7X_VMEM_LIMIT),
    )(fz, *consts)


def _place_queries(q_ref, qs_ref):
    g = pl.program_id(1)
    qcat = jnp.concatenate([q_ref[0, j] for j in range(ATT_GROUP)], axis=1)
    zero = jnp.zeros_like(qcat)
    for gg in range(ATT_KV_HEADS):
        @pl.when(g == gg)
        def _place():
            for hh in range(ATT_KV_HEADS):
                qs_ref[hh * HEAD_DIM:(hh + 1) * HEAD_DIM, :] = qcat if hh == gg else zero


def _score_block(k_ref, qs_ref, blk, tk, lc, masked):
    kb = k_ref[0, pl.ds(pl.multiple_of(blk * tk, tk), tk), :]
    s = _dot(kb, qs_ref[...])
    if masked:
        kidx = blk * tk + lax.broadcasted_iota(jnp.int32, (tk, 1), 0)
        s = jnp.where(kidx < lc, s, NEG_BIG)
    return s


def _attn_finish(acc_ref, o_ref, tq):
    acc = acc_ref[...]
    o = acc[0:HEAD_DIM] / acc[HEAD_DIM:HEAD_DIM + 1]
    o_ref[0] = jnp.concatenate([o[:, j * tq:(j + 1) * tq] for j in range(ATT_GROUP)],
                               axis=0).astype(o_ref.dtype)


def _attn_bounded_kernel(q_ref, k_ref, v_ref, o_ref, qs_ref, p0_ref, p1_ref, acc_ref,
                         *, lc, tq, tk, nqc, nkc, nk):
    qi = pl.program_id(2)
    _place_queries(q_ref, qs_ref)
    acc_ref[...] = jnp.zeros(acc_ref.shape, F32)

    p_ref = (p0_ref, p1_ref)
    tkv = v_ref.shape[-1]
    nsub = tk // tkv

    def probs(blk, slot, masked=False):
        p_ref[slot][...] = jnp.exp2(_score_block(k_ref, qs_ref, blk, tk, lc, masked)).astype(BF16)

    def weighted(blk, slot):
        pv = _dot(v_ref[0, 0, blk * nsub], p_ref[slot][0:tkv, :])
        for c in range(1, nsub):
            pv = pv + _dot(v_ref[0, 0, blk * nsub + c], p_ref[slot][c * tkv:(c + 1) * tkv, :])
        acc_ref[...] += pv

    @pl.when(qi < nqc)
    def _ctx():
        for blk in range(nkc):
            probs(blk, 0, masked=True)
            weighted(blk, 0)

    @pl.when(qi >= nqc)
    def _lat():
        def steady(t, slot):
            probs(t + 1, 1 - slot)
            weighted(t, slot)

        probs(0, 0)
        npairs = (nk - 1) // 2

        def pair(i, carry):
            steady(2 * i, 0)
            steady(2 * i + 1, 1)
            return carry

        lax.fori_loop(0, npairs, pair, 0)
        if (nk - 1) % 2:
            steady(nk - 2, (nk - 2) % 2)
        weighted(nk - 1, (nk - 1) % 2)

    _attn_finish(acc_ref, o_ref, tq)


def _attn_kernel(q_ref, k_ref, v_ref, o_ref, qs_ref, s0_ref, s1_ref, mb0_ref, mb1_ref, p0_ref, p1_ref,
                 al0_ref, al1_ref, m_ref, acc_ref, *, lc, tq, tk, nqc, nkc, nk):
    s_ref, mb_ref = (s0_ref, s1_ref), (mb0_ref, mb1_ref)
    p_ref, al_ref = (p0_ref, p1_ref), (al0_ref, al1_ref)
    qi = pl.program_id(2)

    _place_queries(q_ref, qs_ref)
    m_ref[...] = jnp.full(m_ref.shape, NEG_BIG, F32)
    acc_ref[...] = jnp.zeros(acc_ref.shape, F32)

    def scores(blk, slot, masked=False):
        s = _score_block(k_ref, qs_ref, blk, tk, lc, masked)
        s_ref[slot][...] = s
        mb_ref[slot][...] = jnp.max(s, axis=0, keepdims=True)

    def probs(slot):
        m_prev = m_ref[...]
        m_new = jnp.maximum(m_prev, mb_ref[slot][...])
        al_ref[slot][...] = jnp.exp2(m_prev - m_new)
        p_ref[slot][...] = jnp.exp2(s_ref[slot][...] - m_new).astype(BF16)
        m_ref[...] = m_new

    def weighted(blk, slot):
        acc_ref[...] = al_ref[slot][...] * acc_ref[...] + _dot(v_ref[0, 0, blk], p_ref[slot][...])

    @pl.when(qi < nqc)
    def _ctx():
        for blk in range(nkc):
            scores(blk, 0, masked=True)
            probs(0)
            weighted(blk, 0)

    @pl.when(qi >= nqc)
    def _lat():
        def steady(t, slot):
            scores(t + 2, slot)
            weighted(t, slot)
            probs(1 - slot)

        scores(0, 0)
        scores(1, 1)
        probs(0)
        npairs = (nk - 2) // 2

        def pair(i, carry):
            steady(2 * i, 0)
            steady(2 * i + 1, 1)
            return carry

        lax.fori_loop(0, npairs, pair, 0)
        if (nk - 2) % 2:
            steady(2 * npairs, 0)
        weighted(nk - 2, (nk - 2) % 2)
        probs((nk - 1) % 2)
        weighted(nk - 1, (nk - 1) % 2)

    _attn_finish(acc_ref, o_ref, tq)


def _attention(qt, kn, vt, lc, tq, bounded):
    b, _, _, t = qt.shape
    nkv, tkv = vt.shape[2], vt.shape[4]
    tk = _pick_tile(t, ATTN_BOUNDED_KEYS, mult=tkv) if bounded else tkv
    nk = t // tk
    nq = t // tq
    nqc = lc // tq
    nkc = -(-lc // tk)
    wide = ATT_GROUP * tq
    qs_scratch = pltpu.VMEM((ATT_KV_HEADS * HEAD_DIM, wide), BF16)
    acc_scratch = pltpu.VMEM((HEAD_DIM + 16, wide), F32)
    if bounded:
        body = _attn_bounded_kernel
        scratch = [qs_scratch, pltpu.VMEM((tk, wide), BF16), pltpu.VMEM((tk, wide), BF16), acc_scratch]
    else:
        assert nk >= 2
        body = _attn_kernel
        scratch = [qs_scratch,
                   pltpu.VMEM((tk, wide), F32), pltpu.VMEM((tk, wide), F32),
                   pltpu.VMEM((1, wide), F32), pltpu.VMEM((1, wide), F32),
                   pltpu.VMEM((tk, wide), BF16), pltpu.VMEM((tk, wide), BF16),
                   pltpu.VMEM((1, wide), F32), pltpu.VMEM((1, wide), F32),
                   pltpu.VMEM((1, wide), F32),
                   acc_scratch]
    return pl.pallas_call(
        functools.partial(body, lc=lc, tq=tq, tk=tk, nqc=nqc, nkc=nkc, nk=nk),
        grid=(b, ATT_KV_HEADS, nq),
        in_specs=[
            pl.BlockSpec((1, ATT_GROUP, HEAD_DIM, tq), lambda bi, g, qi: (bi, g, 0, qi)),
            pl.BlockSpec((1, t, ATT_KV_HEADS * HEAD_DIM), lambda bi, g, qi: (bi, 0, 0)),
            pl.BlockSpec((1, 1, nkv, HEAD_DIM + 16, tkv), lambda bi, g, qi: (bi, g, 0, 0, 0)),
        ],
        out_specs=pl.BlockSpec((1, ATT_GROUP * HEAD_DIM, tq), lambda bi, g, qi: (bi, g, qi)),
        out_shape=jax.ShapeDtypeStruct((b, ATT_HEADS * HEAD_DIM, t), BF16),
        scratch_shapes=scratch,
        name="attn_bounded" if bounded else "attn_online",
        compiler_params=pltpu.CompilerParams(
            dimension_semantics=("parallel", "parallel", "arbitrary"),
            vmem_limit_bytes=V7X_VMEM_LIMIT),
    )(qt, kn, vt)


def _hgrn_consts():
    c = HG_CHUNK
    t = np.arange(c)
    tt, rr = t[:, None], t[None, :]
    mats, masks = [], []
    for rev in (False, True):
        mats.append(np.concatenate([(rr >= tt) if rev else (rr <= tt), np.ones((8, c), bool)], axis=0))
        pair_masks = []
        for bsz in HG_LEVELS:
            blk, pos = t // (2 * bsz), t % (2 * bsz)
            qv = (pos < bsz) if rev else (pos >= bsz)
            pair_masks.append((blk[:, None] == blk[None, :]) & qv[:, None] & ~qv[None, :])
        for wsz in HG_WITHIN:
            same = (tt // wsz) == (rr // wsz)
            pair_masks.append(same & ((rr >= tt) if rev else (rr <= tt)))
        masks.append(np.stack([np.tile(m, (HG_HEADS, 1)) for m in pair_masks]))
    return jnp.asarray(np.stack(mats), BF16), jnp.asarray(np.stack(masks), F32)


def _level_reference(cum, bsz, rev):
    c, hw = cum.shape
    i = bsz if rev else bsz - 1
    if 2 * bsz >= 8:
        c3 = cum.reshape(c // (2 * bsz), 2 * bsz, hw)
        return jnp.broadcast_to(c3[:, i:i + 1, :], c3.shape).reshape(c, hw)
    c3 = cum.reshape(c // 8, 8, hw)
    sub = lax.broadcasted_iota(jnp.int32, (1, 8, 1), 1) // (2 * bsz)
    ref = jnp.broadcast_to(c3[:, i:i + 1, :], c3.shape)
    for gidx in range(1, 8 // (2 * bsz)):
        r = gidx * 2 * bsz + i
        ref = jnp.where(sub == gidx, jnp.broadcast_to(c3[:, r:r + 1, :], c3.shape), ref)
    return ref.reshape(c, hw)


def _block_entry_reference(cum, wsz, rev):
    c, hw = cum.shape
    nb = c // wsz
    if nb == 1:
        return jnp.zeros_like(cum)
    c3 = cum.reshape(nb, wsz, hw)
    zero = jnp.zeros((1, 1, hw), F32)
    if rev:
        edge = jnp.concatenate([c3[1:, 0:1, :], zero], axis=0)
    else:
        edge = jnp.concatenate([zero, c3[:-1, wsz - 1:wsz, :]], axis=0)
    return jnp.broadcast_to(edge, c3.shape).reshape(c, hw)


def _hgrn_chunk(d, r0, kk_ref, lf_ref, v_ref, q_ref, cm_ref, am_ref, o_ref, s_ref, hw, within):
    c = HG_CHUNK
    dk = hw // HG_HEADS
    rev = d == 1
    rows = pl.ds(r0, c)
    lf = lf_ref[0, rows, :]
    cm = cm_ref[d]
    hi, mid, lo = _split3(lf)
    e = _dot(cm, hi) + _dot(cm, mid) + _dot(cm, lo)
    cum, tot = e[0:c], e[c:c + 1]
    q = q_ref[0, rows, :]
    k = kk_ref[0, rows, :]
    v = v_ref[0, rows, :]
    qdec = q * jnp.exp(cum)
    kdec = k * jnp.exp(tot - cum)
    ds = jnp.exp(tot)

    lane_head = lax.broadcasted_iota(jnp.int32, (1, hw), 1) // dk
    heads = [lane_head == hh for hh in range(HG_HEADS)]
    tpos = lax.broadcasted_iota(jnp.int32, (c, 1), 0)

    def pair_product(qd, kd, mask_idx):
        q4 = jnp.concatenate([jnp.where(hm, qd, 0.0) for hm in heads], axis=0).astype(BF16)
        return am_ref[d, mask_idx] * _dot_nt(q4, kd.astype(BF16))

    att = jnp.zeros((HG_HEADS * c, c), F32)
    for lv, bsz in enumerate(HG_LEVELS):
        if bsz < within:
            continue
        pos = tpos % (2 * bsz)
        is_q = (pos < bsz) if rev else (pos >= bsz)
        diff = cum - _level_reference(cum, bsz, rev)
        w = jnp.exp(jnp.where(is_q, diff, -diff))
        att = att + pair_product(jnp.where(is_q, q * w, 0.0), jnp.where(is_q, 0.0, k * w), lv)
    widx = len(HG_LEVELS) + HG_WITHIN.index(within)
    if within == 1:
        att = att + pair_product(q, k, widx)
    else:
        diff = cum - _block_entry_reference(cum, within, rev)
        att = att + pair_product(q * jnp.exp(diff), k * jnp.exp(-diff), widx)

    ov = _dot(att.astype(BF16), v)
    o = jnp.zeros((c, hw), F32)
    for hh, hm in enumerate(heads):
        o = o + jnp.where(hm, ov[hh * c:(hh + 1) * c], 0.0)

    st = s_ref[...]
    o = o + _dot_nt(qdec.astype(BF16), st.astype(BF16))
    o_ref[0, rows, :] = o
    u = _dot_tn(v, kdec.astype(BF16))
    rh = lax.broadcasted_iota(jnp.int32, (hw, 1), 0) // dk
    s_ref[...] = st * ds + jnp.where(rh == lane_head, u, 0.0)


def _hgrn_kernel(kf_ref, lff_ref, vf_ref, qf_ref, kb_ref, lfb_ref, vb_ref, qb_ref,
                 cm_ref, am_ref, of_ref, ob_ref, sf_ref, sb_ref, *, nchunk, hw, within):
    @pl.when(pl.program_id(1) == 0)
    def _init():
        sf_ref[...] = jnp.zeros(sf_ref.shape, F32)
        sb_ref[...] = jnp.zeros(sb_ref.shape, F32)

    for jj in range(nchunk):
        _hgrn_chunk(0, jj * HG_CHUNK, kf_ref, lff_ref, vf_ref, qf_ref,
                    cm_ref, am_ref, of_ref, sf_ref, hw, within)
        _hgrn_chunk(1, (nchunk - 1 - jj) * HG_CHUNK, kb_ref, lfb_ref, vb_ref, qb_ref,
                    cm_ref, am_ref, ob_ref, sb_ref, hw, within)


def _hgrn(kf, lff, kb, lfb, v, hq, hc, lc, nchunk, within):
    b, t, hw = kf.shape
    tb = nchunk * HG_CHUNK
    nblk, ncb = t // tb, lc // tb
    cm, am = hc
    fwd = lambda bi, p: (bi, p, 0)
    bwd = lambda bi, p: (bi, jnp.where(p < ncb, ncb - 1 - p, nblk - 1 - (p - ncb)), 0)
    blk = lambda im: pl.BlockSpec((1, tb, hw), im)
    return pl.pallas_call(
        functools.partial(_hgrn_kernel, nchunk=nchunk, hw=hw, within=within),
        grid=(b, nblk),
        in_specs=[blk(fwd)] * 4 + [blk(bwd)] * 4 + [_resident(cm.shape), _resident(am.shape)],
        out_specs=[blk(fwd), blk(bwd)],
        out_shape=[jax.ShapeDtypeStruct((b, t, hw), F32)] * 2,
        scratch_shapes=[pltpu.VMEM((hw, hw), F32), pltpu.VMEM((hw, hw), F32)],
        name=f"hgrn_within{within}",
        compiler_params=pltpu.CompilerParams(
            dimension_semantics=("parallel", "arbitrary"),
            vmem_limit_bytes=V7X_VMEM_LIMIT),
    )(kf, lff, v, hq, kb, lfb, v, hq, cm, am)


def _outproj_kernel(x_ref, mb_ref, mc_ref, y_ref, at_ref, of_ref, ob_ref, g_ref, wf_ref, wo_ref,
                    gn_ref, o_ref, *, lc, tm, d, fw, aw, hw):
    i = pl.program_id(1)
    row = i * tm + lax.broadcasted_iota(jnp.int32, (tm, 1), 0)
    is_ctx = row < lc
    mb, mc = mb_ref[0, 0], mc_ref[0, 0]
    gate = jnp.where(is_ctx, mc[:, 2 * d:3 * d], mb[:, 2 * d:3 * d])

    fx = _dot(y_ref[0], wf_ref[...]).astype(BF16)
    mix = _dot(fx, wo_ref[0:fw, :])
    mix = mix + _dot_tn(at_ref[0], wo_ref[fw:fw + aw, :])

    o = of_ref[0] + ob_ref[0]
    dv = hw // HG_HEADS
    lane_head = lax.broadcasted_iota(jnp.int32, (1, hw), 1) // dv
    o2 = o * o
    ms = jnp.zeros_like(o)
    for hh in range(HG_HEADS):
        sel = lane_head == hh
        ssh = jnp.sum(jnp.where(sel, o2, 0.0), axis=-1, keepdims=True)
        ms = jnp.where(sel, ssh * (1.0 / dv), ms)
    gg = g_ref[0]
    rx = (o * lax.rsqrt(ms + EPS) * gn_ref[...]) * (gg * jax.nn.sigmoid(gg))
    mix = mix + _dot(rx.astype(BF16), wo_ref[fw + aw:fw + aw + hw, :])
    o_ref[0] = x_ref[0] + gate * mix


def _outproj(xs, mod4, l, nb, y, at, of, ob, g, wf, wo, gn, lc, tm):
    b, t, d = xs.shape
    n6 = mod4.shape[-1]
    fw, hw = y.shape[-1], of.shape[-1]
    aw = at.shape[1]
    tok = lambda w: pl.BlockSpec((1, tm, w), lambda bi, i: (bi, i, 0))
    return pl.pallas_call(
        functools.partial(_outproj_kernel, lc=lc, tm=tm, d=d, fw=fw, aw=aw, hw=hw),
        grid=(b, t // tm),
        in_specs=[tok(d),
                  pl.BlockSpec((1, 1, 1, n6), lambda bi, i: (l, bi, 0, 0)),
                  pl.BlockSpec((1, 1, 1, n6), lambda bi, i: (l, nb, 0, 0)),
                  tok(fw),
                  pl.BlockSpec((1, aw, tm), lambda bi, i: (bi, 0, i)),
                  tok(hw), tok(hw), tok(hw),
                  _resident(wf.shape), _resident(wo.shape), _resident(gn.shape)],
        out_specs=tok(d),
        out_shape=jax.ShapeDtypeStruct((b, t, d), F32),
        name="outproj",
        compiler_params=pltpu.CompilerParams(
            dimension_semantics=("parallel", "parallel"),
            vmem_limit_bytes=V7X_VMEM_LIMIT),
    )(xs, mod4, mod4, y, at, of, ob, g, wf, wo, gn)


def _ffn_kernel(x_ref, mb_ref, mc_ref, wg_ref, wu_ref, wd_ref, o_ref, *, lc, tm, d, fchunk):
    i = pl.program_id(1)
    x = x_ref[0]
    row = i * tm + lax.broadcasted_iota(jnp.int32, (tm, 1), 0)
    is_ctx = row < lc
    mb, mc = mb_ref[0, 0], mc_ref[0, 0]
    h = _modulated(x, mb, mc, is_ctx, d, 3, 4).astype(BF16)
    gate = jnp.where(is_ctx, mc[:, 5 * d:6 * d], mb[:, 5 * d:6 * d])
    dff = wg_ref.shape[1]
    acc = jnp.zeros((tm, d), F32)
    for c0 in range(0, dff, fchunk):
        a = _dot(h, wg_ref[:, c0:c0 + fchunk])
        u = _dot(h, wu_ref[:, c0:c0 + fchunk])
        act = (a * jax.nn.sigmoid(a) * u).astype(BF16)
        acc = acc + _dot(act, wd_ref[c0:c0 + fchunk, :])
    o_ref[0] = x + gate * acc


def _ffn(xs, mod4, l, nb, wg, wu, wd, lc, tm):
    b, t, d = xs.shape
    n6 = mod4.shape[-1]
    dff = wg.shape[1]
    fchunk = _pick_tile(dff, 768)
    tok = pl.BlockSpec((1, tm, d), lambda bi, i: (bi, i, 0))
    return pl.pallas_call(
        functools.partial(_ffn_kernel, lc=lc, tm=tm, d=d, fchunk=fchunk),
        grid=(b, t // tm),
        in_specs=[tok,
                  pl.BlockSpec((1, 1, 1, n6), lambda bi, i: (l, bi, 0, 0)),
                  pl.BlockSpec((1, 1, 1, n6), lambda bi, i: (l, nb, 0, 0)),
                  _resident(wg.shape), _resident(wu.shape), _resident(wd.shape)],
        out_specs=tok,
        out_shape=jax.ShapeDtypeStruct((b, t, d), F32),
        name="ffn",
        compiler_params=pltpu.CompilerParams(
            dimension_semantics=("parallel", "parallel"),
            vmem_limit_bytes=V7X_VMEM_LIMIT),
    )(xs, mod4, mod4, wg, wu, wd)


def _final_kernel(x_ref, g_ref, o_ref):
    x = x_ref[0]
    o_ref[0] = x * lax.rsqrt(jnp.mean(x * x, axis=-1, keepdims=True) + EPS) * g_ref[...]


def _final_norm(xs, gain, lc, tm):
    b, t, d = xs.shape
    s = t - lc
    off = lc // tm
    return pl.pallas_call(
        _final_kernel,
        grid=(b, s // tm),
        in_specs=[pl.BlockSpec((1, tm, d), lambda bi, i: (bi, i + off, 0)),
                  pl.BlockSpec((1, d), lambda bi, i: (0, 0))],
        out_specs=pl.BlockSpec((1, tm, d), lambda bi, i: (bi, i, 0)),
        out_shape=jax.ShapeDtypeStruct((b, s, d), F32),
        name="final_norm",
        compiler_params=pltpu.CompilerParams(dimension_semantics=("parallel", "parallel")),
    )(xs, gain.reshape(1, d))


def _rope_tables(lc, s):
    half = HEAD_DIM // 2
    n = jnp.arange(s)
    freqs = ROPE_THETA ** (-jnp.arange(0, half, 2, dtype=F32) / half)
    ang_r = (n // GRID_W).astype(F32)[:, None] * freqs
    ang_c = (n % GRID_W).astype(F32)[:, None] * freqs
    pad = lambda a, v: jnp.concatenate([jnp.full((lc, a.shape[1]), v, F32), a], axis=0)
    cr, sr = pad(jnp.cos(ang_r), 1.0), pad(jnp.sin(ang_r), 0.0)
    cc, sc = pad(jnp.cos(ang_c), 1.0), pad(jnp.sin(ang_c), 0.0)
    cos_h = jnp.concatenate([cr, cr, cc, cc], axis=1)
    sin_h = jnp.concatenate([-sr, sr, -sc, sc], axis=1)
    ckn = jnp.tile(cos_h, (1, ATT_KV_HEADS))
    skn = jnp.tile(sin_h, (1, ATT_KV_HEADS))
    cqt = jnp.concatenate([cr, cc], axis=1).T
    sqt = jnp.concatenate([sr, sc], axis=1).T
    return ckn, skn, cqt, sqt


def kernel(x, c, ctx, c_ctx, w_ada, b_ada, w_in, w_four, q_norm, k_norm, hg_lb_logits, hg_norm,
           w_out, w_gate, w_up, w_down, final_norm):
    b, s, d = x.shape
    lc = ctx.shape[1]
    t = lc + s
    depth = w_ada.shape[0]
    fw = w_four.shape[-1]
    hw = hg_lb_logits.shape[-1]
    qw = ATT_HEADS * HEAD_DIM
    kw = ATT_KV_HEADS * HEAD_DIM
    assert w_in.shape[-1] == fw + qw + 2 * kw + 5 * hw
    assert s % (DFT_N2 * 8) == 0 and lc % 128 == 0 and s % GRID_W == 0

    tq = 256 if (lc % 256 == 0 and t % 256 == 0) else 128
    tkv = tq
    tm = _pick_tile(t, 768, mult=tkv)
    nchunk = max(g for g in (4, 2, 1) if lc % (g * HG_CHUNK) == 0 and t % (g * HG_CHUNK) == 0)
    tfin = 256 if (lc % 256 == 0 and s % 256 == 0) else 128

    xs = jnp.concatenate([ctx, x], axis=1)
    r8 = -(-(b + 1) // 8) * 8
    cvec = jnp.concatenate([c, c_ctx[None], jnp.zeros((r8 - b - 1, d), F32)], axis=0)
    mod = _adaln(cvec, w_ada, b_ada)
    mod4 = mod.reshape(depth, r8, 1, mod.shape[-1])

    lb_sm = jax.nn.softmax(hg_lb_logits.astype(F32), axis=1)
    lb_all = jnp.cumsum(lb_sm, axis=1) - lb_sm[:, :1]

    ckn, skn, cqt, sqt = _rope_tables(lc, s)
    fc = _fourier_consts(lc, s)
    hc = _hgrn_consts()

    o_q, o_k, o_v, o_h = fw, fw + qw, fw + qw + kw, fw + qw + 2 * kw
    for l in range(depth):
        wl = w_in[l]
        wn = jnp.concatenate([wl[:, 0:fw], wl[:, o_k:o_v], wl[:, o_h:]], axis=1).astype(BF16)
        wt = jnp.concatenate([wl[:, o_q:o_k], wl[:, o_v:o_h]], axis=1).T.astype(BF16)
        gk = jnp.tile(k_norm[l], ATT_KV_HEADS)[None]
        gq = q_norm[l][:, None]
        gn = jnp.tile(hg_norm[l], HG_HEADS)[None]
        (fz, kn, kf, lff, kb, lfb, v, hq, g, qt, vt, gd) = _inproj(
            xs, mod4, l, b, wn, wt, gk, ckn, skn, gq, cqt, sqt,
            lb_all[0, l][None], lb_all[1, l][None], lc, tm, tkv)
        y = _fourier(fz, fc, lc)
        bound = (HEAD_DIM ** 0.5 * LOG2E * BOUND_SLACK) * jnp.max(jnp.abs(q_norm[l])) * jnp.max(jnp.abs(k_norm[l]))
        at = lax.cond(bound < MAX_SAFE_LOG2_SCORE,
                      functools.partial(_attention, lc=lc, tq=tq, bounded=True),
                      functools.partial(_attention, lc=lc, tq=tq, bounded=False),
                      qt, kn, vt)
        hg = lambda within: functools.partial(_hgrn, hc=hc, lc=lc, nchunk=nchunk, within=within)
        decay16, decay64 = jnp.max(gd[:, :, 0, 0]), jnp.max(gd[:, :, 1, 0])
        hg_in = (kf, lff, kb, lfb, v, hq)
        of, ob = lax.cond(
            decay64 < HG_MAX_BLOCK_DECAY, hg(64),
            lambda *a: lax.cond(decay16 < HG_MAX_BLOCK_DECAY, hg(16), hg(1), *a),
            *hg_in)
        xs = _outproj(xs, mod4, l, b, y, at, of, ob, g, w_four[l].astype(BF16),
                      w_out[l].astype(BF16), gn, lc, tm)
        xs = _ffn(xs, mod4, l, b, w_gate[l].astype(BF16), w_up[l].astype(BF16),
                  w_down[l].astype(BF16), lc, tm)
    return _final_norm(xs, final_norm, lc, tfin)
```

```python
import functools

import numpy as np
import jax
import jax.numpy as jnp
from jax import lax
from jax.experimental import pallas as pl
from jax.experimental.pallas import tpu as pltpu

F32 = jnp.float32
BF16 = jnp.bfloat16

EPS = 1e-6
GRID_W = 64
ROPE_THETA = 10000.0
FOURIER_GD = 64
ATT_HEADS = 8
ATT_KV_HEADS = 2
ATT_GROUP = ATT_HEADS // ATT_KV_HEADS
HEAD_DIM = 64
HG_HEADS = 4
HG_CHUNK = 64
HG_LEVELS = (32, 16, 8, 4, 2, 1)
HG_WITHIN = (1, 16, 64)
HG_MAX_BLOCK_DECAY = 80.0
DFT_N2 = 128
DFT_ROW_PAD = 8
NEG_BIG = -1e30
LOG2E = 1.4426950408889634
BOUND_SLACK = 1.05
MAX_SAFE_LOG2_SCORE = 100.0
ATTN_BOUNDED_KEYS = 2816
V7X_VMEM_LIMIT = 56 * 1024 * 1024


def _dot(a, b):
    return jnp.dot(a, b, preferred_element_type=F32)


def _dot_nt(a, b):
    return lax.dot_general(a, b, (((1,), (1,)), ((), ())), preferred_element_type=F32)


def _dot_tn(a, b):
    return lax.dot_general(a, b, (((0,), (0,)), ((), ())), preferred_element_type=F32)


def _split3(x):
    hi = x.astype(BF16)
    r1 = x - hi.astype(F32)
    mid = r1.astype(BF16)
    lo = (r1 - mid.astype(F32)).astype(BF16)
    return hi, mid, lo


def _pick_tile(n, cap, mult=128):
    best = None
    for t in range(mult, min(n, cap) + 1, mult):
        if n % t == 0:
            best = t
    if best is None:
        raise ValueError(f"no tile for {n}")
    return best


def _resident(shape):
    nd = len(shape)
    return pl.BlockSpec(shape, lambda *_: (0,) * nd, pipeline_mode=pl.Buffered(1))


def _adaln_kernel(c_ref, w_ref, b_ref, o_ref):
    cv = c_ref[...]
    a = cv * jax.nn.sigmoid(cv)
    w = w_ref[0]
    a_hi = a.astype(BF16)
    a_lo = (a - a_hi.astype(F32)).astype(BF16)
    w_hi = w.astype(BF16)
    w_lo = (w - w_hi.astype(F32)).astype(BF16)
    acc = _dot(a_hi, w_hi) + _dot(a_lo, w_hi) + _dot(a_hi, w_lo)
    o_ref[0] = acc + b_ref[0]


def _adaln(cvec, w_ada, b_ada):
    depth, d, n6 = w_ada.shape
    r8 = cvec.shape[0]
    tn = _pick_tile(n6, 1536)
    return pl.pallas_call(
        _adaln_kernel,
        grid=(depth, n6 // tn),
        in_specs=[
            pl.BlockSpec((r8, d), lambda l, j: (0, 0)),
            pl.BlockSpec((1, d, tn), lambda l, j: (l, 0, j)),
            pl.BlockSpec((1, 1, tn), lambda l, j: (l, 0, j)),
        ],
        out_specs=pl.BlockSpec((1, r8, tn), lambda l, j: (l, 0, j)),
        out_shape=jax.ShapeDtypeStruct((depth, r8, n6), F32),
        name="adaln",
        compiler_params=pltpu.CompilerParams(
            dimension_semantics=("parallel", "parallel"),
            vmem_limit_bytes=V7X_VMEM_LIMIT),
    )(cvec, w_ada, b_ada.reshape(depth, 1, n6))


def _modulated(x, mb, mc, is_ctx, d, k_shift, k_scale):
    r = lax.rsqrt(jnp.mean(x * x, axis=-1, keepdims=True) + EPS)
    sh = jnp.where(is_ctx, mc[:, k_shift * d:(k_shift + 1) * d], mb[:, k_shift * d:(k_shift + 1) * d])
    sc = jnp.where(is_ctx, mc[:, k_scale * d:(k_scale + 1) * d], mb[:, k_scale * d:(k_scale + 1) * d])
    return (x * r) * (1.0 + sc) + sh


def _inproj_kernel(x_ref, mb_ref, mc_ref, wn_ref, wt_ref, gk_ref, ckn_ref, skn_ref,
                   gq_ref, cqt_ref, sqt_ref, lbf_ref, lbb_ref,
                   fz_ref, k_ref, kf_ref, lff_ref, kb_ref, lfb_ref, v_ref, hq_ref, g_ref,
                   qt_ref, vt_ref, gd_ref, *, lc, tm, tkv, d, fw, hw):
    i = pl.program_id(1)
    kw = ATT_KV_HEADS * HEAD_DIM
    qw = ATT_HEADS * HEAD_DIM
    lane = lax.broadcasted_iota(jnp.int32, (1, kw), 1)
    block_decay = [jnp.zeros((1, 1), F32) for _ in HG_WITHIN[1:]]

    for sb in range(tm // tkv):
        r = slice(sb * tkv, (sb + 1) * tkv)
        x = x_ref[0, r, :]
        row = i * tm + sb * tkv + lax.broadcasted_iota(jnp.int32, (tkv, 1), 0)
        h = _modulated(x, mb_ref[0, 0], mc_ref[0, 0], row < lc, d, 0, 1).astype(BF16)

        un = _dot(h, wn_ref[...])
        fz_ref[0, r, :] = un[:, 0:fw]

        o = fw
        uk = un[:, o:o + kw]
        uk2 = uk * uk
        ms = jnp.zeros_like(uk)
        for hh in range(ATT_KV_HEADS):
            sel = (lane // HEAD_DIM) == hh
            ssh = jnp.sum(jnp.where(sel, uk2, 0.0), axis=-1, keepdims=True)
            ms = jnp.where(sel, ssh * (1.0 / HEAD_DIM), ms)
        kn = uk * lax.rsqrt(ms + EPS) * gk_ref[...]
        first = (lane % 32) < 16
        partner = jnp.where(first, pltpu.roll(kn, kw - 16, axis=1), pltpu.roll(kn, 16, axis=1))
        k_ref[0, r, :] = (kn * ckn_ref[r, :] + partner * skn_ref[r, :]).astype(BF16)
        o += kw

        for lb_ref, kk_ref, lf_ref in ((lbf_ref, kf_ref, lff_ref), (lbb_ref, kb_ref, lfb_ref)):
            z = un[:, o:o + hw]
            lb = lb_ref[...]
            kk_ref[0, r, :] = (1.0 - lb) * jax.nn.sigmoid(-z)
            lf = jnp.log(lb + (1.0 - lb) * jax.nn.sigmoid(z))
            lf_ref[0, r, :] = lf
            for gi, wsz in enumerate(HG_WITHIN[1:]):
                tot = -jnp.sum(lf.reshape(tkv // wsz, wsz, hw), axis=1)
                top = jnp.max(jnp.max(tot, axis=0, keepdims=True), axis=1, keepdims=True)
                block_decay[gi] = jnp.maximum(block_decay[gi], top)
            o += hw
        v_ref[0, r, :] = un[:, o:o + hw].astype(BF16)
        hq_ref[0, r, :] = un[:, o + hw:o + 2 * hw]
        g_ref[0, r, :] = un[:, o + 2 * hw:o + 3 * hw]

        ut = _dot_nt(wt_ref[...], h)
        uq = ut[0:qw].reshape(ATT_HEADS, HEAD_DIM, tkv)
        msq = jnp.mean(uq * uq, axis=1, keepdims=True)
        qn = uq * lax.rsqrt(msq + EPS) * gq_ref[...][None]
        cr, sr = cqt_ref[0:16, r][None], sqt_ref[0:16, r][None]
        cc, sc = cqt_ref[16:32, r][None], sqt_ref[16:32, r][None]
        x1r, x2r, x1c, x2c = qn[:, 0:16], qn[:, 16:32], qn[:, 32:48], qn[:, 48:64]
        qr = jnp.concatenate([x1r * cr - x2r * sr, x1r * sr + x2r * cr,
                              x1c * cc - x2c * sc, x1c * sc + x2c * cc], axis=1)
        qt_ref[0, :, :, r] = (qr * (HEAD_DIM ** -0.5 * LOG2E)).astype(BF16)

        uv = ut[qw:qw + kw].reshape(ATT_KV_HEADS, HEAD_DIM, tkv)
        ones_row = lax.broadcasted_iota(jnp.int32, (ATT_KV_HEADS, 16, tkv), 1) == 0
        aug = jnp.where(ones_row, 1.0, 0.0).astype(F32)
        vt_ref[0, :, sb] = jnp.concatenate([uv, aug], axis=1).astype(BF16)

    grow = lax.broadcasted_iota(jnp.int32, (8, 128), 0)
    gd = jnp.zeros((8, 128), F32)
    for gi, dec in enumerate(block_decay):
        gd = jnp.where(grow == gi, dec, gd)
    gd_ref[0, 0] = gd


def _inproj(xs, mod4, l, nb, wn, wt, gk, ckn, skn, gq, cqt, sqt, lbf, lbb, lc, tm, tkv):
    b, t, d = xs.shape
    n6 = mod4.shape[-1]
    fw = FOURIER_GD * 4
    hw = lbf.shape[-1]
    kw = ATT_KV_HEADS * HEAD_DIM
    tok = lambda w, dt: jax.ShapeDtypeStruct((b, t, w), dt)
    tok_spec = lambda w: pl.BlockSpec((1, tm, w), lambda bi, i: (bi, i, 0))
    out_shape = [tok(fw, F32), tok(kw, BF16), tok(hw, F32), tok(hw, F32), tok(hw, F32), tok(hw, F32),
                 tok(hw, BF16), tok(hw, F32), tok(hw, F32),
                 jax.ShapeDtypeStruct((b, ATT_HEADS, HEAD_DIM, t), BF16),
                 jax.ShapeDtypeStruct((b, ATT_KV_HEADS, t // tkv, HEAD_DIM + 16, tkv), BF16),
                 jax.ShapeDtypeStruct((b, t // tm, 8, 128), F32)]
    out_specs = [tok_spec(fw), tok_spec(kw)] + [tok_spec(hw)] * 7 + [
        pl.BlockSpec((1, ATT_HEADS, HEAD_DIM, tm), lambda bi, i: (bi, 0, 0, i)),
        pl.BlockSpec((1, ATT_KV_HEADS, tm // tkv, HEAD_DIM + 16, tkv), lambda bi, i: (bi, 0, i, 0, 0)),
        pl.BlockSpec((1, 1, 8, 128), lambda bi, i: (bi, i, 0, 0))]
    in_specs = [
        pl.BlockSpec((1, tm, d), lambda bi, i: (bi, i, 0)),
        pl.BlockSpec((1, 1, 1, n6), lambda bi, i: (l, bi, 0, 0)),
        pl.BlockSpec((1, 1, 1, n6), lambda bi, i: (l, nb, 0, 0)),
        _resident(wn.shape), _resident(wt.shape), _resident(gk.shape),
        pl.BlockSpec((tm, kw), lambda bi, i: (i, 0)),
        pl.BlockSpec((tm, kw), lambda bi, i: (i, 0)),
        _resident(gq.shape),
        pl.BlockSpec((32, tm), lambda bi, i: (0, i)),
        pl.BlockSpec((32, tm), lambda bi, i: (0, i)),
        _resident(lbf.shape), _resident(lbb.shape),
    ]
    return pl.pallas_call(
        functools.partial(_inproj_kernel, lc=lc, tm=tm, tkv=tkv, d=d, fw=fw, hw=hw),
        grid=(b, t // tm),
        in_specs=in_specs, out_specs=out_specs, out_shape=out_shape,
        name="inproj",
        compiler_params=pltpu.CompilerParams(
            dimension_semantics=("parallel", "parallel"),
            vmem_limit_bytes=V7X_VMEM_LIMIT),
    )(xs, mod4, mod4, wn, wt, gk, ckn, skn, gq, cqt, sqt, lbf, lbb)


def _fourier_kernel(z_ref, f1_ref, twc_ref, tws_ref, c2_ref, s2_ref, cs_ref, cl_ref,
                    y_ref, a1_ref, p_ref, *, lc, n1, scale_c, scale_x):
    n2 = DFT_N2
    cs = cs_ref[...]
    pa, pp = 2 * n1 + DFT_ROW_PAD, n2 + DFT_ROW_PAD

    zc = z_ref[0, 0:lc, :].astype(BF16)
    pc = _dot(cl_ref[...], zc)
    pcat = jnp.concatenate([pc[0:lc], pc[lc:2 * lc]], axis=1).astype(BF16)
    y_ref[0, 0:lc, :] = (_dot(pcat, cs) * scale_c).astype(y_ref.dtype)

    f1 = f1_ref[...]

    def stage1(j, carry):
        xj = z_ref[0, pl.ds(lc + j, n1, stride=n2), :].astype(BF16)
        a1_ref[pl.ds(pl.multiple_of(j * pa, 8), 2 * n1), :] = _dot(f1, xj)
        return carry

    lax.fori_loop(0, n2, stage1, 0, unroll=8)

    c2 = c2_ref[...]
    s2 = s2_ref[...]

    def stage2(k1, carry):
        ar = a1_ref[pl.ds(k1, n2, stride=pa), :]
        ai = a1_ref[pl.ds(n1 + k1, n2, stride=pa), :]
        twc = twc_ref[pl.ds(k1, 1), :]
        tws = tws_ref[pl.ds(k1, 1), :]
        gr = c2 * twc - s2 * tws
        gi = -(s2 * twc + c2 * tws)
        gm = jnp.concatenate([jnp.concatenate([gr, -gi], axis=1),
                              jnp.concatenate([gi, gr], axis=1)], axis=0).astype(BF16)
        rhs = jnp.concatenate([ar, ai], axis=0).astype(BF16)
        pri = _dot(gm, rhs)
        pcat2 = jnp.concatenate([pri[0:n2], pri[n2:2 * n2]], axis=1).astype(BF16)
        p_ref[pl.ds(pl.multiple_of(k1 * pp, 8), n2), :] = _dot(pcat2, cs) * scale_x
        return carry

    lax.fori_loop(0, n1, stage2, 0, unroll=8)

    def stage3(k2, carry):
        blk = p_ref[pl.ds(k2, n1, stride=pp), :]
        y_ref[0, pl.ds(pl.multiple_of(lc + k2 * n1, n1), n1), :] = blk.astype(y_ref.dtype)
        return carry

    lax.fori_loop(0, n2, stage3, 0, unroll=8)


def _fourier_consts(lc, s):
    n2 = DFT_N2
    n1 = s // n2
    f64 = np.float64
    k = np.arange(n1, dtype=f64)
    a1 = 2 * np.pi * np.outer(k, k) / n1
    f1 = np.concatenate([np.cos(a1), -np.sin(a1)], axis=0)
    at = 2 * np.pi * np.outer(np.arange(n1, dtype=f64), np.arange(n2, dtype=f64)) / s
    k2 = np.arange(n2, dtype=f64)
    a2 = 2 * np.pi * np.outer(k2, k2) / n2
    w = 2 * FOURIER_GD
    ch = np.arange(w)
    same = (ch[:, None] // FOURIER_GD) == (ch[None, :] // FOURIER_GD)
    ag = 2 * np.pi * np.outer(ch % FOURIER_GD, ch % FOURIER_GD) / FOURIER_GD
    cs = np.concatenate([np.where(same, np.cos(ag), 0.0), np.where(same, np.sin(ag), 0.0)], axis=0)
    kc = np.arange(lc, dtype=f64)
    al = 2 * np.pi * np.outer(kc, kc) / lc
    cl = np.concatenate([np.cos(al), -np.sin(al)], axis=0)
    return dict(
        f1=jnp.asarray(f1, BF16), twc=jnp.asarray(np.cos(at), F32), tws=jnp.asarray(np.sin(at), F32),
        c2=jnp.asarray(np.cos(a2), F32), s2=jnp.asarray(np.sin(a2), F32),
        cs=jnp.asarray(cs, BF16), cl=jnp.asarray(cl, BF16), n1=n1)


def _fourier(fz, fc, lc):
    b, t, fw = fz.shape
    s = t - lc
    n1 = fc["n1"]
    w = 2 * FOURIER_GD
    consts = [fc[k] for k in ("f1", "twc", "tws", "c2", "s2", "cs", "cl")]
    return pl.pallas_call(
        functools.partial(_fourier_kernel, lc=lc, n1=n1,
                          scale_c=float((lc * FOURIER_GD) ** -0.5),
                          scale_x=float((s * FOURIER_GD) ** -0.5)),
        grid=(b, fw // w),
        in_specs=[pl.BlockSpec((1, t, w), lambda bi, hi: (bi, 0, hi))] +
                 [_resident(c.shape) for c in consts],
        out_specs=pl.BlockSpec((1, t, w), lambda bi, hi: (bi, 0, hi)),
        out_shape=jax.ShapeDtypeStruct((b, t, fw), BF16),
        scratch_shapes=[pltpu.VMEM((DFT_N2 * (2 * n1 + DFT_ROW_PAD), w), F32),
                        pltpu.VMEM((n1 * (DFT_N2 + DFT_ROW_PAD), w), F32)],
        name="fourier",
        compiler_params=pltpu.CompilerParams(
            dimension_semantics=("parallel", "parallel"),
            vmem_limit_bytes=V7X_VMEM_LIMIT),
    )(fz, *consts)


def _place_queries(q_ref, qs_ref):
    g = pl.program_id(1)
    qcat = jnp.concatenate([q_ref[0, j] for j in range(ATT_GROUP)], axis=1)
    zero = jnp.zeros_like(qcat)
    for gg in range(ATT_KV_HEADS):
        @pl.when(g == gg)
        def _place():
            for hh in range(ATT_KV_HEADS):
                qs_ref[hh * HEAD_DIM:(hh + 1) * HEAD_DIM, :] = qcat if hh == gg else zero


def _score_block(k_ref, qs_ref, blk, tk, lc, masked):
    kb = k_ref[0, pl.ds(pl.multiple_of(blk * tk, tk), tk), :]
    s = _dot(kb, qs_ref[...])
    if masked:
        kidx = blk * tk + lax.broadcasted_iota(jnp.int32, (tk, 1), 0)
        s = jnp.where(kidx < lc, s, NEG_BIG)
    return s


def _attn_finish(acc_ref, o_ref, tq):
    acc = acc_ref[...]
    o = acc[0:HEAD_DIM] / acc[HEAD_DIM:HEAD_DIM + 1]
    o_ref[0] = jnp.concatenate([o[:, j * tq:(j + 1) * tq] for j in range(ATT_GROUP)],
                               axis=0).astype(o_ref.dtype)


def _attn_bounded_kernel(q_ref, k_ref, v_ref, o_ref, qs_ref, p0_ref, p1_ref, acc_ref,
                         *, lc, tq, tk, nqc, nkc, nk):
    qi = pl.program_id(2)
    _place_queries(q_ref, qs_ref)
    acc_ref[...] = jnp.zeros(acc_ref.shape, F32)

    p_ref = (p0_ref, p1_ref)
    tkv = v_ref.shape[-1]
    nsub = tk // tkv

    def probs(blk, slot):
        p_ref[slot][...] = jnp.exp2(_score_block(k_ref, qs_ref, blk, tk, lc, False)).astype(BF16)

    def weighted(blk, slot):
        pv = _dot(v_ref[0, 0, blk * nsub], p_ref[slot][0:tkv, :])
        for c in range(1, nsub):
            pv = pv + _dot(v_ref[0, 0, blk * nsub + c], p_ref[slot][c * tkv:(c + 1) * tkv, :])
        acc_ref[...] += pv

    @pl.when(qi < nqc)
    def _ctx():
        nsc = -(-lc // tkv)
        s = _dot(k_ref[0, 0:nsc * tkv, :], qs_ref[...])
        if lc % tkv:
            s = jnp.where(lax.broadcasted_iota(jnp.int32, (nsc * tkv, 1), 0) < lc, s, NEG_BIG)
        p = jnp.exp2(s).astype(BF16)
        pv = _dot(v_ref[0, 0, 0], p[0:tkv])
        for c in range(1, nsc):
            pv = pv + _dot(v_ref[0, 0, c], p[c * tkv:(c + 1) * tkv])
        acc_ref[...] += pv

    @pl.when(qi >= nqc)
    def _lat():
        def steady(t, slot):
            probs(t + 1, 1 - slot)
            weighted(t, slot)

        probs(0, 0)
        npairs = (nk - 1) // 2

        def pair(i, carry):
            steady(2 * i, 0)
            steady(2 * i + 1, 1)
            return carry

        lax.fori_loop(0, npairs, pair, 0)
        if (nk - 1) % 2:
            steady(nk - 2, (nk - 2) % 2)
        weighted(nk - 1, (nk - 1) % 2)

    _attn_finish(acc_ref, o_ref, tq)


def _attn_kernel(q_ref, k_ref, v_ref, o_ref, qs_ref, s0_ref, s1_ref, mb0_ref, mb1_ref, p0_ref, p1_ref,
                 al0_ref, al1_ref, m_ref, acc_ref, *, lc, tq, tk, nqc, nkc, nk):
    s_ref, mb_ref = (s0_ref, s1_ref), (mb0_ref, mb1_ref)
    p_ref, al_ref = (p0_ref, p1_ref), (al0_ref, al1_ref)
    qi = pl.program_id(2)

    _place_queries(q_ref, qs_ref)
    m_ref[...] = jnp.full(m_ref.shape, NEG_BIG, F32)
    acc_ref[...] = jnp.zeros(acc_ref.shape, F32)

    def scores(blk, slot, masked=False):
        s = _score_block(k_ref, qs_ref, blk, tk, lc, masked)
        s_ref[slot][...] = s
        mb_ref[slot][...] = jnp.max(s, axis=0, keepdims=True)

    def probs(slot):
        m_prev = m_ref[...]
        m_new = jnp.maximum(m_prev, mb_ref[slot][...])
        al_ref[slot][...] = jnp.exp2(m_prev - m_new)
        p_ref[slot][...] = jnp.exp2(s_ref[slot][...] - m_new).astype(BF16)
        m_ref[...] = m_new

    def weighted(blk, slot):
        acc_ref[...] = al_ref[slot][...] * acc_ref[...] + _dot(v_ref[0, 0, blk], p_ref[slot][...])

    @pl.when(qi < nqc)
    def _ctx():
        for blk in range(nkc):
            scores(blk, 0, masked=True)
            probs(0)
            weighted(blk, 0)

    @pl.when(qi >= nqc)
    def _lat():
        def steady(t, slot):
            scores(t + 2, slot)
            weighted(t, slot)
            probs(1 - slot)

        scores(0, 0)
        scores(1, 1)
        probs(0)
        npairs = (nk - 2) // 2

        def pair(i, carry):
            steady(2 * i, 0)
            steady(2 * i + 1, 1)
            return carry

        lax.fori_loop(0, npairs, pair, 0)
        if (nk - 2) % 2:
            steady(2 * npairs, 0)
        weighted(nk - 2, (nk - 2) % 2)
        probs((nk - 1) % 2)
        weighted(nk - 1, (nk - 1) % 2)

    _attn_finish(acc_ref, o_ref, tq)


def _attention(qt, kn, vt, lc, tq, bounded):
    b, _, _, t = qt.shape
    nkv, tkv = vt.shape[2], vt.shape[4]
    tk = _pick_tile(t, ATTN_BOUNDED_KEYS, mult=tkv) if bounded else tkv
    nk = t // tk
    nq = t // tq
    nqc = lc // tq
    nkc = -(-lc // tk)
    wide = ATT_GROUP * tq
    qs_scratch = pltpu.VMEM((ATT_KV_HEADS * HEAD_DIM, wide), BF16)
    acc_scratch = pltpu.VMEM((HEAD_DIM + 16, wide), F32)
    if bounded:
        body = _attn_bounded_kernel
        scratch = [qs_scratch, pltpu.VMEM((tk, wide), BF16), pltpu.VMEM((tk, wide), BF16), acc_scratch]
    else:
        assert nk >= 2
        body = _attn_kernel
        scratch = [qs_scratch,
                   pltpu.VMEM((tk, wide), F32), pltpu.VMEM((tk, wide), F32),
                   pltpu.VMEM((1, wide), F32), pltpu.VMEM((1, wide), F32),
                   pltpu.VMEM((tk, wide), BF16), pltpu.VMEM((tk, wide), BF16),
                   pltpu.VMEM((1, wide), F32), pltpu.VMEM((1, wide), F32),
                   pltpu.VMEM((1, wide), F32),
                   acc_scratch]
    return pl.pallas_call(
        functools.partial(body, lc=lc, tq=tq, tk=tk, nqc=nqc, nkc=nkc, nk=nk),
        grid=(b, ATT_KV_HEADS, nq),
        in_specs=[
            pl.BlockSpec((1, ATT_GROUP, HEAD_DIM, tq), lambda bi, g, qi: (bi, g, 0, qi)),
            pl.BlockSpec((1, t, ATT_KV_HEADS * HEAD_DIM), lambda bi, g, qi: (bi, 0, 0)),
            pl.BlockSpec((1, 1, nkv, HEAD_DIM + 16, tkv), lambda bi, g, qi: (bi, g, 0, 0, 0)),
        ],
        out_specs=pl.BlockSpec((1, ATT_GROUP * HEAD_DIM, tq), lambda bi, g, qi: (bi, g, qi)),
        out_shape=jax.ShapeDtypeStruct((b, ATT_HEADS * HEAD_DIM, t), BF16),
        scratch_shapes=scratch,
        name="attn_bounded" if bounded else "attn_online",
        compiler_params=pltpu.CompilerParams(
            dimension_semantics=("parallel", "parallel", "arbitrary"),
            vmem_limit_bytes=V7X_VMEM_LIMIT),
    )(qt, kn, vt)


def _hgrn_consts():
    c = HG_CHUNK
    t = np.arange(c)
    tt, rr = t[:, None], t[None, :]
    mats, masks = [], []
    for rev in (False, True):
        mats.append(np.concatenate([(rr >= tt) if rev else (rr <= tt), np.ones((8, c), bool)], axis=0))
        pair_masks = []
        for bsz in HG_LEVELS:
            blk, pos = t // (2 * bsz), t % (2 * bsz)
            qv = (pos < bsz) if rev else (pos >= bsz)
            pair_masks.append((blk[:, None] == blk[None, :]) & qv[:, None] & ~qv[None, :])
        for wsz in HG_WITHIN:
            same = (tt // wsz) == (rr // wsz)
            pair_masks.append(same & ((rr >= tt) if rev else (rr <= tt)))
        masks.append(np.stack([np.tile(m, (HG_HEADS, 1)) for m in pair_masks]))
    return jnp.asarray(np.stack(mats), BF16), jnp.asarray(np.stack(masks), F32)


def _level_reference(cum, bsz, rev):
    c, hw = cum.shape
    i = bsz if rev else bsz - 1
    if 2 * bsz >= 8:
        c3 = cum.reshape(c // (2 * bsz), 2 * bsz, hw)
        return jnp.broadcast_to(c3[:, i:i + 1, :], c3.shape).reshape(c, hw)
    c3 = cum.reshape(c // 8, 8, hw)
    sub = lax.broadcasted_iota(jnp.int32, (1, 8, 1), 1) // (2 * bsz)
    ref = jnp.broadcast_to(c3[:, i:i + 1, :], c3.shape)
    for gidx in range(1, 8 // (2 * bsz)):
        r = gidx * 2 * bsz + i
        ref = jnp.where(sub == gidx, jnp.broadcast_to(c3[:, r:r + 1, :], c3.shape), ref)
    return ref.reshape(c, hw)


def _block_entry_reference(cum, wsz, rev):
    c, hw = cum.shape
    nb = c // wsz
    if nb == 1:
        return jnp.zeros_like(cum)
    c3 = cum.reshape(nb, wsz, hw)
    zero = jnp.zeros((1, 1, hw), F32)
    if rev:
        edge = jnp.concatenate([c3[1:, 0:1, :], zero], axis=0)
    else:
        edge = jnp.concatenate([zero, c3[:-1, wsz - 1:wsz, :]], axis=0)
    return jnp.broadcast_to(edge, c3.shape).reshape(c, hw)


def _hgrn_chunk(d, r0, kk_ref, lf_ref, v_ref, q_ref, cm_ref, am_ref, hw, within):
    c = HG_CHUNK
    dk = hw // HG_HEADS
    rev = d == 1
    rows = pl.ds(r0, c)
    lf = lf_ref[0, rows, :]
    cm = cm_ref[d]
    hi, mid, lo = _split3(lf)
    e = _dot(cm, hi) + _dot(cm, mid) + _dot(cm, lo)
    cum, tot = e[0:c], e[c:c + 1]
    q = q_ref[0, rows, :]
    k = kk_ref[0, rows, :]
    v = v_ref[0, rows, :]
    qdec = q * jnp.exp(cum)
    kdec = k * jnp.exp(tot - cum)
    ds = jnp.exp(tot)

    lane_head = lax.broadcasted_iota(jnp.int32, (1, hw), 1) // dk
    heads = [lane_head == hh for hh in range(HG_HEADS)]
    tpos = lax.broadcasted_iota(jnp.int32, (c, 1), 0)

    def pair_product(qd, kd, mask_idx):
        q4 = jnp.concatenate([jnp.where(hm, qd, 0.0) for hm in heads], axis=0).astype(BF16)
        return am_ref[d, mask_idx] * _dot_nt(q4, kd.astype(BF16))

    att = jnp.zeros((HG_HEADS * c, c), F32)
    for lv, bsz in enumerate(HG_LEVELS):
        if bsz < within:
            continue
        pos = tpos % (2 * bsz)
        is_q = (pos < bsz) if rev else (pos >= bsz)
        diff = cum - _level_reference(cum, bsz, rev)
        w = jnp.exp(jnp.where(is_q, diff, -diff))
        att = att + pair_product(jnp.where(is_q, q * w, 0.0), jnp.where(is_q, 0.0, k * w), lv)
    widx = len(HG_LEVELS) + HG_WITHIN.index(within)
    if within == 1:
        att = att + pair_product(q, k, widx)
    else:
        diff = cum - _block_entry_reference(cum, within, rev)
        att = att + pair_product(q * jnp.exp(diff), k * jnp.exp(-diff), widx)

    ov = _dot(att.astype(BF16), v)
    o = jnp.zeros((c, hw), F32)
    for hh, hm in enumerate(heads):
        o = o + jnp.where(hm, ov[hh * c:(hh + 1) * c], 0.0)

    u = _dot_tn(v, kdec.astype(BF16))
    rh = lax.broadcasted_iota(jnp.int32, (hw, 1), 0) // dk
    return o, qdec.astype(BF16), jnp.where(rh == lane_head, u, 0.0), ds


def _hgrn_kernel(kf_ref, lff_ref, vf_ref, qf_ref, kb_ref, lfb_ref, vb_ref, qb_ref,
                 cm_ref, am_ref, of_ref, ob_ref, sf_ref, sb_ref, *, nchunk, hw, within):
    @pl.when(pl.program_id(1) == 0)
    def _init():
        sf_ref[...] = jnp.zeros(sf_ref.shape, F32)
        sb_ref[...] = jnp.zeros(sb_ref.shape, F32)

    dirs = ((0, kf_ref, lff_ref, vf_ref, qf_ref, of_ref, sf_ref),
            (1, kb_ref, lfb_ref, vb_ref, qb_ref, ob_ref, sb_ref))
    for jj in range(nchunk):
        for d, kk_ref, lf_ref, v_ref, q_ref, o_ref, s_ref in dirs:
            r0 = (nchunk - 1 - jj if d else jj) * HG_CHUNK
            o, qdec, u, ds = _hgrn_chunk(d, r0, kk_ref, lf_ref, v_ref, q_ref, cm_ref, am_ref, hw, within)
            st = s_ref[...]
            o_ref[0, pl.ds(r0, HG_CHUNK), :] = o + _dot_nt(qdec, st.astype(BF16))
            s_ref[...] = st * ds + u


def _hgrn(kf, lff, kb, lfb, v, hq, hc, lc, nchunk, within):
    b, t, hw = kf.shape
    tb = nchunk * HG_CHUNK
    nblk, ncb = t // tb, lc // tb
    cm, am = hc
    fwd = lambda bi, p: (bi, p, 0)
    bwd = lambda bi, p: (bi, jnp.where(p < ncb, ncb - 1 - p, nblk - 1 - (p - ncb)), 0)
    blk = lambda im: pl.BlockSpec((1, tb, hw), im)
    return pl.pallas_call(
        functools.partial(_hgrn_kernel, nchunk=nchunk, hw=hw, within=within),
        grid=(b, nblk),
        in_specs=[blk(fwd)] * 4 + [blk(bwd)] * 4 + [_resident(cm.shape), _resident(am.shape)],
        out_specs=[blk(fwd), blk(bwd)],
        out_shape=[jax.ShapeDtypeStruct((b, t, hw), F32)] * 2,
        scratch_shapes=[pltpu.VMEM((hw, hw), F32), pltpu.VMEM((hw, hw), F32)],
        name=f"hgrn_within{within}",
        compiler_params=pltpu.CompilerParams(
            dimension_semantics=("parallel", "arbitrary"),
            vmem_limit_bytes=V7X_VMEM_LIMIT),
    )(kf, lff, v, hq, kb, lfb, v, hq, cm, am)


def _outproj_kernel(x_ref, mb_ref, mc_ref, y_ref, at_ref, of_ref, ob_ref, g_ref, wf_ref, wo_ref,
                    gn_ref, o_ref, *, lc, tm, d, fw, aw, hw):
    i = pl.program_id(1)
    row = i * tm + lax.broadcasted_iota(jnp.int32, (tm, 1), 0)
    is_ctx = row < lc
    mb, mc = mb_ref[0, 0], mc_ref[0, 0]
    gate = jnp.where(is_ctx, mc[:, 2 * d:3 * d], mb[:, 2 * d:3 * d])

    fx = _dot(y_ref[0], wf_ref[...]).astype(BF16)
    mix = _dot(fx, wo_ref[0:fw, :])
    mix = mix + _dot_tn(at_ref[0], wo_ref[fw:fw + aw, :])

    o = of_ref[0] + ob_ref[0]
    dv = hw // HG_HEADS
    lane_head = lax.broadcasted_iota(jnp.int32, (1, hw), 1) // dv
    o2 = o * o
    ms = jnp.zeros_like(o)
    for hh in range(HG_HEADS):
        sel = lane_head == hh
        ssh = jnp.sum(jnp.where(sel, o2, 0.0), axis=-1, keepdims=True)
        ms = jnp.where(sel, ssh * (1.0 / dv), ms)
    gg = g_ref[0]
    rx = (o * lax.rsqrt(ms + EPS) * gn_ref[...]) * (gg * jax.nn.sigmoid(gg))
    mix = mix + _dot(rx.astype(BF16), wo_ref[fw + aw:fw + aw + hw, :])
    o_ref[0] = x_ref[0] + gate * mix


def _outproj(xs, mod4, l, nb, y, at, of, ob, g, wf, wo, gn, lc, tm):
    b, t, d = xs.shape
    n6 = mod4.shape[-1]
    fw, hw = y.shape[-1], of.shape[-1]
    aw = at.shape[1]
    tok = lambda w: pl.BlockSpec((1, tm, w), lambda bi, i: (bi, i, 0))
    return pl.pallas_call(
        functools.partial(_outproj_kernel, lc=lc, tm=tm, d=d, fw=fw, aw=aw, hw=hw),
        grid=(b, t // tm),
        in_specs=[tok(d),
                  pl.BlockSpec((1, 1, 1, n6), lambda bi, i: (l, bi, 0, 0)),
                  pl.BlockSpec((1, 1, 1, n6), lambda bi, i: (l, nb, 0, 0)),
                  tok(fw),
                  pl.BlockSpec((1, aw, tm), lambda bi, i: (bi, 0, i)),
                  tok(hw), tok(hw), tok(hw),
                  _resident(wf.shape), _resident(wo.shape), _resident(gn.shape)],
        out_specs=tok(d),
        out_shape=jax.ShapeDtypeStruct((b, t, d), F32),
        name="outproj",
        compiler_params=pltpu.CompilerParams(
            dimension_semantics=("parallel", "parallel"),
            vmem_limit_bytes=V7X_VMEM_LIMIT),
    )(xs, mod4, mod4, y, at, of, ob, g, wf, wo, gn)


def _ffn_kernel(x_ref, mb_ref, mc_ref, wg_ref, wu_ref, wd_ref, *rest, lc, tm, d, fchunk):
    o_ref = rest[-1]
    i = pl.program_id(1)
    x = x_ref[0]
    row = i * tm + lax.broadcasted_iota(jnp.int32, (tm, 1), 0)
    is_ctx = row < lc
    mb, mc = mb_ref[0, 0], mc_ref[0, 0]
    h = _modulated(x, mb, mc, is_ctx, d, 3, 4).astype(BF16)
    gate = jnp.where(is_ctx, mc[:, 5 * d:6 * d], mb[:, 5 * d:6 * d])
    dff = wg_ref.shape[1]
    acc = jnp.zeros((tm, d), F32)
    for c0 in range(0, dff, fchunk):
        a = _dot(h, wg_ref[:, c0:c0 + fchunk])
        u = _dot(h, wu_ref[:, c0:c0 + fchunk])
        act = (a * jax.nn.sigmoid(a) * u).astype(BF16)
        acc = acc + _dot(act, wd_ref[c0:c0 + fchunk, :])
    y = x + gate * acc
    if len(rest) == 2:
        y = y * lax.rsqrt(jnp.mean(y * y, axis=-1, keepdims=True) + EPS) * rest[0][...]
    o_ref[0] = y


def _ffn(xs, mod4, l, nb, wg, wu, wd, lc, tm, final_gain=None):
    b, t, d = xs.shape
    n6 = mod4.shape[-1]
    dff = wg.shape[1]
    fchunk = _pick_tile(dff, 768)
    extra_in = [] if final_gain is None else [final_gain.reshape(1, d)]
    return pl.pallas_call(
        functools.partial(_ffn_kernel, lc=lc, tm=tm, d=d, fchunk=fchunk),
        grid=(b, t // tm),
        in_specs=[pl.BlockSpec((1, tm, d), lambda bi, i: (bi, i, 0)),
                  pl.BlockSpec((1, 1, 1, n6), lambda bi, i: (l, bi, 0, 0)),
                  pl.BlockSpec((1, 1, 1, n6), lambda bi, i: (l, nb, 0, 0)),
                  _resident(wg.shape), _resident(wu.shape), _resident(wd.shape)] +
                 [_resident(a.shape) for a in extra_in],
        out_specs=pl.BlockSpec((1, tm, d), lambda bi, i: (bi, i, 0)),
        out_shape=jax.ShapeDtypeStruct((b, t, d), F32),
        name="ffn" if final_gain is None else "ffn_final",
        compiler_params=pltpu.CompilerParams(
            dimension_semantics=("parallel", "parallel"),
            vmem_limit_bytes=V7X_VMEM_LIMIT),
    )(xs, mod4, mod4, wg, wu, wd, *extra_in)


def _rope_tables(lc, s):
    half = HEAD_DIM // 2
    n = jnp.arange(s)
    freqs = ROPE_THETA ** (-jnp.arange(0, half, 2, dtype=F32) / half)
    ang_r = (n // GRID_W).astype(F32)[:, None] * freqs
    ang_c = (n % GRID_W).astype(F32)[:, None] * freqs
    pad = lambda a, v: jnp.concatenate([jnp.full((lc, a.shape[1]), v, F32), a], axis=0)
    cr, sr = pad(jnp.cos(ang_r), 1.0), pad(jnp.sin(ang_r), 0.0)
    cc, sc = pad(jnp.cos(ang_c), 1.0), pad(jnp.sin(ang_c), 0.0)
    cos_h = jnp.concatenate([cr, cr, cc, cc], axis=1)
    sin_h = jnp.concatenate([-sr, sr, -sc, sc], axis=1)
    ckn = jnp.tile(cos_h, (1, ATT_KV_HEADS))
    skn = jnp.tile(sin_h, (1, ATT_KV_HEADS))
    cqt = jnp.concatenate([cr, cc], axis=1).T
    sqt = jnp.concatenate([sr, sc], axis=1).T
    return ckn, skn, cqt, sqt


def kernel(x, c, ctx, c_ctx, w_ada, b_ada, w_in, w_four, q_norm, k_norm, hg_lb_logits, hg_norm,
           w_out, w_gate, w_up, w_down, final_norm):
    b, s, d = x.shape
    lc = ctx.shape[1]
    t = lc + s
    depth = w_ada.shape[0]
    fw = w_four.shape[-1]
    hw = hg_lb_logits.shape[-1]
    qw = ATT_HEADS * HEAD_DIM
    kw = ATT_KV_HEADS * HEAD_DIM
    assert w_in.shape[-1] == fw + qw + 2 * kw + 5 * hw
    assert s % (DFT_N2 * 8) == 0 and lc % 128 == 0 and s % GRID_W == 0

    tq = 256 if (lc % 256 == 0 and t % 256 == 0) else 128
    tkv = tq
    tm = _pick_tile(t, 768, mult=tkv)
    nchunk = max(g for g in (4, 2, 1) if lc % (g * HG_CHUNK) == 0 and t % (g * HG_CHUNK) == 0)

    xs = jnp.concatenate([ctx, x], axis=1)
    r8 = -(-(b + 1) // 8) * 8
    cvec = jnp.concatenate([c, c_ctx[None], jnp.zeros((r8 - b - 1, d), F32)], axis=0)
    mod = _adaln(cvec, w_ada, b_ada)
    mod4 = mod.reshape(depth, r8, 1, mod.shape[-1])

    lb_sm = jax.nn.softmax(hg_lb_logits.astype(F32), axis=1)
    lb_all = jnp.cumsum(lb_sm, axis=1) - lb_sm[:, :1]

    ckn, skn, cqt, sqt = _rope_tables(lc, s)
    fc = _fourier_consts(lc, s)
    hc = _hgrn_consts()

    o_q, o_k, o_v, o_h = fw, fw + qw, fw + qw + kw, fw + qw + 2 * kw
    for l in range(depth):
        wl = w_in[l]
        wn = jnp.concatenate([wl[:, 0:fw], wl[:, o_k:o_v], wl[:, o_h:]], axis=1).astype(BF16)
        wt = jnp.concatenate([wl[:, o_q:o_k], wl[:, o_v:o_h]], axis=1).T.astype(BF16)
        gk = jnp.tile(k_norm[l], ATT_KV_HEADS)[None]
        gq = q_norm[l][:, None]
        gn = jnp.tile(hg_norm[l], HG_HEADS)[None]
        (fz, kn, kf, lff, kb, lfb, v, hq, g, qt, vt, gd) = _inproj(
            xs, mod4, l, b, wn, wt, gk, ckn, skn, gq, cqt, sqt,
            lb_all[0, l][None], lb_all[1, l][None], lc, tm, tkv)
        y = _fourier(fz, fc, lc)
        bound = (HEAD_DIM ** 0.5 * LOG2E * BOUND_SLACK) * jnp.max(jnp.abs(q_norm[l])) * jnp.max(jnp.abs(k_norm[l]))
        at = lax.cond(bound < MAX_SAFE_LOG2_SCORE,
                      functools.partial(_attention, lc=lc, tq=tq, bounded=True),
                      functools.partial(_attention, lc=lc, tq=tq, bounded=False),
                      qt, kn, vt)
        hg_scan = functools.partial(_hgrn, hc=hc, lc=lc, nchunk=nchunk, within=HG_WITHIN[0])
        for gi, wsz in enumerate(HG_WITHIN[1:]):
            hg_scan = functools.partial(
                lambda ok, big, small, *a: lax.cond(ok, big, small, *a),
                jnp.max(gd[:, :, gi, 0]) < HG_MAX_BLOCK_DECAY,
                functools.partial(_hgrn, hc=hc, lc=lc, nchunk=nchunk, within=wsz), hg_scan)
        of, ob = hg_scan(kf, lff, kb, lfb, v, hq)
        xs = _outproj(xs, mod4, l, b, y, at, of, ob, g, w_four[l].astype(BF16),
                      w_out[l].astype(BF16), gn, lc, tm)
        xs = _ffn(xs, mod4, l, b, w_gate[l].astype(BF16), w_up[l].astype(BF16), w_down[l].astype(BF16),
                  lc, tm, final_norm if l == depth - 1 else None)
    return xs[:, lc:]
```

```python
import functools

import numpy as np
import jax
import jax.numpy as jnp
from jax import lax
from jax.experimental import pallas as pl
from jax.experimental.pallas import tpu as pltpu

F32 = jnp.float32
BF16 = jnp.bfloat16

EPS = 1e-6
GRID_W = 64
ROPE_THETA = 10000.0
FOURIER_GD = 64
ATT_HEADS = 8
ATT_KV_HEADS = 2
ATT_GROUP = ATT_HEADS // ATT_KV_HEADS
HEAD_DIM = 64
HG_HEADS = 4
HG_CHUNK = 64
HG_LEVELS = (32, 16, 8, 4, 2, 1)
HG_WITHIN = (1, 16, 64)
HG_MAX_BLOCK_DECAY = 80.0
DFT_N2 = 128
DFT_ROW_PAD = 8
NEG_BIG = -1e30
LOG2E = 1.4426950408889634
BOUND_SLACK = 1.05
MAX_SAFE_LOG2_SCORE = 100.0
ATTN_BOUNDED_KEYS = 2816
V7X_VMEM_LIMIT = 56 * 1024 * 1024


def _dot(a, b):
    return jnp.dot(a, b, preferred_element_type=F32)


def _dot_nt(a, b):
    return lax.dot_general(a, b, (((1,), (1,)), ((), ())), preferred_element_type=F32)


def _dot_tn(a, b):
    return lax.dot_general(a, b, (((0,), (0,)), ((), ())), preferred_element_type=F32)


def _split3(x):
    hi = x.astype(BF16)
    r1 = x - hi.astype(F32)
    mid = r1.astype(BF16)
    lo = (r1 - mid.astype(F32)).astype(BF16)
    return hi, mid, lo


def _pick_tile(n, cap, mult=128):
    best = None
    for t in range(mult, min(n, cap) + 1, mult):
        if n % t == 0:
            best = t
    if best is None:
        raise ValueError(f"no tile for {n}")
    return best


def _resident(shape):
    nd = len(shape)
    return pl.BlockSpec(shape, lambda *_: (0,) * nd, pipeline_mode=pl.Buffered(1))


def _adaln_kernel(c_ref, w_ref, b_ref, o_ref):
    cv = c_ref[...]
    a = cv * jax.nn.sigmoid(cv)
    w = w_ref[0]
    a_hi = a.astype(BF16)
    a_lo = (a - a_hi.astype(F32)).astype(BF16)
    w_hi = w.astype(BF16)
    w_lo = (w - w_hi.astype(F32)).astype(BF16)
    acc = _dot(a_hi, w_hi) + _dot(a_lo, w_hi) + _dot(a_hi, w_lo)
    o_ref[0] = acc + b_ref[0]


def _adaln(cvec, w_ada, b_ada):
    depth, d, n6 = w_ada.shape
    r8 = cvec.shape[0]
    tn = _pick_tile(n6, 1536)
    return pl.pallas_call(
        _adaln_kernel,
        grid=(depth, n6 // tn),
        in_specs=[
            pl.BlockSpec((r8, d), lambda l, j: (0, 0)),
            pl.BlockSpec((1, d, tn), lambda l, j: (l, 0, j)),
            pl.BlockSpec((1, 1, tn), lambda l, j: (l, 0, j)),
        ],
        out_specs=pl.BlockSpec((1, r8, tn), lambda l, j: (l, 0, j)),
        out_shape=jax.ShapeDtypeStruct((depth, r8, n6), F32),
        name="adaln",
        compiler_params=pltpu.CompilerParams(
            dimension_semantics=("parallel", "parallel"),
            vmem_limit_bytes=V7X_VMEM_LIMIT),
    )(cvec, w_ada, b_ada.reshape(depth, 1, n6))


def _modulated(x, mb, mc, is_ctx, d, k_shift, k_scale):
    r = lax.rsqrt(jnp.mean(x * x, axis=-1, keepdims=True) + EPS)
    sh = jnp.where(is_ctx, mc[:, k_shift * d:(k_shift + 1) * d], mb[:, k_shift * d:(k_shift + 1) * d])
    sc = jnp.where(is_ctx, mc[:, k_scale * d:(k_scale + 1) * d], mb[:, k_scale * d:(k_scale + 1) * d])
    return (x * r) * (1.0 + sc) + sh


def _inproj_kernel(x_ref, mb_ref, mc_ref, wn_ref, wt_ref, gk_ref, ckn_ref, skn_ref,
                   gq_ref, cqt_ref, sqt_ref, lbf_ref, lbb_ref,
                   fz_ref, k_ref, kf_ref, lff_ref, kb_ref, lfb_ref, v_ref, hq_ref, g_ref,
                   qt_ref, vt_ref, gd_ref, *, lc, tm, tkv, d, fw, hw):
    i = pl.program_id(1)
    x = x_ref[0]
    row = i * tm + lax.broadcasted_iota(jnp.int32, (tm, 1), 0)
    is_ctx = row < lc
    h = _modulated(x, mb_ref[0, 0], mc_ref[0, 0], is_ctx, d, 0, 1).astype(BF16)

    un = _dot(h, wn_ref[...])
    fz_ref[0] = un[:, 0:fw]

    o = fw
    kw = ATT_KV_HEADS * HEAD_DIM
    uk = un[:, o:o + kw]
    lane = lax.broadcasted_iota(jnp.int32, (1, kw), 1)
    uk2 = uk * uk
    ms = jnp.zeros_like(uk)
    for hh in range(ATT_KV_HEADS):
        sel = (lane // HEAD_DIM) == hh
        ssh = jnp.sum(jnp.where(sel, uk2, 0.0), axis=-1, keepdims=True)
        ms = jnp.where(sel, ssh * (1.0 / HEAD_DIM), ms)
    kn = uk * lax.rsqrt(ms + EPS) * gk_ref[...]
    first = (lane % 32) < 16
    partner = jnp.where(first, pltpu.roll(kn, kw - 16, axis=1), pltpu.roll(kn, 16, axis=1))
    k_ref[0] = (kn * ckn_ref[...] + partner * skn_ref[...]).astype(BF16)
    o += kw

    block_decay = [jnp.zeros((1, 1), F32) for _ in HG_WITHIN[1:]]
    for lb_ref, kk_ref, lf_ref in ((lbf_ref, kf_ref, lff_ref), (lbb_ref, kb_ref, lfb_ref)):
        z = un[:, o:o + hw]
        lb = lb_ref[...]
        kk_ref[0] = (1.0 - lb) * jax.nn.sigmoid(-z)
        lf = jnp.log(lb + (1.0 - lb) * jax.nn.sigmoid(z))
        lf_ref[0] = lf
        for gi, wsz in enumerate(HG_WITHIN[1:]):
            tot = -jnp.sum(lf.reshape(tm // wsz, wsz, hw), axis=1)
            top = jnp.max(jnp.max(tot, axis=0, keepdims=True), axis=1, keepdims=True)
            block_decay[gi] = jnp.maximum(block_decay[gi], top)
        o += hw
    grow = lax.broadcasted_iota(jnp.int32, (8, 128), 0)
    gd = jnp.zeros((8, 128), F32)
    for gi, dec in enumerate(block_decay):
        gd = jnp.where(grow == gi, dec, gd)
    gd_ref[0, 0] = gd
    v_ref[0] = un[:, o:o + hw].astype(BF16)
    hq_ref[0] = un[:, o + hw:o + 2 * hw]
    g_ref[0] = un[:, o + 2 * hw:o + 3 * hw]

    ut = _dot_nt(wt_ref[...], h)
    qw = ATT_HEADS * HEAD_DIM
    uq = ut[0:qw].reshape(ATT_HEADS, HEAD_DIM, tm)
    msq = jnp.mean(uq * uq, axis=1, keepdims=True)
    qn = uq * lax.rsqrt(msq + EPS) * gq_ref[...][None]
    cr, sr = cqt_ref[0:16][None], sqt_ref[0:16][None]
    cc, sc = cqt_ref[16:32][None], sqt_ref[16:32][None]
    x1r, x2r, x1c, x2c = qn[:, 0:16], qn[:, 16:32], qn[:, 32:48], qn[:, 48:64]
    qr = jnp.concatenate([x1r * cr - x2r * sr, x1r * sr + x2r * cr,
                          x1c * cc - x2c * sc, x1c * sc + x2c * cc], axis=1)
    qt_ref[0] = (qr * (HEAD_DIM ** -0.5 * LOG2E)).astype(BF16)

    uv = ut[qw:qw + kw].reshape(ATT_KV_HEADS, HEAD_DIM, tm)
    ones_row = lax.broadcasted_iota(jnp.int32, (ATT_KV_HEADS, 16, tm), 1) == 0
    aug = jnp.where(ones_row, 1.0, 0.0).astype(F32)
    vaug = jnp.concatenate([uv, aug], axis=1).astype(BF16)
    for cb in range(tm // tkv):
        vt_ref[0, :, cb] = vaug[:, :, cb * tkv:(cb + 1) * tkv]


def _inproj(xs, mod4, l, nb, wn, wt, gk, ckn, skn, gq, cqt, sqt, lbf, lbb, lc, tm, tkv):
    b, t, d = xs.shape
    n6 = mod4.shape[-1]
    fw = FOURIER_GD * 4
    hw = lbf.shape[-1]
    kw = ATT_KV_HEADS * HEAD_DIM
    tok = lambda w, dt: jax.ShapeDtypeStruct((b, t, w), dt)
    tok_spec = lambda w: pl.BlockSpec((1, tm, w), lambda bi, i: (bi, i, 0))
    out_shape = [tok(fw, F32), tok(kw, BF16), tok(hw, F32), tok(hw, F32), tok(hw, F32), tok(hw, F32),
                 tok(hw, BF16), tok(hw, F32), tok(hw, F32),
                 jax.ShapeDtypeStruct((b, ATT_HEADS, HEAD_DIM, t), BF16),
                 jax.ShapeDtypeStruct((b, ATT_KV_HEADS, t // tkv, HEAD_DIM + 16, tkv), BF16),
                 jax.ShapeDtypeStruct((b, t // tm, 8, 128), F32)]
    out_specs = [tok_spec(fw), tok_spec(kw)] + [tok_spec(hw)] * 7 + [
        pl.BlockSpec((1, ATT_HEADS, HEAD_DIM, tm), lambda bi, i: (bi, 0, 0, i)),
        pl.BlockSpec((1, ATT_KV_HEADS, tm // tkv, HEAD_DIM + 16, tkv), lambda bi, i: (bi, 0, i, 0, 0)),
        pl.BlockSpec((1, 1, 8, 128), lambda bi, i: (bi, i, 0, 0))]
    in_specs = [
        pl.BlockSpec((1, tm, d), lambda bi, i: (bi, i, 0)),
        pl.BlockSpec((1, 1, 1, n6), lambda bi, i: (l, bi, 0, 0)),
        pl.BlockSpec((1, 1, 1, n6), lambda bi, i: (l, nb, 0, 0)),
        _resident(wn.shape), _resident(wt.shape), _resident(gk.shape),
        pl.BlockSpec((tm, kw), lambda bi, i: (i, 0)),
        pl.BlockSpec((tm, kw), lambda bi, i: (i, 0)),
        _resident(gq.shape),
        pl.BlockSpec((32, tm), lambda bi, i: (0, i)),
        pl.BlockSpec((32, tm), lambda bi, i: (0, i)),
        _resident(lbf.shape), _resident(lbb.shape),
    ]
    return pl.pallas_call(
        functools.partial(_inproj_kernel, lc=lc, tm=tm, tkv=tkv, d=d, fw=fw, hw=hw),
        grid=(b, t // tm),
        in_specs=in_specs, out_specs=out_specs, out_shape=out_shape,
        name="inproj",
        compiler_params=pltpu.CompilerParams(
            dimension_semantics=("parallel", "parallel"),
            vmem_limit_bytes=V7X_VMEM_LIMIT),
    )(xs, mod4, mod4, wn, wt, gk, ckn, skn, gq, cqt, sqt, lbf, lbb)


def _fourier_kernel(z_ref, f1_ref, twc_ref, tws_ref, c2_ref, s2_ref, cs_ref, cl_ref,
                    y_ref, a1_ref, p_ref, *, lc, n1, scale_c, scale_x):
    n2 = DFT_N2
    cs = cs_ref[...]
    pa, pp = 2 * n1 + DFT_ROW_PAD, n2 + DFT_ROW_PAD

    zc = z_ref[0, 0:lc, :].astype(BF16)
    pc = _dot(cl_ref[...], zc)
    pcat = jnp.concatenate([pc[0:lc], pc[lc:2 * lc]], axis=1).astype(BF16)
    y_ref[0, 0:lc, :] = (_dot(pcat, cs) * scale_c).astype(y_ref.dtype)

    f1 = f1_ref[...]

    def stage1(j, carry):
        xj = z_ref[0, pl.ds(lc + j, n1, stride=n2), :].astype(BF16)
        a1_ref[pl.ds(pl.multiple_of(j * pa, 8), 2 * n1), :] = _dot(f1, xj)
        return carry

    lax.fori_loop(0, n2, stage1, 0, unroll=8)

    c2 = c2_ref[...]
    s2 = s2_ref[...]

    def stage2(k1, carry):
        ar = a1_ref[pl.ds(k1, n2, stride=pa), :]
        ai = a1_ref[pl.ds(n1 + k1, n2, stride=pa), :]
        twc = twc_ref[pl.ds(k1, 1), :]
        tws = tws_ref[pl.ds(k1, 1), :]
        gr = c2 * twc - s2 * tws
        gi = -(s2 * twc + c2 * tws)
        gm = jnp.concatenate([jnp.concatenate([gr, -gi], axis=1),
                              jnp.concatenate([gi, gr], axis=1)], axis=0).astype(BF16)
        rhs = jnp.concatenate([ar, ai], axis=0).astype(BF16)
        pri = _dot(gm, rhs)
        pcat2 = jnp.concatenate([pri[0:n2], pri[n2:2 * n2]], axis=1).astype(BF16)
        p_ref[pl.ds(pl.multiple_of(k1 * pp, 8), n2), :] = _dot(pcat2, cs) * scale_x
        return carry

    lax.fori_loop(0, n1, stage2, 0, unroll=8)

    def stage3(k2, carry):
        blk = p_ref[pl.ds(k2, n1, stride=pp), :]
        y_ref[0, pl.ds(pl.multiple_of(lc + k2 * n1, n1), n1), :] = blk.astype(y_ref.dtype)
        return carry

    lax.fori_loop(0, n2, stage3, 0, unroll=8)


def _fourier_consts(lc, s):
    n2 = DFT_N2
    n1 = s // n2
    f64 = np.float64
    k = np.arange(n1, dtype=f64)
    a1 = 2 * np.pi * np.outer(k, k) / n1
    f1 = np.concatenate([np.cos(a1), -np.sin(a1)], axis=0)
    at = 2 * np.pi * np.outer(np.arange(n1, dtype=f64), np.arange(n2, dtype=f64)) / s
    k2 = np.arange(n2, dtype=f64)
    a2 = 2 * np.pi * np.outer(k2, k2) / n2
    w = 2 * FOURIER_GD
    ch = np.arange(w)
    same = (ch[:, None] // FOURIER_GD) == (ch[None, :] // FOURIER_GD)
    ag = 2 * np.pi * np.outer(ch % FOURIER_GD, ch % FOURIER_GD) / FOURIER_GD
    cs = np.concatenate([np.where(same, np.cos(ag), 0.0), np.where(same, np.sin(ag), 0.0)], axis=0)
    kc = np.arange(lc, dtype=f64)
    al = 2 * np.pi * np.outer(kc, kc) / lc
    cl = np.concatenate([np.cos(al), -np.sin(al)], axis=0)
    return dict(
        f1=jnp.asarray(f1, BF16), twc=jnp.asarray(np.cos(at), F32), tws=jnp.asarray(np.sin(at), F32),
        c2=jnp.asarray(np.cos(a2), F32), s2=jnp.asarray(np.sin(a2), F32),
        cs=jnp.asarray(cs, BF16), cl=jnp.asarray(cl, BF16), n1=n1)


def _fourier(fz, fc, lc):
    b, t, fw = fz.shape
    s = t - lc
    n1 = fc["n1"]
    w = 2 * FOURIER_GD
    consts = [fc[k] for k in ("f1", "twc", "tws", "c2", "s2", "cs", "cl")]
    return pl.pallas_call(
        functools.partial(_fourier_kernel, lc=lc, n1=n1,
                          scale_c=float((lc * FOURIER_GD) ** -0.5),
                          scale_x=float((s * FOURIER_GD) ** -0.5)),
        grid=(b, fw // w),
        in_specs=[pl.BlockSpec((1, t, w), lambda bi, hi: (bi, 0, hi))] +
                 [_resident(c.shape) for c in consts],
        out_specs=pl.BlockSpec((1, t, w), lambda bi, hi: (bi, 0, hi)),
        out_shape=jax.ShapeDtypeStruct((b, t, fw), BF16),
        scratch_shapes=[pltpu.VMEM((DFT_N2 * (2 * n1 + DFT_ROW_PAD), w), F32),
                        pltpu.VMEM((n1 * (DFT_N2 + DFT_ROW_PAD), w), F32)],
        name="fourier",
        compiler_params=pltpu.CompilerParams(
            dimension_semantics=("parallel", "parallel"),
            vmem_limit_bytes=V7X_VMEM_LIMIT),
    )(fz, *consts)


def _place_queries(q_ref, qs_ref):
    g = pl.program_id(1)
    qcat = jnp.concatenate([q_ref[0, j] for j in range(ATT_GROUP)], axis=1)
    zero = jnp.zeros_like(qcat)
    for gg in range(ATT_KV_HEADS):
        @pl.when(g == gg)
        def _place():
            for hh in range(ATT_KV_HEADS):
                qs_ref[hh * HEAD_DIM:(hh + 1) * HEAD_DIM, :] = qcat if hh == gg else zero


def _score_block(k_ref, qs_ref, blk, tk, lc, masked):
    kb = k_ref[0, pl.ds(pl.multiple_of(blk * tk, tk), tk), :]
    s = _dot(kb, qs_ref[...])
    if masked:
        kidx = blk * tk + lax.broadcasted_iota(jnp.int32, (tk, 1), 0)
        s = jnp.where(kidx < lc, s, NEG_BIG)
    return s


def _attn_finish(acc_ref, o_ref, tq):
    acc = acc_ref[...]
    o = acc[0:HEAD_DIM] / acc[HEAD_DIM:HEAD_DIM + 1]
    o_ref[0] = jnp.concatenate([o[:, j * tq:(j + 1) * tq] for j in range(ATT_GROUP)],
                               axis=0).astype(o_ref.dtype)


def _attn_bounded_kernel(q_ref, k_ref, v_ref, o_ref, qs_ref, p0_ref, p1_ref, acc_ref,
                         *, lc, tq, tk, nqc, nkc, nk):
    qi = pl.program_id(2)
    _place_queries(q_ref, qs_ref)
    acc_ref[...] = jnp.zeros(acc_ref.shape, F32)

    p_ref = (p0_ref, p1_ref)
    tkv = v_ref.shape[-1]
    nsub = tk // tkv

    def probs(blk, slot):
        p_ref[slot][...] = jnp.exp2(_score_block(k_ref, qs_ref, blk, tk, lc, False)).astype(BF16)

    def weighted(blk, slot):
        pv = _dot(v_ref[0, 0, blk * nsub], p_ref[slot][0:tkv, :])
        for c in range(1, nsub):
            pv = pv + _dot(v_ref[0, 0, blk * nsub + c], p_ref[slot][c * tkv:(c + 1) * tkv, :])
        acc_ref[...] += pv

    @pl.when(qi < nqc)
    def _ctx():
        nsc = -(-lc // tkv)
        s = _dot(k_ref[0, 0:nsc * tkv, :], qs_ref[...])
        if lc % tkv:
            s = jnp.where(lax.broadcasted_iota(jnp.int32, (nsc * tkv, 1), 0) < lc, s, NEG_BIG)
        p = jnp.exp2(s).astype(BF16)
        pv = _dot(v_ref[0, 0, 0], p[0:tkv])
        for c in range(1, nsc):
            pv = pv + _dot(v_ref[0, 0, c], p[c * tkv:(c + 1) * tkv])
        acc_ref[...] += pv

    @pl.when(qi >= nqc)
    def _lat():
        def steady(t, slot):
            probs(t + 1, 1 - slot)
            weighted(t, slot)

        probs(0, 0)
        npairs = (nk - 1) // 2

        def pair(i, carry):
            steady(2 * i, 0)
            steady(2 * i + 1, 1)
            return carry

        lax.fori_loop(0, npairs, pair, 0)
        if (nk - 1) % 2:
            steady(nk - 2, (nk - 2) % 2)
        weighted(nk - 1, (nk - 1) % 2)

    _attn_finish(acc_ref, o_ref, tq)


def _attn_kernel(q_ref, k_ref, v_ref, o_ref, qs_ref, s0_ref, s1_ref, mb0_ref, mb1_ref, p0_ref, p1_ref,
                 al0_ref, al1_ref, m_ref, acc_ref, *, lc, tq, tk, nqc, nkc, nk):
    s_ref, mb_ref = (s0_ref, s1_ref), (mb0_ref, mb1_ref)
    p_ref, al_ref = (p0_ref, p1_ref), (al0_ref, al1_ref)
    qi = pl.program_id(2)

    _place_queries(q_ref, qs_ref)
    m_ref[...] = jnp.full(m_ref.shape, NEG_BIG, F32)
    acc_ref[...] = jnp.zeros(acc_ref.shape, F32)

    def scores(blk, slot, masked=False):
        s = _score_block(k_ref, qs_ref, blk, tk, lc, masked)
        s_ref[slot][...] = s
        mb_ref[slot][...] = jnp.max(s, axis=0, keepdims=True)

    def probs(slot):
        m_prev = m_ref[...]
        m_new = jnp.maximum(m_prev, mb_ref[slot][...])
        al_ref[slot][...] = jnp.exp2(m_prev - m_new)
        p_ref[slot][...] = jnp.exp2(s_ref[slot][...] - m_new).astype(BF16)
        m_ref[...] = m_new

    def weighted(blk, slot):
        acc_ref[...] = al_ref[slot][...] * acc_ref[...] + _dot(v_ref[0, 0, blk], p_ref[slot][...])

    @pl.when(qi < nqc)
    def _ctx():
        for blk in range(nkc):
            scores(blk, 0, masked=True)
            probs(0)
            weighted(blk, 0)

    @pl.when(qi >= nqc)
    def _lat():
        def steady(t, slot):
            scores(t + 2, slot)
            weighted(t, slot)
            probs(1 - slot)

        scores(0, 0)
        scores(1, 1)
        probs(0)
        npairs = (nk - 2) // 2

        def pair(i, carry):
            steady(2 * i, 0)
            steady(2 * i + 1, 1)
            return carry

        lax.fori_loop(0, npairs, pair, 0)
        if (nk - 2) % 2:
            steady(2 * npairs, 0)
        weighted(nk - 2, (nk - 2) % 2)
        probs((nk - 1) % 2)
        weighted(nk - 1, (nk - 1) % 2)

    _attn_finish(acc_ref, o_ref, tq)


def _attention(qt, kn, vt, lc, tq, bounded):
    b, _, _, t = qt.shape
    nkv, tkv = vt.shape[2], vt.shape[4]
    tk = _pick_tile(t, ATTN_BOUNDED_KEYS, mult=tkv) if bounded else tkv
    nk = t // tk
    nq = t // tq
    nqc = lc // tq
    nkc = -(-lc // tk)
    wide = ATT_GROUP * tq
    qs_scratch = pltpu.VMEM((ATT_KV_HEADS * HEAD_DIM, wide), BF16)
    acc_scratch = pltpu.VMEM((HEAD_DIM + 16, wide), F32)
    if bounded:
        body = _attn_bounded_kernel
        scratch = [qs_scratch, pltpu.VMEM((tk, wide), BF16), pltpu.VMEM((tk, wide), BF16), acc_scratch]
    else:
        assert nk >= 2
        body = _attn_kernel
        scratch = [qs_scratch,
                   pltpu.VMEM((tk, wide), F32), pltpu.VMEM((tk, wide), F32),
                   pltpu.VMEM((1, wide), F32), pltpu.VMEM((1, wide), F32),
                   pltpu.VMEM((tk, wide), BF16), pltpu.VMEM((tk, wide), BF16),
                   pltpu.VMEM((1, wide), F32), pltpu.VMEM((1, wide), F32),
                   pltpu.VMEM((1, wide), F32),
                   acc_scratch]
    return pl.pallas_call(
        functools.partial(body, lc=lc, tq=tq, tk=tk, nqc=nqc, nkc=nkc, nk=nk),
        grid=(b, ATT_KV_HEADS, nq),
        in_specs=[
            pl.BlockSpec((1, ATT_GROUP, HEAD_DIM, tq), lambda bi, g, qi: (bi, g, 0, qi)),
            pl.BlockSpec((1, t, ATT_KV_HEADS * HEAD_DIM), lambda bi, g, qi: (bi, 0, 0)),
            pl.BlockSpec((1, 1, nkv, HEAD_DIM + 16, tkv), lambda bi, g, qi: (bi, g, 0, 0, 0)),
        ],
        out_specs=pl.BlockSpec((1, ATT_GROUP * HEAD_DIM, tq), lambda bi, g, qi: (bi, g, qi)),
        out_shape=jax.ShapeDtypeStruct((b, ATT_HEADS * HEAD_DIM, t), BF16),
        scratch_shapes=scratch,
        name="attn_bounded" if bounded else "attn_online",
        compiler_params=pltpu.CompilerParams(
            dimension_semantics=("parallel", "parallel", "arbitrary"),
            vmem_limit_bytes=V7X_VMEM_LIMIT),
    )(qt, kn, vt)


def _hgrn_consts():
    c = HG_CHUNK
    t = np.arange(c)
    tt, rr = t[:, None], t[None, :]
    mats, masks = [], []
    for rev in (False, True):
        mats.append(np.concatenate([(rr >= tt) if rev else (rr <= tt), np.ones((8, c), bool)], axis=0))
        pair_masks = []
        for bsz in HG_LEVELS:
            blk, pos = t // (2 * bsz), t % (2 * bsz)
            qv = (pos < bsz) if rev else (pos >= bsz)
            pair_masks.append((blk[:, None] == blk[None, :]) & qv[:, None] & ~qv[None, :])
        for wsz in HG_WITHIN:
            same = (tt // wsz) == (rr // wsz)
            pair_masks.append(same & ((rr >= tt) if rev else (rr <= tt)))
        masks.append(np.stack([np.tile(m, (HG_HEADS, 1)) for m in pair_masks]))
    return jnp.asarray(np.stack(mats), BF16), jnp.asarray(np.stack(masks), F32)


def _level_reference(cum, bsz, rev):
    c, hw = cum.shape
    i = bsz if rev else bsz - 1
    if 2 * bsz >= 8:
        c3 = cum.reshape(c // (2 * bsz), 2 * bsz, hw)
        return jnp.broadcast_to(c3[:, i:i + 1, :], c3.shape).reshape(c, hw)
    c3 = cum.reshape(c // 8, 8, hw)
    sub = lax.broadcasted_iota(jnp.int32, (1, 8, 1), 1) // (2 * bsz)
    ref = jnp.broadcast_to(c3[:, i:i + 1, :], c3.shape)
    for gidx in range(1, 8 // (2 * bsz)):
        r = gidx * 2 * bsz + i
        ref = jnp.where(sub == gidx, jnp.broadcast_to(c3[:, r:r + 1, :], c3.shape), ref)
    return ref.reshape(c, hw)


def _block_entry_reference(cum, wsz, rev):
    c, hw = cum.shape
    nb = c // wsz
    if nb == 1:
        return jnp.zeros_like(cum)
    c3 = cum.reshape(nb, wsz, hw)
    zero = jnp.zeros((1, 1, hw), F32)
    if rev:
        edge = jnp.concatenate([c3[1:, 0:1, :], zero], axis=0)
    else:
        edge = jnp.concatenate([zero, c3[:-1, wsz - 1:wsz, :]], axis=0)
    return jnp.broadcast_to(edge, c3.shape).reshape(c, hw)


def _hgrn_chunk(d, r0, kk_ref, lf_ref, v_ref, q_ref, cm_ref, am_ref, o_ref, s_ref, hw, within):
    c = HG_CHUNK
    dk = hw // HG_HEADS
    rev = d == 1
    rows = pl.ds(r0, c)
    lf = lf_ref[0, rows, :]
    cm = cm_ref[d]
    hi, mid, lo = _split3(lf)
    e = _dot(cm, hi) + _dot(cm, mid) + _dot(cm, lo)
    cum, tot = e[0:c], e[c:c + 1]
    q = q_ref[0, rows, :]
    k = kk_ref[0, rows, :]
    v = v_ref[0, rows, :]
    qdec = q * jnp.exp(cum)
    kdec = k * jnp.exp(tot - cum)
    ds = jnp.exp(tot)

    lane_head = lax.broadcasted_iota(jnp.int32, (1, hw), 1) // dk
    heads = [lane_head == hh for hh in range(HG_HEADS)]
    tpos = lax.broadcasted_iota(jnp.int32, (c, 1), 0)

    def pair_product(qd, kd, mask_idx):
        q4 = jnp.concatenate([jnp.where(hm, qd, 0.0) for hm in heads], axis=0).astype(BF16)
        return am_ref[d, mask_idx] * _dot_nt(q4, kd.astype(BF16))

    att = jnp.zeros((HG_HEADS * c, c), F32)
    for lv, bsz in enumerate(HG_LEVELS):
        if bsz < within:
            continue
        pos = tpos % (2 * bsz)
        is_q = (pos < bsz) if rev else (pos >= bsz)
        diff = cum - _level_reference(cum, bsz, rev)
        w = jnp.exp(jnp.where(is_q, diff, -diff))
        att = att + pair_product(jnp.where(is_q, q * w, 0.0), jnp.where(is_q, 0.0, k * w), lv)
    widx = len(HG_LEVELS) + HG_WITHIN.index(within)
    if within == 1:
        att = att + pair_product(q, k, widx)
    else:
        diff = cum - _block_entry_reference(cum, within, rev)
        att = att + pair_product(q * jnp.exp(diff), k * jnp.exp(-diff), widx)

    ov = _dot(att.astype(BF16), v)
    o = jnp.zeros((c, hw), F32)
    for hh, hm in enumerate(heads):
        o = o + jnp.where(hm, ov[hh * c:(hh + 1) * c], 0.0)

    st = s_ref[...]
    o = o + _dot_nt(qdec.astype(BF16), st.astype(BF16))
    o_ref[0, rows, :] = o
    u = _dot_tn(v, kdec.astype(BF16))
    rh = lax.broadcasted_iota(jnp.int32, (hw, 1), 0) // dk
    s_ref[...] = st * ds + jnp.where(rh == lane_head, u, 0.0)


def _hgrn_kernel(kf_ref, lff_ref, vf_ref, qf_ref, kb_ref, lfb_ref, vb_ref, qb_ref,
                 cm_ref, am_ref, of_ref, ob_ref, sf_ref, sb_ref, *, nchunk, hw, within):
    @pl.when(pl.program_id(1) == 0)
    def _init():
        sf_ref[...] = jnp.zeros(sf_ref.shape, F32)
        sb_ref[...] = jnp.zeros(sb_ref.shape, F32)

    for jj in range(nchunk):
        _hgrn_chunk(0, jj * HG_CHUNK, kf_ref, lff_ref, vf_ref, qf_ref,
                    cm_ref, am_ref, of_ref, sf_ref, hw, within)
        _hgrn_chunk(1, (nchunk - 1 - jj) * HG_CHUNK, kb_ref, lfb_ref, vb_ref, qb_ref,
                    cm_ref, am_ref, ob_ref, sb_ref, hw, within)


def _hgrn(kf, lff, kb, lfb, v, hq, hc, lc, nchunk, within):
    b, t, hw = kf.shape
    tb = nchunk * HG_CHUNK
    nblk, ncb = t // tb, lc // tb
    cm, am = hc
    fwd = lambda bi, p: (bi, p, 0)
    bwd = lambda bi, p: (bi, jnp.where(p < ncb, ncb - 1 - p, nblk - 1 - (p - ncb)), 0)
    blk = lambda im: pl.BlockSpec((1, tb, hw), im)
    return pl.pallas_call(
        functools.partial(_hgrn_kernel, nchunk=nchunk, hw=hw, within=within),
        grid=(b, nblk),
        in_specs=[blk(fwd)] * 4 + [blk(bwd)] * 4 + [_resident(cm.shape), _resident(am.shape)],
        out_specs=[blk(fwd), blk(bwd)],
        out_shape=[jax.ShapeDtypeStruct((b, t, hw), F32)] * 2,
        scratch_shapes=[pltpu.VMEM((hw, hw), F32), pltpu.VMEM((hw, hw), F32)],
        name=f"hgrn_within{within}",
        compiler_params=pltpu.CompilerParams(
            dimension_semantics=("parallel", "arbitrary"),
            vmem_limit_bytes=V7X_VMEM_LIMIT),
    )(kf, lff, v, hq, kb, lfb, v, hq, cm, am)


def _outproj_kernel(x_ref, mb_ref, mc_ref, y_ref, at_ref, of_ref, ob_ref, g_ref, wf_ref, wo_ref,
                    gn_ref, o_ref, *, lc, tm, d, fw, aw, hw):
    i = pl.program_id(1)
    row = i * tm + lax.broadcasted_iota(jnp.int32, (tm, 1), 0)
    is_ctx = row < lc
    mb, mc = mb_ref[0, 0], mc_ref[0, 0]
    gate = jnp.where(is_ctx, mc[:, 2 * d:3 * d], mb[:, 2 * d:3 * d])

    fx = _dot(y_ref[0], wf_ref[...]).astype(BF16)
    mix = _dot(fx, wo_ref[0:fw, :])
    mix = mix + _dot_tn(at_ref[0], wo_ref[fw:fw + aw, :])

    o = of_ref[0] + ob_ref[0]
    dv = hw // HG_HEADS
    lane_head = lax.broadcasted_iota(jnp.int32, (1, hw), 1) // dv
    o2 = o * o
    ms = jnp.zeros_like(o)
    for hh in range(HG_HEADS):
        sel = lane_head == hh
        ssh = jnp.sum(jnp.where(sel, o2, 0.0), axis=-1, keepdims=True)
        ms = jnp.where(sel, ssh * (1.0 / dv), ms)
    gg = g_ref[0]
    rx = (o * lax.rsqrt(ms + EPS) * gn_ref[...]) * (gg * jax.nn.sigmoid(gg))
    mix = mix + _dot(rx.astype(BF16), wo_ref[fw + aw:fw + aw + hw, :])
    o_ref[0] = x_ref[0] + gate * mix


def _outproj(xs, mod4, l, nb, y, at, of, ob, g, wf, wo, gn, lc, tm):
    b, t, d = xs.shape
    n6 = mod4.shape[-1]
    fw, hw = y.shape[-1], of.shape[-1]
    aw = at.shape[1]
    tok = lambda w: pl.BlockSpec((1, tm, w), lambda bi, i: (bi, i, 0))
    return pl.pallas_call(
        functools.partial(_outproj_kernel, lc=lc, tm=tm, d=d, fw=fw, aw=aw, hw=hw),
        grid=(b, t // tm),
        in_specs=[tok(d),
                  pl.BlockSpec((1, 1, 1, n6), lambda bi, i: (l, bi, 0, 0)),
                  pl.BlockSpec((1, 1, 1, n6), lambda bi, i: (l, nb, 0, 0)),
                  tok(fw),
                  pl.BlockSpec((1, aw, tm), lambda bi, i: (bi, 0, i)),
                  tok(hw), tok(hw), tok(hw),
                  _resident(wf.shape), _resident(wo.shape), _resident(gn.shape)],
        out_specs=tok(d),
        out_shape=jax.ShapeDtypeStruct((b, t, d), F32),
        name="outproj",
        compiler_params=pltpu.CompilerParams(
            dimension_semantics=("parallel", "parallel"),
            vmem_limit_bytes=V7X_VMEM_LIMIT),
    )(xs, mod4, mod4, y, at, of, ob, g, wf, wo, gn)


def _ffn_kernel(x_ref, mb_ref, mc_ref, wg_ref, wu_ref, wd_ref, *rest, lc, tm, d, fchunk):
    o_ref = rest[-1]
    i = pl.program_id(1)
    x = x_ref[0]
    row = i * tm + lax.broadcasted_iota(jnp.int32, (tm, 1), 0)
    is_ctx = row < lc
    mb, mc = mb_ref[0, 0], mc_ref[0, 0]
    h = _modulated(x, mb, mc, is_ctx, d, 3, 4).astype(BF16)
    gate = jnp.where(is_ctx, mc[:, 5 * d:6 * d], mb[:, 5 * d:6 * d])
    dff = wg_ref.shape[1]
    acc = jnp.zeros((tm, d), F32)
    for c0 in range(0, dff, fchunk):
        a = _dot(h, wg_ref[:, c0:c0 + fchunk])
        u = _dot(h, wu_ref[:, c0:c0 + fchunk])
        act = (a * jax.nn.sigmoid(a) * u).astype(BF16)
        acc = acc + _dot(act, wd_ref[c0:c0 + fchunk, :])
    y = x + gate * acc
    if len(rest) == 2:
        y = y * lax.rsqrt(jnp.mean(y * y, axis=-1, keepdims=True) + EPS) * rest[0][...]
    o_ref[0] = y


def _ffn(xs, mod4, l, nb, wg, wu, wd, lc, tm, final_gain=None):
    b, t, d = xs.shape
    n6 = mod4.shape[-1]
    dff = wg.shape[1]
    fchunk = _pick_tile(dff, 768)
    extra_in = [] if final_gain is None else [final_gain.reshape(1, d)]
    return pl.pallas_call(
        functools.partial(_ffn_kernel, lc=lc, tm=tm, d=d, fchunk=fchunk),
        grid=(b, t // tm),
        in_specs=[pl.BlockSpec((1, tm, d), lambda bi, i: (bi, i, 0)),
                  pl.BlockSpec((1, 1, 1, n6), lambda bi, i: (l, bi, 0, 0)),
                  pl.BlockSpec((1, 1, 1, n6), lambda bi, i: (l, nb, 0, 0)),
                  _resident(wg.shape), _resident(wu.shape), _resident(wd.shape)] +
                 [_resident(a.shape) for a in extra_in],
        out_specs=pl.BlockSpec((1, tm, d), lambda bi, i: (bi, i, 0)),
        out_shape=jax.ShapeDtypeStruct((b, t, d), F32),
        name="ffn" if final_gain is None else "ffn_final",
        compiler_params=pltpu.CompilerParams(
            dimension_semantics=("parallel", "parallel"),
            vmem_limit_bytes=V7X_VMEM_LIMIT),
    )(xs, mod4, mod4, wg, wu, wd, *extra_in)


def _rope_tables(lc, s):
    half = HEAD_DIM // 2
    n = jnp.arange(s)
    freqs = ROPE_THETA ** (-jnp.arange(0, half, 2, dtype=F32) / half)
    ang_r = (n // GRID_W).astype(F32)[:, None] * freqs
    ang_c = (n % GRID_W).astype(F32)[:, None] * freqs
    pad = lambda a, v: jnp.concatenate([jnp.full((lc, a.shape[1]), v, F32), a], axis=0)
    cr, sr = pad(jnp.cos(ang_r), 1.0), pad(jnp.sin(ang_r), 0.0)
    cc, sc = pad(jnp.cos(ang_c), 1.0), pad(jnp.sin(ang_c), 0.0)
    cos_h = jnp.concatenate([cr, cr, cc, cc], axis=1)
    sin_h = jnp.concatenate([-sr, sr, -sc, sc], axis=1)
    ckn = jnp.tile(cos_h, (1, ATT_KV_HEADS))
    skn = jnp.tile(sin_h, (1, ATT_KV_HEADS))
    cqt = jnp.concatenate([cr, cc], axis=1).T
    sqt = jnp.concatenate([sr, sc], axis=1).T
    return ckn, skn, cqt, sqt


def kernel(x, c, ctx, c_ctx, w_ada, b_ada, w_in, w_four, q_norm, k_norm, hg_lb_logits, hg_norm,
           w_out, w_gate, w_up, w_down, final_norm):
    b, s, d = x.shape
    lc = ctx.shape[1]
    t = lc + s
    depth = w_ada.shape[0]
    fw = w_four.shape[-1]
    hw = hg_lb_logits.shape[-1]
    qw = ATT_HEADS * HEAD_DIM
    kw = ATT_KV_HEADS * HEAD_DIM
    assert w_in.shape[-1] == fw + qw + 2 * kw + 5 * hw
    assert s % (DFT_N2 * 8) == 0 and lc % 128 == 0 and s % GRID_W == 0

    tq = 256 if (lc % 256 == 0 and t % 256 == 0) else 128
    tkv = tq
    tm = _pick_tile(t, 768, mult=tkv)
    nchunk = max(g for g in (4, 2, 1) if lc % (g * HG_CHUNK) == 0 and t % (g * HG_CHUNK) == 0)

    xs = jnp.concatenate([ctx, x], axis=1)
    r8 = -(-(b + 1) // 8) * 8
    cvec = jnp.concatenate([c, c_ctx[None], jnp.zeros((r8 - b - 1, d), F32)], axis=0)
    mod = _adaln(cvec, w_ada, b_ada)
    mod4 = mod.reshape(depth, r8, 1, mod.shape[-1])

    lb_sm = jax.nn.softmax(hg_lb_logits.astype(F32), axis=1)
    lb_all = jnp.cumsum(lb_sm, axis=1) - lb_sm[:, :1]

    ckn, skn, cqt, sqt = _rope_tables(lc, s)
    fc = _fourier_consts(lc, s)
    hc = _hgrn_consts()

    o_q, o_k, o_v, o_h = fw, fw + qw, fw + qw + kw, fw + qw + 2 * kw
    for l in range(depth):
        wl = w_in[l]
        wn = jnp.concatenate([wl[:, 0:fw], wl[:, o_k:o_v], wl[:, o_h:]], axis=1).astype(BF16)
        wt = jnp.concatenate([wl[:, o_q:o_k], wl[:, o_v:o_h]], axis=1).T.astype(BF16)
        gk = jnp.tile(k_norm[l], ATT_KV_HEADS)[None]
        gq = q_norm[l][:, None]
        gn = jnp.tile(hg_norm[l], HG_HEADS)[None]
        (fz, kn, kf, lff, kb, lfb, v, hq, g, qt, vt, gd) = _inproj(
            xs, mod4, l, b, wn, wt, gk, ckn, skn, gq, cqt, sqt,
            lb_all[0, l][None], lb_all[1, l][None], lc, tm, tkv)
        y = _fourier(fz, fc, lc)
        bound = (HEAD_DIM ** 0.5 * LOG2E * BOUND_SLACK) * jnp.max(jnp.abs(q_norm[l])) * jnp.max(jnp.abs(k_norm[l]))
        at = lax.cond(bound < MAX_SAFE_LOG2_SCORE,
                      functools.partial(_attention, lc=lc, tq=tq, bounded=True),
                      functools.partial(_attention, lc=lc, tq=tq, bounded=False),
                      qt, kn, vt)
        hg_scan = functools.partial(_hgrn, hc=hc, lc=lc, nchunk=nchunk, within=HG_WITHIN[0])
        for gi, wsz in enumerate(HG_WITHIN[1:]):
            hg_scan = functools.partial(
                lambda ok, big, small, *a: lax.cond(ok, big, small, *a),
                jnp.max(gd[:, :, gi, 0]) < HG_MAX_BLOCK_DECAY,
                functools.partial(_hgrn, hc=hc, lc=lc, nchunk=nchunk, within=wsz), hg_scan)
        of, ob = hg_scan(kf, lff, kb, lfb, v, hq)
        xs = _outproj(xs, mod4, l, b, y, at, of, ob, g, w_four[l].astype(BF16),
                      w_out[l].astype(BF16), gn, lc, tm)
        xs = _ffn(xs, mod4, l, b, w_gate[l].astype(BF16), w_up[l].astype(BF16), w_down[l].astype(BF16),
                  lc, tm, final_norm if l == depth - 1 else None)
    return xs[:, lc:]
```

```python
import functools

import numpy as np
import jax
import jax.numpy as jnp
from jax import lax
from jax.experimental import pallas as pl
from jax.experimental.pallas import tpu as pltpu

F32 = jnp.float32
BF16 = jnp.bfloat16

EPS = 1e-6
GRID_W = 64
ROPE_THETA = 10000.0
FOURIER_GD = 64
ATT_HEADS = 8
ATT_KV_HEADS = 2
ATT_GROUP = ATT_HEADS // ATT_KV_HEADS
HEAD_DIM = 64
HG_HEADS = 4
HG_CHUNK = 64
HG_LEVELS = (32, 16, 8, 4, 2, 1)
HG_WITHIN = (1, 16, 64)
HG_MAX_BLOCK_DECAY = 80.0
DFT_N2 = 128
DFT_ROW_PAD = 8
NEG_BIG = -1e30
LOG2E = 1.4426950408889634
BOUND_SLACK = 1.05
MAX_SAFE_LOG2_SCORE = 100.0
V7X_VMEM_LIMIT = 56 * 1024 * 1024


def _dot(a, b):
    return jnp.dot(a, b, preferred_element_type=F32)


def _dot_nt(a, b):
    return lax.dot_general(a, b, (((1,), (1,)), ((), ())), preferred_element_type=F32)


def _dot_tn(a, b):
    return lax.dot_general(a, b, (((0,), (0,)), ((), ())), preferred_element_type=F32)


def _split3(x):
    hi = x.astype(BF16)
    r1 = x - hi.astype(F32)
    mid = r1.astype(BF16)
    lo = (r1 - mid.astype(F32)).astype(BF16)
    return hi, mid, lo


def _pick_tile(n, cap, mult=128):
    best = None
    for t in range(mult, min(n, cap) + 1, mult):
        if n % t == 0:
            best = t
    if best is None:
        raise ValueError(f"no tile for {n}")
    return best


def _resident(shape):
    nd = len(shape)
    return pl.BlockSpec(shape, lambda *_: (0,) * nd, pipeline_mode=pl.Buffered(1))


def _adaln_kernel(c_ref, w_ref, b_ref, o_ref):
    cv = c_ref[...]
    a = cv * jax.nn.sigmoid(cv)
    w = w_ref[0]
    a_hi = a.astype(BF16)
    a_lo = (a - a_hi.astype(F32)).astype(BF16)
    w_hi = w.astype(BF16)
    w_lo = (w - w_hi.astype(F32)).astype(BF16)
    acc = _dot(a_hi, w_hi) + _dot(a_lo, w_hi) + _dot(a_hi, w_lo)
    o_ref[0] = acc + b_ref[0]


def _adaln(cvec, w_ada, b_ada):
    depth, d, n6 = w_ada.shape
    r8 = cvec.shape[0]
    tn = _pick_tile(n6, 1536)
    return pl.pallas_call(
        _adaln_kernel,
        grid=(depth, n6 // tn),
        in_specs=[
            pl.BlockSpec((r8, d), lambda l, j: (0, 0)),
            pl.BlockSpec((1, d, tn), lambda l, j: (l, 0, j)),
            pl.BlockSpec((1, 1, tn), lambda l, j: (l, 0, j)),
        ],
        out_specs=pl.BlockSpec((1, r8, tn), lambda l, j: (l, 0, j)),
        out_shape=jax.ShapeDtypeStruct((depth, r8, n6), F32),
        name="adaln",
        compiler_params=pltpu.CompilerParams(
            dimension_semantics=("parallel", "parallel"),
            vmem_limit_bytes=V7X_VMEM_LIMIT),
    )(cvec, w_ada, b_ada.reshape(depth, 1, n6))


def _modulated(x, mb, mc, is_ctx, d, k_shift, k_scale):
    r = lax.rsqrt(jnp.mean(x * x, axis=-1, keepdims=True) + EPS)
    sh = jnp.where(is_ctx, mc[:, k_shift * d:(k_shift + 1) * d], mb[:, k_shift * d:(k_shift + 1) * d])
    sc = jnp.where(is_ctx, mc[:, k_scale * d:(k_scale + 1) * d], mb[:, k_scale * d:(k_scale + 1) * d])
    return (x * r) * (1.0 + sc) + sh


def _inproj_kernel(x_ref, mb_ref, mc_ref, wn_ref, wt_ref, gk_ref, ckn_ref, skn_ref,
                   gq_ref, cqt_ref, sqt_ref, lbf_ref, lbb_ref,
                   fz_ref, k_ref, kf_ref, lff_ref, kb_ref, lfb_ref, v_ref, hq_ref, g_ref,
                   qt_ref, vt_ref, gd_ref, *, lc, tm, tkv, d, fw, hw):
    i = pl.program_id(1)
    x = x_ref[0]
    row = i * tm + lax.broadcasted_iota(jnp.int32, (tm, 1), 0)
    is_ctx = row < lc
    h = _modulated(x, mb_ref[0, 0], mc_ref[0, 0], is_ctx, d, 0, 1).astype(BF16)

    un = _dot(h, wn_ref[...])
    fz_ref[0] = un[:, 0:fw]

    o = fw
    kw = ATT_KV_HEADS * HEAD_DIM
    uk = un[:, o:o + kw]
    lane = lax.broadcasted_iota(jnp.int32, (1, kw), 1)
    uk2 = uk * uk
    ms = jnp.zeros_like(uk)
    for hh in range(ATT_KV_HEADS):
        sel = (lane // HEAD_DIM) == hh
        ssh = jnp.sum(jnp.where(sel, uk2, 0.0), axis=-1, keepdims=True)
        ms = jnp.where(sel, ssh * (1.0 / HEAD_DIM), ms)
    kn = uk * lax.rsqrt(ms + EPS) * gk_ref[...]
    first = (lane % 32) < 16
    partner = jnp.where(first, pltpu.roll(kn, kw - 16, axis=1), pltpu.roll(kn, 16, axis=1))
    k_ref[0] = (kn * ckn_ref[...] + partner * skn_ref[...]).astype(BF16)
    o += kw

    block_decay = [jnp.zeros((1, 1), F32) for _ in HG_WITHIN[1:]]
    for lb_ref, kk_ref, lf_ref in ((lbf_ref, kf_ref, lff_ref), (lbb_ref, kb_ref, lfb_ref)):
        z = un[:, o:o + hw]
        lb = lb_ref[...]
        kk_ref[0] = (1.0 - lb) * jax.nn.sigmoid(-z)
        lf = jnp.log(lb + (1.0 - lb) * jax.nn.sigmoid(z))
        lf_ref[0] = lf
        for gi, wsz in enumerate(HG_WITHIN[1:]):
            tot = -jnp.sum(lf.reshape(tm // wsz, wsz, hw), axis=1)
            top = jnp.max(jnp.max(tot, axis=0, keepdims=True), axis=1, keepdims=True)
            block_decay[gi] = jnp.maximum(block_decay[gi], top)
        o += hw
    grow = lax.broadcasted_iota(jnp.int32, (8, 128), 0)
    gd = jnp.zeros((8, 128), F32)
    for gi, dec in enumerate(block_decay):
        gd = jnp.where(grow == gi, dec, gd)
    gd_ref[0, 0] = gd
    v_ref[0] = un[:, o:o + hw].astype(BF16)
    hq_ref[0] = un[:, o + hw:o + 2 * hw]
    g_ref[0] = un[:, o + 2 * hw:o + 3 * hw]

    ut = _dot_nt(wt_ref[...], h)
    qw = ATT_HEADS * HEAD_DIM
    uq = ut[0:qw].reshape(ATT_HEADS, HEAD_DIM, tm)
    msq = jnp.mean(uq * uq, axis=1, keepdims=True)
    qn = uq * lax.rsqrt(msq + EPS) * gq_ref[...][None]
    cr, sr = cqt_ref[0:16][None], sqt_ref[0:16][None]
    cc, sc = cqt_ref[16:32][None], sqt_ref[16:32][None]
    x1r, x2r, x1c, x2c = qn[:, 0:16], qn[:, 16:32], qn[:, 32:48], qn[:, 48:64]
    qr = jnp.concatenate([x1r * cr - x2r * sr, x1r * sr + x2r * cr,
                          x1c * cc - x2c * sc, x1c * sc + x2c * cc], axis=1)
    qt_ref[0] = (qr * (HEAD_DIM ** -0.5 * LOG2E)).astype(BF16)

    uv = ut[qw:qw + kw].reshape(ATT_KV_HEADS, HEAD_DIM, tm)
    ones_row = lax.broadcasted_iota(jnp.int32, (ATT_KV_HEADS, 16, tm), 1) == 0
    aug = jnp.where(ones_row, 1.0, 0.0).astype(F32)
    vaug = jnp.concatenate([uv, aug], axis=1).astype(BF16)
    for cb in range(tm // tkv):
        vt_ref[0, :, cb] = vaug[:, :, cb * tkv:(cb + 1) * tkv]


def _inproj(xs, mod4, l, nb, wn, wt, gk, ckn, skn, gq, cqt, sqt, lbf, lbb, lc, tm, tkv):
    b, t, d = xs.shape
    n6 = mod4.shape[-1]
    fw = FOURIER_GD * 4
    hw = lbf.shape[-1]
    kw = ATT_KV_HEADS * HEAD_DIM
    tok = lambda w, dt: jax.ShapeDtypeStruct((b, t, w), dt)
    tok_spec = lambda w: pl.BlockSpec((1, tm, w), lambda bi, i: (bi, i, 0))
    out_shape = [tok(fw, F32), tok(kw, BF16), tok(hw, F32), tok(hw, F32), tok(hw, F32), tok(hw, F32),
                 tok(hw, BF16), tok(hw, F32), tok(hw, F32),
                 jax.ShapeDtypeStruct((b, ATT_HEADS, HEAD_DIM, t), BF16),
                 jax.ShapeDtypeStruct((b, ATT_KV_HEADS, t // tkv, HEAD_DIM + 16, tkv), BF16),
                 jax.ShapeDtypeStruct((b, t // tm, 8, 128), F32)]
    out_specs = [tok_spec(fw), tok_spec(kw)] + [tok_spec(hw)] * 7 + [
        pl.BlockSpec((1, ATT_HEADS, HEAD_DIM, tm), lambda bi, i: (bi, 0, 0, i)),
        pl.BlockSpec((1, ATT_KV_HEADS, tm // tkv, HEAD_DIM + 16, tkv), lambda bi, i: (bi, 0, i, 0, 0)),
        pl.BlockSpec((1, 1, 8, 128), lambda bi, i: (bi, i, 0, 0))]
    in_specs = [
        pl.BlockSpec((1, tm, d), lambda bi, i: (bi, i, 0)),
        pl.BlockSpec((1, 1, 1, n6), lambda bi, i: (l, bi, 0, 0)),
        pl.BlockSpec((1, 1, 1, n6), lambda bi, i: (l, nb, 0, 0)),
        _resident(wn.shape), _resident(wt.shape), _resident(gk.shape),
        pl.BlockSpec((tm, kw), lambda bi, i: (i, 0)),
        pl.BlockSpec((tm, kw), lambda bi, i: (i, 0)),
        _resident(gq.shape),
        pl.BlockSpec((32, tm), lambda bi, i: (0, i)),
        pl.BlockSpec((32, tm), lambda bi, i: (0, i)),
        _resident(lbf.shape), _resident(lbb.shape),
    ]
    return pl.pallas_call(
        functools.partial(_inproj_kernel, lc=lc, tm=tm, tkv=tkv, d=d, fw=fw, hw=hw),
        grid=(b, t // tm),
        in_specs=in_specs, out_specs=out_specs, out_shape=out_shape,
        name="inproj",
        compiler_params=pltpu.CompilerParams(
            dimension_semantics=("parallel", "parallel"),
            vmem_limit_bytes=V7X_VMEM_LIMIT),
    )(xs, mod4, mod4, wn, wt, gk, ckn, skn, gq, cqt, sqt, lbf, lbb)


def _fourier_kernel(z_ref, f1_ref, twc_ref, tws_ref, c2_ref, s2_ref, cs_ref, cl_ref,
                    y_ref, a1_ref, p_ref, *, lc, n1, scale_c, scale_x):
    n2 = DFT_N2
    cs = cs_ref[...]
    pa, pp = 2 * n1 + DFT_ROW_PAD, n2 + DFT_ROW_PAD

    zc = z_ref[0, 0:lc, :].astype(BF16)
    pc = _dot(cl_ref[...], zc)
    pcat = jnp.concatenate([pc[0:lc], pc[lc:2 * lc]], axis=1).astype(BF16)
    y_ref[0, 0:lc, :] = (_dot(pcat, cs) * scale_c).astype(y_ref.dtype)

    f1 = f1_ref[...]

    def stage1(j, carry):
        xj = z_ref[0, pl.ds(lc + j, n1, stride=n2), :].astype(BF16)
        a1_ref[pl.ds(pl.multiple_of(j * pa, 8), 2 * n1), :] = _dot(f1, xj)
        return carry

    lax.fori_loop(0, n2, stage1, 0, unroll=8)

    c2 = c2_ref[...]
    s2 = s2_ref[...]

    group = 8 if n1 % 8 == 0 else 1

    def stage2(gidx, carry):
        k1s = [gidx * group + u for u in range(group)]
        pris = []
        for k1 in k1s:
            ar = a1_ref[pl.ds(k1, n2, stride=pa), :]
            ai = a1_ref[pl.ds(n1 + k1, n2, stride=pa), :]
            twc = twc_ref[pl.ds(k1, 1), :]
            tws = tws_ref[pl.ds(k1, 1), :]
            gr = c2 * twc - s2 * tws
            gi = -(s2 * twc + c2 * tws)
            gm = jnp.concatenate([jnp.concatenate([gr, -gi], axis=1),
                                  jnp.concatenate([gi, gr], axis=1)], axis=0).astype(BF16)
            rhs = jnp.concatenate([ar, ai], axis=0).astype(BF16)
            pris.append(_dot(gm, rhs))
        for k1, pri in zip(k1s, pris):
            pcat2 = jnp.concatenate([pri[0:n2], pri[n2:2 * n2]], axis=1).astype(BF16)
            p_ref[pl.ds(pl.multiple_of(k1 * pp, 8), n2), :] = _dot(pcat2, cs) * scale_x
        return carry

    lax.fori_loop(0, n1 // group, stage2, 0)

    def stage3(k2, carry):
        blk = p_ref[pl.ds(k2, n1, stride=pp), :]
        y_ref[0, pl.ds(pl.multiple_of(lc + k2 * n1, n1), n1), :] = blk.astype(y_ref.dtype)
        return carry

    lax.fori_loop(0, n2, stage3, 0, unroll=8)


def _fourier_consts(lc, s):
    n2 = DFT_N2
    n1 = s // n2
    f64 = np.float64
    k = np.arange(n1, dtype=f64)
    a1 = 2 * np.pi * np.outer(k, k) / n1
    f1 = np.concatenate([np.cos(a1), -np.sin(a1)], axis=0)
    at = 2 * np.pi * np.outer(np.arange(n1, dtype=f64), np.arange(n2, dtype=f64)) / s
    k2 = np.arange(n2, dtype=f64)
    a2 = 2 * np.pi * np.outer(k2, k2) / n2
    w = 2 * FOURIER_GD
    ch = np.arange(w)
    same = (ch[:, None] // FOURIER_GD) == (ch[None, :] // FOURIER_GD)
    ag = 2 * np.pi * np.outer(ch % FOURIER_GD, ch % FOURIER_GD) / FOURIER_GD
    cs = np.concatenate([np.where(same, np.cos(ag), 0.0), np.where(same, np.sin(ag), 0.0)], axis=0)
    kc = np.arange(lc, dtype=f64)
    al = 2 * np.pi * np.outer(kc, kc) / lc
    cl = np.concatenate([np.cos(al), -np.sin(al)], axis=0)
    return dict(
        f1=jnp.asarray(f1, BF16), twc=jnp.asarray(np.cos(at), F32), tws=jnp.asarray(np.sin(at), F32),
        c2=jnp.asarray(np.cos(a2), F32), s2=jnp.asarray(np.sin(a2), F32),
        cs=jnp.asarray(cs, BF16), cl=jnp.asarray(cl, BF16), n1=n1)


def _fourier(fz, fc, lc):
    b, t, fw = fz.shape
    s = t - lc
    n1 = fc["n1"]
    w = 2 * FOURIER_GD
    consts = [fc[k] for k in ("f1", "twc", "tws", "c2", "s2", "cs", "cl")]
    return pl.pallas_call(
        functools.partial(_fourier_kernel, lc=lc, n1=n1,
                          scale_c=float((lc * FOURIER_GD) ** -0.5),
                          scale_x=float((s * FOURIER_GD) ** -0.5)),
        grid=(b, fw // w),
        in_specs=[pl.BlockSpec((1, t, w), lambda bi, hi: (bi, 0, hi))] +
                 [_resident(c.shape) for c in consts],
        out_specs=pl.BlockSpec((1, t, w), lambda bi, hi: (bi, 0, hi)),
        out_shape=jax.ShapeDtypeStruct((b, t, fw), BF16),
        scratch_shapes=[pltpu.VMEM((DFT_N2 * (2 * n1 + DFT_ROW_PAD), w), F32),
                        pltpu.VMEM((n1 * (DFT_N2 + DFT_ROW_PAD), w), F32)],
        name="fourier",
        compiler_params=pltpu.CompilerParams(
            dimension_semantics=("parallel", "parallel"),
            vmem_limit_bytes=V7X_VMEM_LIMIT),
    )(fz, *consts)


def _place_queries(q_ref, qs_ref):
    g = pl.program_id(1)
    qcat = jnp.concatenate([q_ref[0, j] for j in range(ATT_GROUP)], axis=1)
    zero = jnp.zeros_like(qcat)
    for gg in range(ATT_KV_HEADS):
        @pl.when(g == gg)
        def _place():
            for hh in range(ATT_KV_HEADS):
                qs_ref[hh * HEAD_DIM:(hh + 1) * HEAD_DIM, :] = qcat if hh == gg else zero


def _score_block(k_ref, qs_ref, blk, tk, lc, masked):
    kb = k_ref[0, pl.ds(pl.multiple_of(blk * tk, tk), tk), :]
    s = _dot(kb, qs_ref[...])
    if masked:
        kidx = blk * tk + lax.broadcasted_iota(jnp.int32, (tk, 1), 0)
        s = jnp.where(kidx < lc, s, NEG_BIG)
    return s


def _attn_finish(acc_ref, o_ref, tq):
    acc = acc_ref[...]
    o = acc[0:HEAD_DIM] / acc[HEAD_DIM:HEAD_DIM + 1]
    o_ref[0] = jnp.concatenate([o[:, j * tq:(j + 1) * tq] for j in range(ATT_GROUP)],
                               axis=0).astype(o_ref.dtype)


def _attn_bounded_kernel(q_ref, k_ref, v_ref, o_ref, qs_ref, acc_ref, *, lc, tq, tk, nqc, nkc, nk):
    qi = pl.program_id(2)
    _place_queries(q_ref, qs_ref)
    acc_ref[...] = jnp.zeros(acc_ref.shape, F32)
    tkv = tk

    @pl.when(qi < nqc)
    def _ctx():
        nsc = -(-lc // tkv)
        s = _dot(k_ref[0, 0:nsc * tkv, :], qs_ref[...])
        if lc % tkv:
            s = jnp.where(lax.broadcasted_iota(jnp.int32, (nsc * tkv, 1), 0) < lc, s, NEG_BIG)
        p = jnp.exp2(s).astype(BF16)
        pv = _dot(v_ref[0, 0, 0], p[0:tkv])
        for c in range(1, nsc):
            pv = pv + _dot(v_ref[0, 0, c], p[c * tkv:(c + 1) * tkv])
        acc_ref[...] += pv

    @pl.when(qi >= nqc)
    def _lat():
        def probs_of(c):
            return jnp.exp2(_dot(k_ref[0, c * tkv:(c + 1) * tkv, :], qs_ref[...])).astype(BF16)

        p_cur = probs_of(0)
        acc = jnp.zeros(acc_ref.shape, F32)
        for c in range(nk):
            p_next = probs_of(c + 1) if c + 1 < nk else None
            acc = acc + _dot(v_ref[0, 0, c], p_cur)
            p_cur = p_next
        acc_ref[...] = acc

    _attn_finish(acc_ref, o_ref, tq)


def _attn_kernel(q_ref, k_ref, v_ref, o_ref, qs_ref, s0_ref, s1_ref, mb0_ref, mb1_ref, p0_ref, p1_ref,
                 al0_ref, al1_ref, m_ref, acc_ref, *, lc, tq, tk, nqc, nkc, nk):
    s_ref, mb_ref = (s0_ref, s1_ref), (mb0_ref, mb1_ref)
    p_ref, al_ref = (p0_ref, p1_ref), (al0_ref, al1_ref)
    qi = pl.program_id(2)

    _place_queries(q_ref, qs_ref)
    m_ref[...] = jnp.full(m_ref.shape, NEG_BIG, F32)
    acc_ref[...] = jnp.zeros(acc_ref.shape, F32)

    def scores(blk, slot, masked=False):
        s = _score_block(k_ref, qs_ref, blk, tk, lc, masked)
        s_ref[slot][...] = s
        mb_ref[slot][...] = jnp.max(s, axis=0, keepdims=True)

    def probs(slot):
        m_prev = m_ref[...]
        m_new = jnp.maximum(m_prev, mb_ref[slot][...])
        al_ref[slot][...] = jnp.exp2(m_prev - m_new)
        p_ref[slot][...] = jnp.exp2(s_ref[slot][...] - m_new).astype(BF16)
        m_ref[...] = m_new

    def weighted(blk, slot):
        acc_ref[...] = al_ref[slot][...] * acc_ref[...] + _dot(v_ref[0, 0, blk], p_ref[slot][...])

    @pl.when(qi < nqc)
    def _ctx():
        for blk in range(nkc):
            scores(blk, 0, masked=True)
            probs(0)
            weighted(blk, 0)

    @pl.when(qi >= nqc)
    def _lat():
        def steady(t, slot):
            scores(t + 2, slot)
            weighted(t, slot)
            probs(1 - slot)

        scores(0, 0)
        scores(1, 1)
        probs(0)
        npairs = (nk - 2) // 2

        def pair(i, carry):
            steady(2 * i, 0)
            steady(2 * i + 1, 1)
            return carry

        lax.fori_loop(0, npairs, pair, 0)
        if (nk - 2) % 2:
            steady(2 * npairs, 0)
        weighted(nk - 2, (nk - 2) % 2)
        probs((nk - 1) % 2)
        weighted(nk - 1, (nk - 1) % 2)

    _attn_finish(acc_ref, o_ref, tq)


def _attention(qt, kn, vt, lc, tq, bounded):
    b, _, _, t = qt.shape
    nkv, tkv = vt.shape[2], vt.shape[4]
    tk, nk = tkv, nkv
    nq = t // tq
    nqc = lc // tq
    nkc = -(-lc // tk)
    wide = ATT_GROUP * tq
    qs_scratch = pltpu.VMEM((ATT_KV_HEADS * HEAD_DIM, wide), BF16)
    acc_scratch = pltpu.VMEM((HEAD_DIM + 16, wide), F32)
    if bounded:
        body, scratch = _attn_bounded_kernel, [qs_scratch, acc_scratch]
    else:
        assert nk >= 2
        body = _attn_kernel
        scratch = [qs_scratch,
                   pltpu.VMEM((tk, wide), F32), pltpu.VMEM((tk, wide), F32),
                   pltpu.VMEM((1, wide), F32), pltpu.VMEM((1, wide), F32),
                   pltpu.VMEM((tk, wide), BF16), pltpu.VMEM((tk, wide), BF16),
                   pltpu.VMEM((1, wide), F32), pltpu.VMEM((1, wide), F32),
                   pltpu.VMEM((1, wide), F32),
                   acc_scratch]
    return pl.pallas_call(
        functools.partial(body, lc=lc, tq=tq, tk=tk, nqc=nqc, nkc=nkc, nk=nk),
        grid=(b, ATT_KV_HEADS, nq),
        in_specs=[
            pl.BlockSpec((1, ATT_GROUP, HEAD_DIM, tq), lambda bi, g, qi: (bi, g, 0, qi)),
            pl.BlockSpec((1, t, ATT_KV_HEADS * HEAD_DIM), lambda bi, g, qi: (bi, 0, 0)),
            pl.BlockSpec((1, 1, nkv, HEAD_DIM + 16, tkv), lambda bi, g, qi: (bi, g, 0, 0, 0)),
        ],
        out_specs=pl.BlockSpec((1, ATT_GROUP * HEAD_DIM, tq), lambda bi, g, qi: (bi, g, qi)),
        out_shape=jax.ShapeDtypeStruct((b, ATT_HEADS * HEAD_DIM, t), BF16),
        scratch_shapes=scratch,
        name="attn_bounded" if bounded else "attn_online",
        compiler_params=pltpu.CompilerParams(
            dimension_semantics=("parallel", "parallel", "arbitrary"),
            vmem_limit_bytes=V7X_VMEM_LIMIT),
    )(qt, kn, vt)


def _hgrn_consts():
    c = HG_CHUNK
    t = np.arange(c)
    tt, rr = t[:, None], t[None, :]
    mats, masks = [], []
    for rev in (False, True):
        mats.append(np.concatenate([(rr >= tt) if rev else (rr <= tt), np.ones((8, c), bool)], axis=0))
        pair_masks = []
        for bsz in HG_LEVELS:
            blk, pos = t // (2 * bsz), t % (2 * bsz)
            qv = (pos < bsz) if rev else (pos >= bsz)
            pair_masks.append((blk[:, None] == blk[None, :]) & qv[:, None] & ~qv[None, :])
        for wsz in HG_WITHIN:
            same = (tt // wsz) == (rr // wsz)
            pair_masks.append(same & ((rr >= tt) if rev else (rr <= tt)))
        masks.append(np.stack([np.tile(m, (HG_HEADS, 1)) for m in pair_masks]))
    return jnp.asarray(np.stack(mats), BF16), jnp.asarray(np.stack(masks), F32)


def _level_reference(cum, bsz, rev):
    c, hw = cum.shape
    i = bsz if rev else bsz - 1
    if 2 * bsz >= 8:
        c3 = cum.reshape(c // (2 * bsz), 2 * bsz, hw)
        return jnp.broadcast_to(c3[:, i:i + 1, :], c3.shape).reshape(c, hw)
    c3 = cum.reshape(c // 8, 8, hw)
    sub = lax.broadcasted_iota(jnp.int32, (1, 8, 1), 1) // (2 * bsz)
    ref = jnp.broadcast_to(c3[:, i:i + 1, :], c3.shape)
    for gidx in range(1, 8 // (2 * bsz)):
        r = gidx * 2 * bsz + i
        ref = jnp.where(sub == gidx, jnp.broadcast_to(c3[:, r:r + 1, :], c3.shape), ref)
    return ref.reshape(c, hw)


def _block_entry_reference(cum, wsz, rev):
    c, hw = cum.shape
    nb = c // wsz
    if nb == 1:
        return jnp.zeros_like(cum)
    c3 = cum.reshape(nb, wsz, hw)
    zero = jnp.zeros((1, 1, hw), F32)
    if rev:
        edge = jnp.concatenate([c3[1:, 0:1, :], zero], axis=0)
    else:
        edge = jnp.concatenate([zero, c3[:-1, wsz - 1:wsz, :]], axis=0)
    return jnp.broadcast_to(edge, c3.shape).reshape(c, hw)


def _hgrn_prefix(d, r0, lf_ref, cm_ref):
    c = HG_CHUNK
    lf = lf_ref[0, pl.ds(r0, c), :]
    cm = cm_ref[d]
    hi, mid, lo = _split3(lf)
    e = _dot(cm, hi) + _dot(cm, mid) + _dot(cm, lo)
    return e[0:c], e[c:c + 1]


def _hgrn_pairs(d, r0, kk_ref, q_ref, am_ref, cum, tot, hw, within):
    c = HG_CHUNK
    dk = hw // HG_HEADS
    rev = d == 1
    rows = pl.ds(r0, c)
    q = q_ref[0, rows, :]
    k = kk_ref[0, rows, :]
    qdec = q * jnp.exp(cum)
    kdec = k * jnp.exp(tot - cum)
    ds = jnp.exp(tot)

    lane_head = lax.broadcasted_iota(jnp.int32, (1, hw), 1) // dk
    heads = [lane_head == hh for hh in range(HG_HEADS)]
    tpos = lax.broadcasted_iota(jnp.int32, (c, 1), 0)

    def pair_product(qd, kd, mask_idx):
        q4 = jnp.concatenate([jnp.where(hm, qd, 0.0) for hm in heads], axis=0).astype(BF16)
        return am_ref[d, mask_idx] * _dot_nt(q4, kd.astype(BF16))

    att = jnp.zeros((HG_HEADS * c, c), F32)
    for lv, bsz in enumerate(HG_LEVELS):
        if bsz < within:
            continue
        pos = tpos % (2 * bsz)
        is_q = (pos < bsz) if rev else (pos >= bsz)
        diff = cum - _level_reference(cum, bsz, rev)
        w = jnp.exp(jnp.where(is_q, diff, -diff))
        att = att + pair_product(jnp.where(is_q, q * w, 0.0), jnp.where(is_q, 0.0, k * w), lv)
    widx = len(HG_LEVELS) + HG_WITHIN.index(within)
    if within == 1:
        att = att + pair_product(q, k, widx)
    else:
        diff = cum - _block_entry_reference(cum, within, rev)
        att = att + pair_product(q * jnp.exp(diff), k * jnp.exp(-diff), widx)
    return att.astype(BF16), qdec.astype(BF16), kdec.astype(BF16), ds


def _hgrn_apply(r0, v_ref, att, kdec, hw):
    c = HG_CHUNK
    dk = hw // HG_HEADS
    v = v_ref[0, pl.ds(r0, c), :]
    lane_head = lax.broadcasted_iota(jnp.int32, (1, hw), 1) // dk
    ov = _dot(att, v)
    o = jnp.zeros((c, hw), F32)
    for hh in range(HG_HEADS):
        o = o + jnp.where(lane_head == hh, ov[hh * c:(hh + 1) * c], 0.0)
    u = _dot_tn(v, kdec)
    rh = lax.broadcasted_iota(jnp.int32, (hw, 1), 0) // dk
    return o, jnp.where(rh == lane_head, u, 0.0)


def _hgrn_kernel(kf_ref, lff_ref, vf_ref, qf_ref, kb_ref, lfb_ref, vb_ref, qb_ref,
                 cm_ref, am_ref, of_ref, ob_ref, sf_ref, sb_ref, *, nchunk, hw, within):
    @pl.when(pl.program_id(1) == 0)
    def _init():
        sf_ref[...] = jnp.zeros(sf_ref.shape, F32)
        sb_ref[...] = jnp.zeros(sb_ref.shape, F32)

    jobs = []
    for jj in range(nchunk):
        jobs.append((0, jj * HG_CHUNK, kf_ref, lff_ref, vf_ref, qf_ref, of_ref, sf_ref))
        jobs.append((1, (nchunk - 1 - jj) * HG_CHUNK, kb_ref, lfb_ref, vb_ref, qb_ref, ob_ref, sb_ref))
    pre = [_hgrn_prefix(d, r0, lf_ref, cm_ref) for d, r0, _, lf_ref, _, _, _, _ in jobs]
    pairs = [_hgrn_pairs(d, r0, kk_ref, q_ref, am_ref, cum, tot, hw, within)
             for (d, r0, kk_ref, _, _, q_ref, _, _), (cum, tot) in zip(jobs, pre)]
    applied = [_hgrn_apply(r0, v_ref, att, kdec, hw)
               for (_, r0, _, _, v_ref, _, _, _), (att, _, kdec, _) in zip(jobs, pairs)]
    for (_, r0, _, _, _, _, o_ref, s_ref), (_, qdec, _, ds), (o, u) in zip(jobs, pairs, applied):
        st = s_ref[...]
        o_ref[0, pl.ds(r0, HG_CHUNK), :] = o + _dot_nt(qdec, st.astype(BF16))
        s_ref[...] = st * ds + u


def _hgrn(kf, lff, kb, lfb, v, hq, hc, lc, nchunk, within):
    b, t, hw = kf.shape
    tb = nchunk * HG_CHUNK
    nblk, ncb = t // tb, lc // tb
    cm, am = hc
    fwd = lambda bi, p: (bi, p, 0)
    bwd = lambda bi, p: (bi, jnp.where(p < ncb, ncb - 1 - p, nblk - 1 - (p - ncb)), 0)
    blk = lambda im: pl.BlockSpec((1, tb, hw), im)
    return pl.pallas_call(
        functools.partial(_hgrn_kernel, nchunk=nchunk, hw=hw, within=within),
        grid=(b, nblk),
        in_specs=[blk(fwd)] * 4 + [blk(bwd)] * 4 + [_resident(cm.shape), _resident(am.shape)],
        out_specs=[blk(fwd), blk(bwd)],
        out_shape=[jax.ShapeDtypeStruct((b, t, hw), F32)] * 2,
        scratch_shapes=[pltpu.VMEM((hw, hw), F32), pltpu.VMEM((hw, hw), F32)],
        name=f"hgrn_within{within}",
        compiler_params=pltpu.CompilerParams(
            dimension_semantics=("parallel", "arbitrary"),
            vmem_limit_bytes=V7X_VMEM_LIMIT),
    )(kf, lff, v, hq, kb, lfb, v, hq, cm, am)


def _outproj_kernel(x_ref, mb_ref, mc_ref, y_ref, at_ref, of_ref, ob_ref, g_ref, wf_ref, wo_ref,
                    gn_ref, o_ref, *, lc, tm, d, fw, aw, hw):
    i = pl.program_id(1)
    row = i * tm + lax.broadcasted_iota(jnp.int32, (tm, 1), 0)
    is_ctx = row < lc
    mb, mc = mb_ref[0, 0], mc_ref[0, 0]
    gate = jnp.where(is_ctx, mc[:, 2 * d:3 * d], mb[:, 2 * d:3 * d])

    fx = _dot(y_ref[0], wf_ref[...]).astype(BF16)
    mix = _dot_tn(at_ref[0], wo_ref[fw:fw + aw, :])
    mix = mix + _dot(fx, wo_ref[0:fw, :])

    o = of_ref[0] + ob_ref[0]
    dv = hw // HG_HEADS
    lane_head = lax.broadcasted_iota(jnp.int32, (1, hw), 1) // dv
    o2 = o * o
    ms = jnp.zeros_like(o)
    for hh in range(HG_HEADS):
        sel = lane_head == hh
        ssh = jnp.sum(jnp.where(sel, o2, 0.0), axis=-1, keepdims=True)
        ms = jnp.where(sel, ssh * (1.0 / dv), ms)
    gg = g_ref[0]
    rx = (o * lax.rsqrt(ms + EPS) * gn_ref[...]) * (gg * jax.nn.sigmoid(gg))
    mix = mix + _dot(rx.astype(BF16), wo_ref[fw + aw:fw + aw + hw, :])
    o_ref[0] = x_ref[0] + gate * mix


def _outproj(xs, mod4, l, nb, y, at, of, ob, g, wf, wo, gn, lc, tm):
    b, t, d = xs.shape
    n6 = mod4.shape[-1]
    fw, hw = y.shape[-1], of.shape[-1]
    aw = at.shape[1]
    tok = lambda w: pl.BlockSpec((1, tm, w), lambda bi, i: (bi, i, 0))
    return pl.pallas_call(
        functools.partial(_outproj_kernel, lc=lc, tm=tm, d=d, fw=fw, aw=aw, hw=hw),
        grid=(b, t // tm),
        in_specs=[tok(d),
                  pl.BlockSpec((1, 1, 1, n6), lambda bi, i: (l, bi, 0, 0)),
                  pl.BlockSpec((1, 1, 1, n6), lambda bi, i: (l, nb, 0, 0)),
                  tok(fw),
                  pl.BlockSpec((1, aw, tm), lambda bi, i: (bi, 0, i)),
                  tok(hw), tok(hw), tok(hw),
                  _resident(wf.shape), _resident(wo.shape), _resident(gn.shape)],
        out_specs=tok(d),
        out_shape=jax.ShapeDtypeStruct((b, t, d), F32),
        name="outproj",
        compiler_params=pltpu.CompilerParams(
            dimension_semantics=("parallel", "parallel"),
            vmem_limit_bytes=V7X_VMEM_LIMIT),
    )(xs, mod4, mod4, y, at, of, ob, g, wf, wo, gn)


def _ffn_kernel(x_ref, mb_ref, mc_ref, wg_ref, wu_ref, wd_ref, *rest, lc, tm, d, fchunk):
    o_ref = rest[-1]
    i = pl.program_id(1)
    x = x_ref[0]
    row = i * tm + lax.broadcasted_iota(jnp.int32, (tm, 1), 0)
    is_ctx = row < lc
    mb, mc = mb_ref[0, 0], mc_ref[0, 0]
    h = _modulated(x, mb, mc, is_ctx, d, 3, 4).astype(BF16)
    gate = jnp.where(is_ctx, mc[:, 5 * d:6 * d], mb[:, 5 * d:6 * d])
    dff = wg_ref.shape[1]
    def gate_up(c0):
        a = _dot(h, wg_ref[:, c0:c0 + fchunk])
        u = _dot(h, wu_ref[:, c0:c0 + fchunk])
        return (a * jax.nn.sigmoid(a) * u).astype(BF16)

    acc = jnp.zeros((tm, d), F32)
    starts = list(range(0, dff, fchunk))
    act = gate_up(starts[0])
    for ci, c0 in enumerate(starts):
        nxt = gate_up(starts[ci + 1]) if ci + 1 < len(starts) else None
        acc = acc + _dot(act, wd_ref[c0:c0 + fchunk, :])
        act = nxt
    y = x + gate * acc
    if len(rest) == 2:
        y = y * lax.rsqrt(jnp.mean(y * y, axis=-1, keepdims=True) + EPS) * rest[0][...]
    o_ref[0] = y


def _ffn(xs, mod4, l, nb, wg, wu, wd, lc, tm, final_gain=None):
    b, t, d = xs.shape
    n6 = mod4.shape[-1]
    dff = wg.shape[1]
    fchunk = _pick_tile(dff, 768)
    extra_in = [] if final_gain is None else [final_gain.reshape(1, d)]
    return pl.pallas_call(
        functools.partial(_ffn_kernel, lc=lc, tm=tm, d=d, fchunk=fchunk),
        grid=(b, t // tm),
        in_specs=[pl.BlockSpec((1, tm, d), lambda bi, i: (bi, i, 0)),
                  pl.BlockSpec((1, 1, 1, n6), lambda bi, i: (l, bi, 0, 0)),
                  pl.BlockSpec((1, 1, 1, n6), lambda bi, i: (l, nb, 0, 0)),
                  _resident(wg.shape), _resident(wu.shape), _resident(wd.shape)] +
                 [_resident(a.shape) for a in extra_in],
        out_specs=pl.BlockSpec((1, tm, d), lambda bi, i: (bi, i, 0)),
        out_shape=jax.ShapeDtypeStruct((b, t, d), F32),
        name="ffn" if final_gain is None else "ffn_final",
        compiler_params=pltpu.CompilerParams(
            dimension_semantics=("parallel", "parallel"),
            vmem_limit_bytes=V7X_VMEM_LIMIT),
    )(xs, mod4, mod4, wg, wu, wd, *extra_in)


def _rope_tables(lc, s):
    half = HEAD_DIM // 2
    n = jnp.arange(s)
    freqs = ROPE_THETA ** (-jnp.arange(0, half, 2, dtype=F32) / half)
    ang_r = (n // GRID_W).astype(F32)[:, None] * freqs
    ang_c = (n % GRID_W).astype(F32)[:, None] * freqs
    pad = lambda a, v: jnp.concatenate([jnp.full((lc, a.shape[1]), v, F32), a], axis=0)
    cr, sr = pad(jnp.cos(ang_r), 1.0), pad(jnp.sin(ang_r), 0.0)
    cc, sc = pad(jnp.cos(ang_c), 1.0), pad(jnp.sin(ang_c), 0.0)
    cos_h = jnp.concatenate([cr, cr, cc, cc], axis=1)
    sin_h = jnp.concatenate([-sr, sr, -sc, sc], axis=1)
    ckn = jnp.tile(cos_h, (1, ATT_KV_HEADS))
    skn = jnp.tile(sin_h, (1, ATT_KV_HEADS))
    cqt = jnp.concatenate([cr, cc], axis=1).T
    sqt = jnp.concatenate([sr, sc], axis=1).T
    return ckn, skn, cqt, sqt


def kernel(x, c, ctx, c_ctx, w_ada, b_ada, w_in, w_four, q_norm, k_norm, hg_lb_logits, hg_norm,
           w_out, w_gate, w_up, w_down, final_norm):
    b, s, d = x.shape
    lc = ctx.shape[1]
    t = lc + s
    depth = w_ada.shape[0]
    fw = w_four.shape[-1]
    hw = hg_lb_logits.shape[-1]
    qw = ATT_HEADS * HEAD_DIM
    kw = ATT_KV_HEADS * HEAD_DIM
    assert w_in.shape[-1] == fw + qw + 2 * kw + 5 * hw
    assert s % (DFT_N2 * 8) == 0 and lc % 128 == 0 and s % GRID_W == 0

    tq = 256 if (lc % 256 == 0 and t % 256 == 0) else 128
    tkv = tq
    tm = _pick_tile(t, 768, mult=tkv)
    nchunk = max(g for g in (4, 2, 1) if lc % (g * HG_CHUNK) == 0 and t % (g * HG_CHUNK) == 0)

    xs = jnp.concatenate([ctx, x], axis=1)
    r8 = -(-(b + 1) // 8) * 8
    cvec = jnp.concatenate([c, c_ctx[None], jnp.zeros((r8 - b - 1, d), F32)], axis=0)
    mod = _adaln(cvec, w_ada, b_ada)
    mod4 = mod.reshape(depth, r8, 1, mod.shape[-1])

    lb_sm = jax.nn.softmax(hg_lb_logits.astype(F32), axis=1)
    lb_all = jnp.cumsum(lb_sm, axis=1) - lb_sm[:, :1]

    ckn, skn, cqt, sqt = _rope_tables(lc, s)
    fc = _fourier_consts(lc, s)
    hc = _hgrn_consts()

    o_q, o_k, o_v, o_h = fw, fw + qw, fw + qw + kw, fw + qw + 2 * kw
    for l in range(depth):
        wl = w_in[l]
        wn = jnp.concatenate([wl[:, 0:fw], wl[:, o_k:o_v], wl[:, o_h:]], axis=1).astype(BF16)
        wt = jnp.concatenate([wl[:, o_q:o_k], wl[:, o_v:o_h]], axis=1).T.astype(BF16)
        gk = jnp.tile(k_norm[l], ATT_KV_HEADS)[None]
        gq = q_norm[l][:, None]
        gn = jnp.tile(hg_norm[l], HG_HEADS)[None]
        (fz, kn, kf, lff, kb, lfb, v, hq, g, qt, vt, gd) = _inproj(
            xs, mod4, l, b, wn, wt, gk, ckn, skn, gq, cqt, sqt,
            lb_all[0, l][None], lb_all[1, l][None], lc, tm, tkv)
        y = _fourier(fz, fc, lc)
        bound = (HEAD_DIM ** 0.5 * LOG2E * BOUND_SLACK) * jnp.max(jnp.abs(q_norm[l])) * jnp.max(jnp.abs(k_norm[l]))
        at = lax.cond(bound < MAX_SAFE_LOG2_SCORE,
                      functools.partial(_attention, lc=lc, tq=tq, bounded=True),
                      functools.partial(_attention, lc=lc, tq=tq, bounded=False),
                      qt, kn, vt)
        hg_scan = functools.partial(_hgrn, hc=hc, lc=lc, nchunk=nchunk, within=HG_WITHIN[0])
        for gi, wsz in enumerate(HG_WITHIN[1:]):
            hg_scan = functools.partial(
                lambda ok, big, small, *a: lax.cond(ok, big, small, *a),
                jnp.max(gd[:, :, gi, 0]) < HG_MAX_BLOCK_DECAY,
                functools.partial(_hgrn, hc=hc, lc=lc, nchunk=nchunk, within=wsz), hg_scan)
        of, ob = hg_scan(kf, lff, kb, lfb, v, hq)
        xs = _outproj(xs, mod4, l, b, y, at, of, ob, g, w_four[l].astype(BF16),
                      w_out[l].astype(BF16), gn, lc, tm)
        xs = _ffn(xs, mod4, l, b, w_gate[l].astype(BF16), w_up[l].astype(BF16), w_down[l].astype(BF16),
                  lc, tm, final_norm if l == depth - 1 else None)
    return xs[:, lc:]
```

```python
import functools

import numpy as np
import jax
import jax.numpy as jnp
from jax import lax
from jax.experimental import pallas as pl
from jax.experimental.pallas import tpu as pltpu

F32 = jnp.float32
BF16 = jnp.bfloat16

EPS = 1e-6
GRID_W = 64
ROPE_THETA = 10000.0
FOURIER_GD = 64
ATT_HEADS = 8
ATT_KV_HEADS = 2
ATT_GROUP = ATT_HEADS // ATT_KV_HEADS
HEAD_DIM = 64
HG_HEADS = 4
HG_CHUNK = 64
HG_LEVELS = (32, 16, 8, 4, 2, 1)
HG_WITHIN = (1, 16, 64)
HG_MAX_BLOCK_DECAY = 80.0
DFT_N2 = 128
DFT_ROW_PAD = 8
NEG_BIG = -1e30
LOG2E = 1.4426950408889634
BOUND_SLACK = 1.05
MAX_SAFE_LOG2_SCORE = 100.0
ATTN_BOUNDED_KEYS = 2816
V7X_VMEM_LIMIT = 56 * 1024 * 1024


def _dot(a, b):
    return jnp.dot(a, b, preferred_element_type=F32)


def _dot_nt(a, b):
    return lax.dot_general(a, b, (((1,), (1,)), ((), ())), preferred_element_type=F32)


def _dot_tn(a, b):
    return lax.dot_general(a, b, (((0,), (0,)), ((), ())), preferred_element_type=F32)


def _split3(x):
    hi = x.astype(BF16)
    r1 = x - hi.astype(F32)
    mid = r1.astype(BF16)
    lo = (r1 - mid.astype(F32)).astype(BF16)
    return hi, mid, lo


def _pick_tile(n, cap, mult=128):
    best = None
    for t in range(mult, min(n, cap) + 1, mult):
        if n % t == 0:
            best = t
    if best is None:
        raise ValueError(f"no tile for {n}")
    return best


def _resident(shape):
    nd = len(shape)
    return pl.BlockSpec(shape, lambda *_: (0,) * nd, pipeline_mode=pl.Buffered(1))


def _adaln_kernel(c_ref, w_ref, b_ref, o_ref):
    cv = c_ref[...]
    a = cv * jax.nn.sigmoid(cv)
    w = w_ref[0]
    a_hi = a.astype(BF16)
    a_lo = (a - a_hi.astype(F32)).astype(BF16)
    w_hi = w.astype(BF16)
    w_lo = (w - w_hi.astype(F32)).astype(BF16)
    acc = _dot(a_hi, w_hi) + _dot(a_lo, w_hi) + _dot(a_hi, w_lo)
    o_ref[0] = acc + b_ref[0]


def _adaln(cvec, w_ada, b_ada):
    depth, d, n6 = w_ada.shape
    r8 = cvec.shape[0]
    tn = _pick_tile(n6, 1536)
    return pl.pallas_call(
        _adaln_kernel,
        grid=(depth, n6 // tn),
        in_specs=[
            pl.BlockSpec((r8, d), lambda l, j: (0, 0)),
            pl.BlockSpec((1, d, tn), lambda l, j: (l, 0, j)),
            pl.BlockSpec((1, 1, tn), lambda l, j: (l, 0, j)),
        ],
        out_specs=pl.BlockSpec((1, r8, tn), lambda l, j: (l, 0, j)),
        out_shape=jax.ShapeDtypeStruct((depth, r8, n6), F32),
        name="adaln",
        compiler_params=pltpu.CompilerParams(
            dimension_semantics=("parallel", "parallel"),
            vmem_limit_bytes=V7X_VMEM_LIMIT),
    )(cvec, w_ada, b_ada.reshape(depth, 1, n6))


def _modulated(x, mb, mc, is_ctx, d, k_shift, k_scale):
    r = lax.rsqrt(jnp.mean(x * x, axis=-1, keepdims=True) + EPS)
    sh = jnp.where(is_ctx, mc[:, k_shift * d:(k_shift + 1) * d], mb[:, k_shift * d:(k_shift + 1) * d])
    sc = jnp.where(is_ctx, mc[:, k_scale * d:(k_scale + 1) * d], mb[:, k_scale * d:(k_scale + 1) * d])
    return (x * r) * (1.0 + sc) + sh


def _inproj_kernel(x_ref, mb_ref, mc_ref, wn_ref, wt_ref, gk_ref, ckn_ref, skn_ref,
                   gq_ref, cqt_ref, sqt_ref, lbf_ref, lbb_ref,
                   fz_ref, k_ref, kf_ref, lff_ref, kb_ref, lfb_ref, v_ref, hq_ref, g_ref,
                   qt_ref, vt_ref, gd_ref, *, lc, tm, tkv, d, fw, hw):
    i = pl.program_id(1)
    x = x_ref[0]
    row = i * tm + lax.broadcasted_iota(jnp.int32, (tm, 1), 0)
    is_ctx = row < lc
    h = _modulated(x, mb_ref[0, 0], mc_ref[0, 0], is_ctx, d, 0, 1).astype(BF16)

    un = _dot(h, wn_ref[...])
    fz_ref[0] = un[:, 0:fw]

    o = fw
    kw = ATT_KV_HEADS * HEAD_DIM
    uk = un[:, o:o + kw]
    lane = lax.broadcasted_iota(jnp.int32, (1, kw), 1)
    uk2 = uk * uk
    ms = jnp.zeros_like(uk)
    for hh in range(ATT_KV_HEADS):
        sel = (lane // HEAD_DIM) == hh
        ssh = jnp.sum(jnp.where(sel, uk2, 0.0), axis=-1, keepdims=True)
        ms = jnp.where(sel, ssh * (1.0 / HEAD_DIM), ms)
    kn = uk * lax.rsqrt(ms + EPS) * gk_ref[...]
    first = (lane % 32) < 16
    partner = jnp.where(first, pltpu.roll(kn, kw - 16, axis=1), pltpu.roll(kn, 16, axis=1))
    k_ref[0] = (kn * ckn_ref[...] + partner * skn_ref[...]).astype(BF16)
    o += kw

    block_decay = [jnp.zeros((1, 1), F32) for _ in HG_WITHIN[1:]]
    for lb_ref, kk_ref, lf_ref in ((lbf_ref, kf_ref, lff_ref), (lbb_ref, kb_ref, lfb_ref)):
        z = un[:, o:o + hw]
        lb = lb_ref[...]
        kk_ref[0] = (1.0 - lb) * jax.nn.sigmoid(-z)
        lf = jnp.log(lb + (1.0 - lb) * jax.nn.sigmoid(z))
        lf_ref[0] = lf
        for gi, wsz in enumerate(HG_WITHIN[1:]):
            tot = -jnp.sum(lf.reshape(tm // wsz, wsz, hw), axis=1)
            top = jnp.max(jnp.max(tot, axis=0, keepdims=True), axis=1, keepdims=True)
            block_decay[gi] = jnp.maximum(block_decay[gi], top)
        o += hw
    grow = lax.broadcasted_iota(jnp.int32, (8, 128), 0)
    gd = jnp.zeros((8, 128), F32)
    for gi, dec in enumerate(block_decay):
        gd = jnp.where(grow == gi, dec, gd)
    gd_ref[0, 0] = gd
    v_ref[0] = un[:, o:o + hw].astype(BF16)
    hq_ref[0] = un[:, o + hw:o + 2 * hw]
    g_ref[0] = un[:, o + 2 * hw:o + 3 * hw]

    ut = _dot_nt(wt_ref[...], h)
    qw = ATT_HEADS * HEAD_DIM
    uq = ut[0:qw].reshape(ATT_HEADS, HEAD_DIM, tm)
    msq = jnp.mean(uq * uq, axis=1, keepdims=True)
    qn = uq * lax.rsqrt(msq + EPS) * gq_ref[...][None]
    cr, sr = cqt_ref[0:16][None], sqt_ref[0:16][None]
    cc, sc = cqt_ref[16:32][None], sqt_ref[16:32][None]
    x1r, x2r, x1c, x2c = qn[:, 0:16], qn[:, 16:32], qn[:, 32:48], qn[:, 48:64]
    qr = jnp.concatenate([x1r * cr - x2r * sr, x1r * sr + x2r * cr,
                          x1c * cc - x2c * sc, x1c * sc + x2c * cc], axis=1)
    qt_ref[0] = (qr * (HEAD_DIM ** -0.5 * LOG2E)).astype(BF16)

    uv = ut[qw:qw + kw].reshape(ATT_KV_HEADS, HEAD_DIM, tm)
    ones_row = lax.broadcasted_iota(jnp.int32, (ATT_KV_HEADS, 16, tm), 1) == 0
    aug = jnp.where(ones_row, 1.0, 0.0).astype(F32)
    vaug = jnp.concatenate([uv, aug], axis=1).astype(BF16)
    for cb in range(tm // tkv):
        vt_ref[0, :, cb] = vaug[:, :, cb * tkv:(cb + 1) * tkv]


def _inproj(xs, mod4, l, nb, wn, wt, gk, ckn, skn, gq, cqt, sqt, lbf, lbb, lc, tm, tkv):
    b, t, d = xs.shape
    n6 = mod4.shape[-1]
    fw = FOURIER_GD * 4
    hw = lbf.shape[-1]
    kw = ATT_KV_HEADS * HEAD_DIM
    tok = lambda w, dt: jax.ShapeDtypeStruct((b, t, w), dt)
    tok_spec = lambda w: pl.BlockSpec((1, tm, w), lambda bi, i: (bi, i, 0))
    out_shape = [tok(fw, F32), tok(kw, BF16), tok(hw, F32), tok(hw, F32), tok(hw, F32), tok(hw, F32),
                 tok(hw, BF16), tok(hw, F32), tok(hw, F32),
                 jax.ShapeDtypeStruct((b, ATT_HEADS, HEAD_DIM, t), BF16),
                 jax.ShapeDtypeStruct((b, ATT_KV_HEADS, t // tkv, HEAD_DIM + 16, tkv), BF16),
                 jax.ShapeDtypeStruct((b, t // tm, 8, 128), F32)]
    out_specs = [tok_spec(fw), tok_spec(kw)] + [tok_spec(hw)] * 7 + [
        pl.BlockSpec((1, ATT_HEADS, HEAD_DIM, tm), lambda bi, i: (bi, 0, 0, i)),
        pl.BlockSpec((1, ATT_KV_HEADS, tm // tkv, HEAD_DIM + 16, tkv), lambda bi, i: (bi, 0, i, 0, 0)),
        pl.BlockSpec((1, 1, 8, 128), lambda bi, i: (bi, i, 0, 0))]
    in_specs = [
        pl.BlockSpec((1, tm, d), lambda bi, i: (bi, i, 0)),
        pl.BlockSpec((1, 1, 1, n6), lambda bi, i: (l, bi, 0, 0)),
        pl.BlockSpec((1, 1, 1, n6), lambda bi, i: (l, nb, 0, 0)),
        _resident(wn.shape), _resident(wt.shape), _resident(gk.shape),
        pl.BlockSpec((tm, kw), lambda bi, i: (i, 0)),
        pl.BlockSpec((tm, kw), lambda bi, i: (i, 0)),
        _resident(gq.shape),
        pl.BlockSpec((32, tm), lambda bi, i: (0, i)),
        pl.BlockSpec((32, tm), lambda bi, i: (0, i)),
        _resident(lbf.shape), _resident(lbb.shape),
    ]
    return pl.pallas_call(
        functools.partial(_inproj_kernel, lc=lc, tm=tm, tkv=tkv, d=d, fw=fw, hw=hw),
        grid=(b, t // tm),
        in_specs=in_specs, out_specs=out_specs, out_shape=out_shape,
        name="inproj",
        compiler_params=pltpu.CompilerParams(
            dimension_semantics=("parallel", "parallel"),
            vmem_limit_bytes=V7X_VMEM_LIMIT),
    )(xs, mod4, mod4, wn, wt, gk, ckn, skn, gq, cqt, sqt, lbf, lbb)


def _fourier_kernel(z_ref, f1_ref, twc_ref, tws_ref, c2_ref, s2_ref, cs_ref, cl_ref,
                    y_ref, a1_ref, p_ref, *, lc, n1, scale_c, scale_x):
    n2 = DFT_N2
    cs = cs_ref[...]
    pa, pp = 2 * n1 + DFT_ROW_PAD, n2 + DFT_ROW_PAD

    zc = z_ref[0, 0:lc, :].astype(BF16)
    pc = _dot(cl_ref[...], zc)
    pcat = jnp.concatenate([pc[0:lc], pc[lc:2 * lc]], axis=1).astype(BF16)
    y_ref[0, 0:lc, :] = (_dot(pcat, cs) * scale_c).astype(y_ref.dtype)

    f1 = f1_ref[...]

    def stage1(j, carry):
        xj = z_ref[0, pl.ds(lc + j, n1, stride=n2), :].astype(BF16)
        a1_ref[pl.ds(pl.multiple_of(j * pa, 8), 2 * n1), :] = _dot(f1, xj)
        return carry

    lax.fori_loop(0, n2, stage1, 0, unroll=8)

    c2 = c2_ref[...]
    s2 = s2_ref[...]

    group = 8 if n1 % 8 == 0 else 1

    def stage2(gidx, carry):
        k1s = [gidx * group + u for u in range(group)]
        pris = []
        for k1 in k1s:
            ar = a1_ref[pl.ds(k1, n2, stride=pa), :]
            ai = a1_ref[pl.ds(n1 + k1, n2, stride=pa), :]
            twc = twc_ref[pl.ds(k1, 1), :]
            tws = tws_ref[pl.ds(k1, 1), :]
            gr = c2 * twc - s2 * tws
            gi = -(s2 * twc + c2 * tws)
            gm = jnp.concatenate([jnp.concatenate([gr, -gi], axis=1),
                                  jnp.concatenate([gi, gr], axis=1)], axis=0).astype(BF16)
            rhs = jnp.concatenate([ar, ai], axis=0).astype(BF16)
            pris.append(_dot(gm, rhs))
        for k1, pri in zip(k1s, pris):
            pcat2 = jnp.concatenate([pri[0:n2], pri[n2:2 * n2]], axis=1).astype(BF16)
            p_ref[pl.ds(pl.multiple_of(k1 * pp, 8), n2), :] = _dot(pcat2, cs) * scale_x
        return carry

    lax.fori_loop(0, n1 // group, stage2, 0)

    def stage3(k2, carry):
        blk = p_ref[pl.ds(k2, n1, stride=pp), :]
        y_ref[0, pl.ds(pl.multiple_of(lc + k2 * n1, n1), n1), :] = blk.astype(y_ref.dtype)
        return carry

    lax.fori_loop(0, n2, stage3, 0, unroll=8)


def _fourier_consts(lc, s):
    n2 = DFT_N2
    n1 = s // n2
    f64 = np.float64
    k = np.arange(n1, dtype=f64)
    a1 = 2 * np.pi * np.outer(k, k) / n1
    f1 = np.concatenate([np.cos(a1), -np.sin(a1)], axis=0)
    at = 2 * np.pi * np.outer(np.arange(n1, dtype=f64), np.arange(n2, dtype=f64)) / s
    k2 = np.arange(n2, dtype=f64)
    a2 = 2 * np.pi * np.outer(k2, k2) / n2
    w = 2 * FOURIER_GD
    ch = np.arange(w)
    same = (ch[:, None] // FOURIER_GD) == (ch[None, :] // FOURIER_GD)
    ag = 2 * np.pi * np.outer(ch % FOURIER_GD, ch % FOURIER_GD) / FOURIER_GD
    cs = np.concatenate([np.where(same, np.cos(ag), 0.0), np.where(same, np.sin(ag), 0.0)], axis=0)
    kc = np.arange(lc, dtype=f64)
    al = 2 * np.pi * np.outer(kc, kc) / lc
    cl = np.concatenate([np.cos(al), -np.sin(al)], axis=0)
    return dict(
        f1=jnp.asarray(f1, BF16), twc=jnp.asarray(np.cos(at), F32), tws=jnp.asarray(np.sin(at), F32),
        c2=jnp.asarray(np.cos(a2), F32), s2=jnp.asarray(np.sin(a2), F32),
        cs=jnp.asarray(cs, BF16), cl=jnp.asarray(cl, BF16), n1=n1)


def _fourier(fz, fc, lc):
    b, t, fw = fz.shape
    s = t - lc
    n1 = fc["n1"]
    w = 2 * FOURIER_GD
    consts = [fc[k] for k in ("f1", "twc", "tws", "c2", "s2", "cs", "cl")]
    return pl.pallas_call(
        functools.partial(_fourier_kernel, lc=lc, n1=n1,
                          scale_c=float((lc * FOURIER_GD) ** -0.5),
                          scale_x=float((s * FOURIER_GD) ** -0.5)),
        grid=(b, fw // w),
        in_specs=[pl.BlockSpec((1, t, w), lambda bi, hi: (bi, 0, hi))] +
                 [_resident(c.shape) for c in consts],
        out_specs=pl.BlockSpec((1, t, w), lambda bi, hi: (bi, 0, hi)),
        out_shape=jax.ShapeDtypeStruct((b, t, fw), BF16),
        scratch_shapes=[pltpu.VMEM((DFT_N2 * (2 * n1 + DFT_ROW_PAD), w), F32),
                        pltpu.VMEM((n1 * (DFT_N2 + DFT_ROW_PAD), w), F32)],
        name="fourier",
        compiler_params=pltpu.CompilerParams(
            dimension_semantics=("parallel", "parallel"),
            vmem_limit_bytes=V7X_VMEM_LIMIT),
    )(fz, *consts)


def _place_queries(q_ref, qs_ref):
    g = pl.program_id(1)
    qcat = jnp.concatenate([q_ref[0, j] for j in range(ATT_GROUP)], axis=1)
    zero = jnp.zeros_like(qcat)
    for gg in range(ATT_KV_HEADS):
        @pl.when(g == gg)
        def _place():
            for hh in range(ATT_KV_HEADS):
                qs_ref[hh * HEAD_DIM:(hh + 1) * HEAD_DIM, :] = qcat if hh == gg else zero


def _score_block(k_ref, qs_ref, blk, tk, lc, masked):
    kb = k_ref[0, pl.ds(pl.multiple_of(blk * tk, tk), tk), :]
    s = _dot(kb, qs_ref[...])
    if masked:
        kidx = blk * tk + lax.broadcasted_iota(jnp.int32, (tk, 1), 0)
        s = jnp.where(kidx < lc, s, NEG_BIG)
    return s


def _attn_finish(acc_ref, o_ref, tq):
    acc = acc_ref[...]
    o = acc[0:HEAD_DIM] / acc[HEAD_DIM:HEAD_DIM + 1]
    o_ref[0] = jnp.concatenate([o[:, j * tq:(j + 1) * tq] for j in range(ATT_GROUP)],
                               axis=0).astype(o_ref.dtype)


def _attn_bounded_kernel(q_ref, k_ref, v_ref, o_ref, qs_ref, p0_ref, p1_ref, acc_ref,
                         *, lc, tq, tk, nqc, nkc, nk):
    qi = pl.program_id(2)
    _place_queries(q_ref, qs_ref)
    acc_ref[...] = jnp.zeros(acc_ref.shape, F32)

    p_ref = (p0_ref, p1_ref)
    tkv = v_ref.shape[-1]
    nsub = tk // tkv

    def probs(blk, slot):
        p_ref[slot][...] = jnp.exp2(_score_block(k_ref, qs_ref, blk, tk, lc, False)).astype(BF16)

    def weighted(blk, slot):
        pv = _dot(v_ref[0, 0, blk * nsub], p_ref[slot][0:tkv, :])
        for c in range(1, nsub):
            pv = pv + _dot(v_ref[0, 0, blk * nsub + c], p_ref[slot][c * tkv:(c + 1) * tkv, :])
        acc_ref[...] += pv

    @pl.when(qi < nqc)
    def _ctx():
        nsc = -(-lc // tkv)
        s = _dot(k_ref[0, 0:nsc * tkv, :], qs_ref[...])
        if lc % tkv:
            s = jnp.where(lax.broadcasted_iota(jnp.int32, (nsc * tkv, 1), 0) < lc, s, NEG_BIG)
        p = jnp.exp2(s).astype(BF16)
        pv = _dot(v_ref[0, 0, 0], p[0:tkv])
        for c in range(1, nsc):
            pv = pv + _dot(v_ref[0, 0, c], p[c * tkv:(c + 1) * tkv])
        acc_ref[...] += pv

    @pl.when(qi >= nqc)
    def _lat():
        def steady(t, slot):
            probs(t + 1, 1 - slot)
            weighted(t, slot)

        probs(0, 0)
        npairs = (nk - 1) // 2

        def pair(i, carry):
            steady(2 * i, 0)
            steady(2 * i + 1, 1)
            return carry

        lax.fori_loop(0, npairs, pair, 0)
        if (nk - 1) % 2:
            steady(nk - 2, (nk - 2) % 2)
        weighted(nk - 1, (nk - 1) % 2)

    _attn_finish(acc_ref, o_ref, tq)


def _attn_kernel(q_ref, k_ref, v_ref, o_ref, qs_ref, s0_ref, s1_ref, mb0_ref, mb1_ref, p0_ref, p1_ref,
                 al0_ref, al1_ref, m_ref, acc_ref, *, lc, tq, tk, nqc, nkc, nk):
    s_ref, mb_ref = (s0_ref, s1_ref), (mb0_ref, mb1_ref)
    p_ref, al_ref = (p0_ref, p1_ref), (al0_ref, al1_ref)
    qi = pl.program_id(2)

    _place_queries(q_ref, qs_ref)
    m_ref[...] = jnp.full(m_ref.shape, NEG_BIG, F32)
    acc_ref[...] = jnp.zeros(acc_ref.shape, F32)

    def scores(blk, slot, masked=False):
        s = _score_block(k_ref, qs_ref, blk, tk, lc, masked)
        s_ref[slot][...] = s
        mb_ref[slot][...] = jnp.max(s, axis=0, keepdims=True)

    def probs(slot):
        m_prev = m_ref[...]
        m_new = jnp.maximum(m_prev, mb_ref[slot][...])
        al_ref[slot][...] = jnp.exp2(m_prev - m_new)
        p_ref[slot][...] = jnp.exp2(s_ref[slot][...] - m_new).astype(BF16)
        m_ref[...] = m_new

    def weighted(blk, slot):
        acc_ref[...] = al_ref[slot][...] * acc_ref[...] + _dot(v_ref[0, 0, blk], p_ref[slot][...])

    @pl.when(qi < nqc)
    def _ctx():
        for blk in range(nkc):
            scores(blk, 0, masked=True)
            probs(0)
            weighted(blk, 0)

    @pl.when(qi >= nqc)
    def _lat():
        def steady(t, slot):
            scores(t + 2, slot)
            weighted(t, slot)
            probs(1 - slot)

        scores(0, 0)
        scores(1, 1)
        probs(0)
        npairs = (nk - 2) // 2

        def pair(i, carry):
            steady(2 * i, 0)
            steady(2 * i + 1, 1)
            return carry

        lax.fori_loop(0, npairs, pair, 0)
        if (nk - 2) % 2:
            steady(2 * npairs, 0)
        weighted(nk - 2, (nk - 2) % 2)
        probs((nk - 1) % 2)
        weighted(nk - 1, (nk - 1) % 2)

    _attn_finish(acc_ref, o_ref, tq)


def _attention(qt, kn, vt, lc, tq, bounded):
    b, _, _, t = qt.shape
    nkv, tkv = vt.shape[2], vt.shape[4]
    tk = _pick_tile(t, ATTN_BOUNDED_KEYS, mult=tkv) if bounded else tkv
    nk = t // tk
    nq = t // tq
    nqc = lc // tq
    nkc = -(-lc // tk)
    wide = ATT_GROUP * tq
    qs_scratch = pltpu.VMEM((ATT_KV_HEADS * HEAD_DIM, wide), BF16)
    acc_scratch = pltpu.VMEM((HEAD_DIM + 16, wide), F32)
    if bounded:
        body = _attn_bounded_kernel
        scratch = [qs_scratch, pltpu.VMEM((tk, wide), BF16), pltpu.VMEM((tk, wide), BF16), acc_scratch]
    else:
        assert nk >= 2
        body = _attn_kernel
        scratch = [qs_scratch,
                   pltpu.VMEM((tk, wide), F32), pltpu.VMEM((tk, wide), F32),
                   pltpu.VMEM((1, wide), F32), pltpu.VMEM((1, wide), F32),
                   pltpu.VMEM((tk, wide), BF16), pltpu.VMEM((tk, wide), BF16),
                   pltpu.VMEM((1, wide), F32), pltpu.VMEM((1, wide), F32),
                   pltpu.VMEM((1, wide), F32),
                   acc_scratch]
    return pl.pallas_call(
        functools.partial(body, lc=lc, tq=tq, tk=tk, nqc=nqc, nkc=nkc, nk=nk),
        grid=(b, ATT_KV_HEADS, nq),
        in_specs=[
            pl.BlockSpec((1, ATT_GROUP, HEAD_DIM, tq), lambda bi, g, qi: (bi, g, 0, qi)),
            pl.BlockSpec((1, t, ATT_KV_HEADS * HEAD_DIM), lambda bi, g, qi: (bi, 0, 0)),
            pl.BlockSpec((1, 1, nkv, HEAD_DIM + 16, tkv), lambda bi, g, qi: (bi, g, 0, 0, 0)),
        ],
        out_specs=pl.BlockSpec((1, ATT_GROUP * HEAD_DIM, tq), lambda bi, g, qi: (bi, g, qi)),
        out_shape=jax.ShapeDtypeStruct((b, ATT_HEADS * HEAD_DIM, t), BF16),
        scratch_shapes=scratch,
        name="attn_bounded" if bounded else "attn_online",
        compiler_params=pltpu.CompilerParams(
            dimension_semantics=("parallel", "parallel", "arbitrary"),
            vmem_limit_bytes=V7X_VMEM_LIMIT),
    )(qt, kn, vt)


def _hgrn_consts():
    c = HG_CHUNK
    t = np.arange(c)
    tt, rr = t[:, None], t[None, :]
    mats, masks = [], []
    for rev in (False, True):
        mats.append(np.concatenate([(rr >= tt) if rev else (rr <= tt), np.ones((8, c), bool)], axis=0))
        pair_masks = []
        for bsz in HG_LEVELS:
            blk, pos = t // (2 * bsz), t % (2 * bsz)
            qv = (pos < bsz) if rev else (pos >= bsz)
            pair_masks.append((blk[:, None] == blk[None, :]) & qv[:, None] & ~qv[None, :])
        for wsz in HG_WITHIN:
            same = (tt // wsz) == (rr // wsz)
            pair_masks.append(same & ((rr >= tt) if rev else (rr <= tt)))
        masks.append(np.stack([np.tile(m, (HG_HEADS, 1)) for m in pair_masks]))
    return jnp.asarray(np.stack(mats), BF16), jnp.asarray(np.stack(masks), F32)


def _level_reference(cum, bsz, rev):
    c, hw = cum.shape
    i = bsz if rev else bsz - 1
    if 2 * bsz >= 8:
        c3 = cum.reshape(c // (2 * bsz), 2 * bsz, hw)
        return jnp.broadcast_to(c3[:, i:i + 1, :], c3.shape).reshape(c, hw)
    c3 = cum.reshape(c // 8, 8, hw)
    sub = lax.broadcasted_iota(jnp.int32, (1, 8, 1), 1) // (2 * bsz)
    ref = jnp.broadcast_to(c3[:, i:i + 1, :], c3.shape)
    for gidx in range(1, 8 // (2 * bsz)):
        r = gidx * 2 * bsz + i
        ref = jnp.where(sub == gidx, jnp.broadcast_to(c3[:, r:r + 1, :], c3.shape), ref)
    return ref.reshape(c, hw)


def _block_entry_reference(cum, wsz, rev):
    c, hw = cum.shape
    nb = c // wsz
    if nb == 1:
        return jnp.zeros_like(cum)
    c3 = cum.reshape(nb, wsz, hw)
    zero = jnp.zeros((1, 1, hw), F32)
    if rev:
        edge = jnp.concatenate([c3[1:, 0:1, :], zero], axis=0)
    else:
        edge = jnp.concatenate([zero, c3[:-1, wsz - 1:wsz, :]], axis=0)
    return jnp.broadcast_to(edge, c3.shape).reshape(c, hw)


def _hgrn_prefix(d, r0, lf_ref, cm_ref):
    c = HG_CHUNK
    lf = lf_ref[0, pl.ds(r0, c), :]
    cm = cm_ref[d]
    hi, mid, lo = _split3(lf)
    e = _dot(cm, hi) + _dot(cm, mid) + _dot(cm, lo)
    return e[0:c], e[c:c + 1]


def _hgrn_pairs(d, r0, kk_ref, q_ref, am_ref, cum, tot, hw, within):
    c = HG_CHUNK
    dk = hw // HG_HEADS
    rev = d == 1
    rows = pl.ds(r0, c)
    q = q_ref[0, rows, :]
    k = kk_ref[0, rows, :]
    qdec = q * jnp.exp(cum)
    kdec = k * jnp.exp(tot - cum)
    ds = jnp.exp(tot)

    lane_head = lax.broadcasted_iota(jnp.int32, (1, hw), 1) // dk
    heads = [lane_head == hh for hh in range(HG_HEADS)]
    tpos = lax.broadcasted_iota(jnp.int32, (c, 1), 0)

    def pair_product(qd, kd, mask_idx):
        q4 = jnp.concatenate([jnp.where(hm, qd, 0.0) for hm in heads], axis=0).astype(BF16)
        return am_ref[d, mask_idx] * _dot_nt(q4, kd.astype(BF16))

    att = jnp.zeros((HG_HEADS * c, c), F32)
    for lv, bsz in enumerate(HG_LEVELS):
        if bsz < within:
            continue
        pos = tpos % (2 * bsz)
        is_q = (pos < bsz) if rev else (pos >= bsz)
        diff = cum - _level_reference(cum, bsz, rev)
        w = jnp.exp(jnp.where(is_q, diff, -diff))
        att = att + pair_product(jnp.where(is_q, q * w, 0.0), jnp.where(is_q, 0.0, k * w), lv)
    widx = len(HG_LEVELS) + HG_WITHIN.index(within)
    if within == 1:
        att = att + pair_product(q, k, widx)
    else:
        diff = cum - _block_entry_reference(cum, within, rev)
        att = att + pair_product(q * jnp.exp(diff), k * jnp.exp(-diff), widx)
    return att.astype(BF16), qdec.astype(BF16), kdec.astype(BF16), ds


def _hgrn_apply(r0, v_ref, att, kdec, hw):
    c = HG_CHUNK
    dk = hw // HG_HEADS
    v = v_ref[0, pl.ds(r0, c), :]
    lane_head = lax.broadcasted_iota(jnp.int32, (1, hw), 1) // dk
    ov = _dot(att, v)
    o = jnp.zeros((c, hw), F32)
    for hh in range(HG_HEADS):
        o = o + jnp.where(lane_head == hh, ov[hh * c:(hh + 1) * c], 0.0)
    u = _dot_tn(v, kdec)
    rh = lax.broadcasted_iota(jnp.int32, (hw, 1), 0) // dk
    return o, jnp.where(rh == lane_head, u, 0.0)


def _hgrn_kernel(kf_ref, lff_ref, vf_ref, qf_ref, kb_ref, lfb_ref, vb_ref, qb_ref,
                 cm_ref, am_ref, of_ref, ob_ref, sf_ref, sb_ref, *, nchunk, hw, within):
    @pl.when(pl.program_id(1) == 0)
    def _init():
        sf_ref[...] = jnp.zeros(sf_ref.shape, F32)
        sb_ref[...] = jnp.zeros(sb_ref.shape, F32)

    jobs = []
    for jj in range(nchunk):
        jobs.append((0, jj * HG_CHUNK, kf_ref, lff_ref, vf_ref, qf_ref, of_ref, sf_ref))
        jobs.append((1, (nchunk - 1 - jj) * HG_CHUNK, kb_ref, lfb_ref, vb_ref, qb_ref, ob_ref, sb_ref))
    pre = [_hgrn_prefix(d, r0, lf_ref, cm_ref) for d, r0, _, lf_ref, _, _, _, _ in jobs]
    pairs = [_hgrn_pairs(d, r0, kk_ref, q_ref, am_ref, cum, tot, hw, within)
             for (d, r0, kk_ref, _, _, q_ref, _, _), (cum, tot) in zip(jobs, pre)]
    applied = [_hgrn_apply(r0, v_ref, att, kdec, hw)
               for (_, r0, _, _, v_ref, _, _, _), (att, _, kdec, _) in zip(jobs, pairs)]
    for (_, r0, _, _, _, _, o_ref, s_ref), (_, qdec, _, ds), (o, u) in zip(jobs, pairs, applied):
        st = s_ref[...]
        o_ref[0, pl.ds(r0, HG_CHUNK), :] = o + _dot_nt(qdec, st.astype(BF16))
        s_ref[...] = st * ds + u


def _hgrn(kf, lff, kb, lfb, v, hq, hc, lc, nchunk, within):
    b, t, hw = kf.shape
    tb = nchunk * HG_CHUNK
    nblk, ncb = t // tb, lc // tb
    cm, am = hc
    fwd = lambda bi, p: (bi, p, 0)
    bwd = lambda bi, p: (bi, jnp.where(p < ncb, ncb - 1 - p, nblk - 1 - (p - ncb)), 0)
    blk = lambda im: pl.BlockSpec((1, tb, hw), im)
    return pl.pallas_call(
        functools.partial(_hgrn_kernel, nchunk=nchunk, hw=hw, within=within),
        grid=(b, nblk),
        in_specs=[blk(fwd)] * 4 + [blk(bwd)] * 4 + [_resident(cm.shape), _resident(am.shape)],
        out_specs=[blk(fwd), blk(bwd)],
        out_shape=[jax.ShapeDtypeStruct((b, t, hw), F32)] * 2,
        scratch_shapes=[pltpu.VMEM((hw, hw), F32), pltpu.VMEM((hw, hw), F32)],
        name=f"hgrn_within{within}",
        compiler_params=pltpu.CompilerParams(
            dimension_semantics=("parallel", "arbitrary"),
            vmem_limit_bytes=V7X_VMEM_LIMIT),
    )(kf, lff, v, hq, kb, lfb, v, hq, cm, am)


def _mixer_residual(x, mb, mc, is_ctx, y_ref, at_ref, of_ref, ob_ref, g_ref, wf_ref, wo_ref, gn_ref,
                    d, fw, aw, hw):
    gate = jnp.where(is_ctx, mc[:, 2 * d:3 * d], mb[:, 2 * d:3 * d])

    fx = _dot(y_ref[0], wf_ref[...]).astype(BF16)
    mix = _dot_tn(at_ref[0], wo_ref[fw:fw + aw, :])
    mix = mix + _dot(fx, wo_ref[0:fw, :])

    o = of_ref[0] + ob_ref[0]
    dv = hw // HG_HEADS
    lane_head = lax.broadcasted_iota(jnp.int32, (1, hw), 1) // dv
    o2 = o * o
    ms = jnp.zeros_like(o)
    for hh in range(HG_HEADS):
        sel = lane_head == hh
        ssh = jnp.sum(jnp.where(sel, o2, 0.0), axis=-1, keepdims=True)
        ms = jnp.where(sel, ssh * (1.0 / dv), ms)
    gg = g_ref[0]
    rx = (o * lax.rsqrt(ms + EPS) * gn_ref[...]) * (gg * jax.nn.sigmoid(gg))
    mix = mix + _dot(rx.astype(BF16), wo_ref[fw + aw:fw + aw + hw, :])
    return x + gate * mix


def _ffn_residual(x, mb, mc, is_ctx, wg_ref, wu_ref, wd_ref, d, fchunk):
    tm = x.shape[0]
    h = _modulated(x, mb, mc, is_ctx, d, 3, 4).astype(BF16)
    gate = jnp.where(is_ctx, mc[:, 5 * d:6 * d], mb[:, 5 * d:6 * d])
    dff = wg_ref.shape[1]

    def gate_up(c0):
        a = _dot(h, wg_ref[:, c0:c0 + fchunk])
        u = _dot(h, wu_ref[:, c0:c0 + fchunk])
        return (a * jax.nn.sigmoid(a) * u).astype(BF16)

    acc = jnp.zeros((tm, d), F32)
    starts = list(range(0, dff, fchunk))
    act = gate_up(starts[0])
    for ci, c0 in enumerate(starts):
        nxt = gate_up(starts[ci + 1]) if ci + 1 < len(starts) else None
        acc = acc + _dot(act, wd_ref[c0:c0 + fchunk, :])
        act = nxt
    return x + gate * acc


def _tail_kernel(x_ref, mb_ref, mc_ref, y_ref, at_ref, of_ref, ob_ref, g_ref, wf_ref, wo_ref, gn_ref,
                 wg_ref, wu_ref, wd_ref, *rest, lc, tm, d, fw, aw, hw, fchunk):
    o_ref = rest[-1]
    row = pl.program_id(1) * tm + lax.broadcasted_iota(jnp.int32, (tm, 1), 0)
    is_ctx = row < lc
    mb, mc = mb_ref[0, 0], mc_ref[0, 0]
    x = _mixer_residual(x_ref[0], mb, mc, is_ctx, y_ref, at_ref, of_ref, ob_ref, g_ref,
                        wf_ref, wo_ref, gn_ref, d, fw, aw, hw)
    y = _ffn_residual(x, mb, mc, is_ctx, wg_ref, wu_ref, wd_ref, d, fchunk)
    if len(rest) == 2:
        y = y * lax.rsqrt(jnp.mean(y * y, axis=-1, keepdims=True) + EPS) * rest[0][...]
    o_ref[0] = y


def _tail(xs, mod4, l, nb, y, at, of, ob, g, wf, wo, gn, wg, wu, wd, lc, tm, final_gain=None):
    b, t, d = xs.shape
    n6 = mod4.shape[-1]
    fw, hw = y.shape[-1], of.shape[-1]
    aw = at.shape[1]
    fchunk = _pick_tile(wg.shape[1], 768)
    tok = lambda w: pl.BlockSpec((1, tm, w), lambda bi, i: (bi, i, 0))
    weights = [wf, wo, gn, wg, wu, wd] + ([] if final_gain is None else [final_gain.reshape(1, d)])
    return pl.pallas_call(
        functools.partial(_tail_kernel, lc=lc, tm=tm, d=d, fw=fw, aw=aw, hw=hw, fchunk=fchunk),
        grid=(b, t // tm),
        in_specs=[tok(d),
                  pl.BlockSpec((1, 1, 1, n6), lambda bi, i: (l, bi, 0, 0)),
                  pl.BlockSpec((1, 1, 1, n6), lambda bi, i: (l, nb, 0, 0)),
                  tok(fw),
                  pl.BlockSpec((1, aw, tm), lambda bi, i: (bi, 0, i)),
                  tok(hw), tok(hw), tok(hw)] + [_resident(a.shape) for a in weights],
        out_specs=tok(d),
        out_shape=jax.ShapeDtypeStruct((b, t, d), F32),
        name="tail" if final_gain is None else "tail_final",
        compiler_params=pltpu.CompilerParams(
            dimension_semantics=("parallel", "parallel"),
            vmem_limit_bytes=V7X_VMEM_LIMIT),
    )(xs, mod4, mod4, y, at, of, ob, g, *weights)


def _rope_tables(lc, s):
    half = HEAD_DIM // 2
    n = jnp.arange(s)
    freqs = ROPE_THETA ** (-jnp.arange(0, half, 2, dtype=F32) / half)
    ang_r = (n // GRID_W).astype(F32)[:, None] * freqs
    ang_c = (n % GRID_W).astype(F32)[:, None] * freqs
    pad = lambda a, v: jnp.concatenate([jnp.full((lc, a.shape[1]), v, F32), a], axis=0)
    cr, sr = pad(jnp.cos(ang_r), 1.0), pad(jnp.sin(ang_r), 0.0)
    cc, sc = pad(jnp.cos(ang_c), 1.0), pad(jnp.sin(ang_c), 0.0)
    cos_h = jnp.concatenate([cr, cr, cc, cc], axis=1)
    sin_h = jnp.concatenate([-sr, sr, -sc, sc], axis=1)
    ckn = jnp.tile(cos_h, (1, ATT_KV_HEADS))
    skn = jnp.tile(sin_h, (1, ATT_KV_HEADS))
    cqt = jnp.concatenate([cr, cc], axis=1).T
    sqt = jnp.concatenate([sr, sc], axis=1).T
    return ckn, skn, cqt, sqt


def kernel(x, c, ctx, c_ctx, w_ada, b_ada, w_in, w_four, q_norm, k_norm, hg_lb_logits, hg_norm,
           w_out, w_gate, w_up, w_down, final_norm):
    b, s, d = x.shape
    lc = ctx.shape[1]
    t = lc + s
    depth = w_ada.shape[0]
    fw = w_four.shape[-1]
    hw = hg_lb_logits.shape[-1]
    qw = ATT_HEADS * HEAD_DIM
    kw = ATT_KV_HEADS * HEAD_DIM
    assert w_in.shape[-1] == fw + qw + 2 * kw + 5 * hw
    assert s % (DFT_N2 * 8) == 0 and lc % 128 == 0 and s % GRID_W == 0

    tq = 256 if (lc % 256 == 0 and t % 256 == 0) else 128
    tkv = tq
    tm = _pick_tile(t, 768, mult=tkv)
    nchunk = max(g for g in (4, 2, 1) if lc % (g * HG_CHUNK) == 0 and t % (g * HG_CHUNK) == 0)

    xs = jnp.concatenate([ctx, x], axis=1)
    r8 = -(-(b + 1) // 8) * 8
    cvec = jnp.concatenate([c, c_ctx[None], jnp.zeros((r8 - b - 1, d), F32)], axis=0)
    mod = _adaln(cvec, w_ada, b_ada)
    mod4 = mod.reshape(depth, r8, 1, mod.shape[-1])

    lb_sm = jax.nn.softmax(hg_lb_logits.astype(F32), axis=1)
    lb_all = jnp.cumsum(lb_sm, axis=1) - lb_sm[:, :1]

    ckn, skn, cqt, sqt = _rope_tables(lc, s)
    fc = _fourier_consts(lc, s)
    hc = _hgrn_consts()

    o_q, o_k, o_v, o_h = fw, fw + qw, fw + qw + kw, fw + qw + 2 * kw
    for l in range(depth):
        wl = w_in[l]
        wn = jnp.concatenate([wl[:, 0:fw], wl[:, o_k:o_v], wl[:, o_h:]], axis=1).astype(BF16)
        wt = jnp.concatenate([wl[:, o_q:o_k], wl[:, o_v:o_h]], axis=1).T.astype(BF16)
        gk = jnp.tile(k_norm[l], ATT_KV_HEADS)[None]
        gq = q_norm[l][:, None]
        gn = jnp.tile(hg_norm[l], HG_HEADS)[None]
        (fz, kn, kf, lff, kb, lfb, v, hq, g, qt, vt, gd) = _inproj(
            xs, mod4, l, b, wn, wt, gk, ckn, skn, gq, cqt, sqt,
            lb_all[0, l][None], lb_all[1, l][None], lc, tm, tkv)
        y = _fourier(fz, fc, lc)
        bound = (HEAD_DIM ** 0.5 * LOG2E * BOUND_SLACK) * jnp.max(jnp.abs(q_norm[l])) * jnp.max(jnp.abs(k_norm[l]))
        at = lax.cond(bound < MAX_SAFE_LOG2_SCORE,
                      functools.partial(_attention, lc=lc, tq=tq, bounded=True),
                      functools.partial(_attention, lc=lc, tq=tq, bounded=False),
                      qt, kn, vt)
        hg_scan = functools.partial(_hgrn, hc=hc, lc=lc, nchunk=nchunk, within=HG_WITHIN[0])
        for gi, wsz in enumerate(HG_WITHIN[1:]):
            hg_scan = functools.partial(
                lambda ok, big, small, *a: lax.cond(ok, big, small, *a),
                jnp.max(gd[:, :, gi, 0]) < HG_MAX_BLOCK_DECAY,
                functools.partial(_hgrn, hc=hc, lc=lc, nchunk=nchunk, within=wsz), hg_scan)
        of, ob = hg_scan(kf, lff, kb, lfb, v, hq)
        xs = _tail(xs, mod4, l, b, y, at, of, ob, g, w_four[l].astype(BF16), w_out[l].astype(BF16), gn,
                   w_gate[l].astype(BF16), w_up[l].astype(BF16), w_down[l].astype(BF16),
                   lc, tm, final_norm if l == depth - 1 else None)
    return xs[:, lc:]
```

```python
import functools

import numpy as np
import jax
import jax.numpy as jnp
from jax import lax
from jax.experimental import pallas as pl
from jax.experimental.pallas import tpu as pltpu

F32 = jnp.float32
BF16 = jnp.bfloat16

EPS = 1e-6
GRID_W = 64
ROPE_THETA = 10000.0
FOURIER_GD = 64
ATT_HEADS = 8
ATT_KV_HEADS = 2
ATT_GROUP = ATT_HEADS // ATT_KV_HEADS
HEAD_DIM = 64
HG_HEADS = 4
HG_CHUNK = 64
HG_LEVELS = (32, 16, 8, 4, 2, 1)
HG_WITHIN = (1, 16, 64)
HG_MAX_BLOCK_DECAY = 80.0
DFT_N2 = 128
DFT_ROW_PAD = 8
NEG_BIG = -1e30
LOG2E = 1.4426950408889634
BOUND_SLACK = 1.05
MAX_SAFE_LOG2_SCORE = 100.0
ATTN_BOUNDED_KEYS = 2816
V7X_VMEM_LIMIT = 56 * 1024 * 1024


def _dot(a, b):
    return jnp.dot(a, b, preferred_element_type=F32)


def _dot_nt(a, b):
    return lax.dot_general(a, b, (((1,), (1,)), ((), ())), preferred_element_type=F32)


def _dot_tn(a, b):
    return lax.dot_general(a, b, (((0,), (0,)), ((), ())), preferred_element_type=F32)


def _split3(x):
    hi = x.astype(BF16)
    r1 = x - hi.astype(F32)
    mid = r1.astype(BF16)
    lo = (r1 - mid.astype(F32)).astype(BF16)
    return hi, mid, lo


def _pick_tile(n, cap, mult=128):
    best = None
    for t in range(mult, min(n, cap) + 1, mult):
        if n % t == 0:
            best = t
    if best is None:
        raise ValueError(f"no tile for {n}")
    return best


def _resident(shape):
    nd = len(shape)
    return pl.BlockSpec(shape, lambda *_: (0,) * nd, pipeline_mode=pl.Buffered(1))


def _adaln_kernel(c_ref, w_ref, b_ref, o_ref):
    cv = c_ref[...]
    a = cv * jax.nn.sigmoid(cv)
    w = w_ref[0]
    a_hi = a.astype(BF16)
    a_lo = (a - a_hi.astype(F32)).astype(BF16)
    w_hi = w.astype(BF16)
    w_lo = (w - w_hi.astype(F32)).astype(BF16)
    acc = _dot(a_hi, w_hi) + _dot(a_lo, w_hi) + _dot(a_hi, w_lo)
    o_ref[0] = acc + b_ref[0]


def _adaln(cvec, w_ada, b_ada):
    depth, d, n6 = w_ada.shape
    r8 = cvec.shape[0]
    tn = _pick_tile(n6, 1536)
    return pl.pallas_call(
        _adaln_kernel,
        grid=(depth, n6 // tn),
        in_specs=[
            pl.BlockSpec((r8, d), lambda l, j: (0, 0)),
            pl.BlockSpec((1, d, tn), lambda l, j: (l, 0, j)),
            pl.BlockSpec((1, 1, tn), lambda l, j: (l, 0, j)),
        ],
        out_specs=pl.BlockSpec((1, r8, tn), lambda l, j: (l, 0, j)),
        out_shape=jax.ShapeDtypeStruct((depth, r8, n6), F32),
        name="adaln",
        compiler_params=pltpu.CompilerParams(
            dimension_semantics=("parallel", "parallel"),
            vmem_limit_bytes=V7X_VMEM_LIMIT),
    )(cvec, w_ada, b_ada.reshape(depth, 1, n6))


def _modulated(x, mb, mc, is_ctx, d, k_shift, k_scale):
    r = lax.rsqrt(jnp.mean(x * x, axis=-1, keepdims=True) + EPS)
    sh = jnp.where(is_ctx, mc[:, k_shift * d:(k_shift + 1) * d], mb[:, k_shift * d:(k_shift + 1) * d])
    sc = jnp.where(is_ctx, mc[:, k_scale * d:(k_scale + 1) * d], mb[:, k_scale * d:(k_scale + 1) * d])
    return (x * r) * (1.0 + sc) + sh


def _inproj_kernel(x_ref, mb_ref, mc_ref, wn_ref, wt_ref, gk_ref, ckn_ref, skn_ref,
                   gq_ref, cqt_ref, sqt_ref, lbf_ref, lbb_ref,
                   fz_ref, k_ref, kf_ref, lff_ref, kb_ref, lfb_ref, v_ref, hq_ref, g_ref,
                   qt_ref, vt_ref, gd_ref, *, lc, tm, tkv, d, fw, hw):
    i = pl.program_id(1)
    x = x_ref[0]
    row = i * tm + lax.broadcasted_iota(jnp.int32, (tm, 1), 0)
    is_ctx = row < lc
    h = _modulated(x, mb_ref[0, 0], mc_ref[0, 0], is_ctx, d, 0, 1).astype(BF16)

    un = _dot(h, wn_ref[...])
    fz_ref[0] = un[:, 0:fw]

    o = fw
    kw = ATT_KV_HEADS * HEAD_DIM
    uk = un[:, o:o + kw]
    lane = lax.broadcasted_iota(jnp.int32, (1, kw), 1)
    uk2 = uk * uk
    ms = jnp.zeros_like(uk)
    for hh in range(ATT_KV_HEADS):
        sel = (lane // HEAD_DIM) == hh
        ssh = jnp.sum(jnp.where(sel, uk2, 0.0), axis=-1, keepdims=True)
        ms = jnp.where(sel, ssh * (1.0 / HEAD_DIM), ms)
    kn = uk * lax.rsqrt(ms + EPS) * gk_ref[...]
    first = (lane % 32) < 16
    partner = jnp.where(first, pltpu.roll(kn, kw - 16, axis=1), pltpu.roll(kn, 16, axis=1))
    k_ref[0] = (kn * ckn_ref[...] + partner * skn_ref[...]).astype(BF16)
    o += kw

    block_decay = [jnp.zeros((1, 1), F32) for _ in HG_WITHIN[1:]]
    for lb_ref, kk_ref, lf_ref in ((lbf_ref, kf_ref, lff_ref), (lbb_ref, kb_ref, lfb_ref)):
        z = un[:, o:o + hw]
        lb = lb_ref[...]
        kk_ref[0] = (1.0 - lb) * jax.nn.sigmoid(-z)
        lf = jnp.log(lb + (1.0 - lb) * jax.nn.sigmoid(z))
        lf_ref[0] = lf
        for gi, wsz in enumerate(HG_WITHIN[1:]):
            tot = -jnp.sum(lf.reshape(tm // wsz, wsz, hw), axis=1)
            top = jnp.max(jnp.max(tot, axis=0, keepdims=True), axis=1, keepdims=True)
            block_decay[gi] = jnp.maximum(block_decay[gi], top)
        o += hw
    grow = lax.broadcasted_iota(jnp.int32, (8, 128), 0)
    gd = jnp.zeros((8, 128), F32)
    for gi, dec in enumerate(block_decay):
        gd = jnp.where(grow == gi, dec, gd)
    gd_ref[0, 0] = gd
    v_ref[0] = un[:, o:o + hw].astype(BF16)
    hq_ref[0] = un[:, o + hw:o + 2 * hw]
    g_ref[0] = un[:, o + 2 * hw:o + 3 * hw]

    ut = _dot_nt(wt_ref[...], h)
    qw = ATT_HEADS * HEAD_DIM
    uq = ut[0:qw].reshape(ATT_HEADS, HEAD_DIM, tm)
    msq = jnp.mean(uq * uq, axis=1, keepdims=True)
    qn = uq * lax.rsqrt(msq + EPS) * gq_ref[...][None]
    cr, sr = cqt_ref[0:16][None], sqt_ref[0:16][None]
    cc, sc = cqt_ref[16:32][None], sqt_ref[16:32][None]
    x1r, x2r, x1c, x2c = qn[:, 0:16], qn[:, 16:32], qn[:, 32:48], qn[:, 48:64]
    qr = jnp.concatenate([x1r * cr - x2r * sr, x1r * sr + x2r * cr,
                          x1c * cc - x2c * sc, x1c * sc + x2c * cc], axis=1)
    qt_ref[0] = (qr * (HEAD_DIM ** -0.5 * LOG2E)).astype(BF16)

    uv = ut[qw:qw + kw].reshape(ATT_KV_HEADS, HEAD_DIM, tm)
    ones_row = lax.broadcasted_iota(jnp.int32, (ATT_KV_HEADS, 16, tm), 1) == 0
    aug = jnp.where(ones_row, 1.0, 0.0).astype(F32)
    vaug = jnp.concatenate([uv, aug], axis=1).astype(BF16)
    for cb in range(tm // tkv):
        vt_ref[0, :, cb] = vaug[:, :, cb * tkv:(cb + 1) * tkv]


def _inproj(xs, mod4, l, nb, wn, wt, gk, ckn, skn, gq, cqt, sqt, lbf, lbb, lc, tm, tkv):
    b, t, d = xs.shape
    n6 = mod4.shape[-1]
    fw = FOURIER_GD * 4
    hw = lbf.shape[-1]
    kw = ATT_KV_HEADS * HEAD_DIM
    tok = lambda w, dt: jax.ShapeDtypeStruct((b, t, w), dt)
    tok_spec = lambda w: pl.BlockSpec((1, tm, w), lambda bi, i: (bi, i, 0))
    out_shape = [tok(fw, F32), tok(kw, BF16), tok(hw, F32), tok(hw, F32), tok(hw, F32), tok(hw, F32),
                 tok(hw, BF16), tok(hw, F32), tok(hw, F32),
                 jax.ShapeDtypeStruct((b, ATT_HEADS, HEAD_DIM, t), BF16),
                 jax.ShapeDtypeStruct((b, ATT_KV_HEADS, t // tkv, HEAD_DIM + 16, tkv), BF16),
                 jax.ShapeDtypeStruct((b, t // tm, 8, 128), F32)]
    out_specs = [tok_spec(fw), tok_spec(kw)] + [tok_spec(hw)] * 7 + [
        pl.BlockSpec((1, ATT_HEADS, HEAD_DIM, tm), lambda bi, i: (bi, 0, 0, i)),
        pl.BlockSpec((1, ATT_KV_HEADS, tm // tkv, HEAD_DIM + 16, tkv), lambda bi, i: (bi, 0, i, 0, 0)),
        pl.BlockSpec((1, 1, 8, 128), lambda bi, i: (bi, i, 0, 0))]
    in_specs = [
        pl.BlockSpec((1, tm, d), lambda bi, i: (bi, i, 0)),
        pl.BlockSpec((1, 1, 1, n6), lambda bi, i: (l, bi, 0, 0)),
        pl.BlockSpec((1, 1, 1, n6), lambda bi, i: (l, nb, 0, 0)),
        _resident(wn.shape), _resident(wt.shape), _resident(gk.shape),
        pl.BlockSpec((tm, kw), lambda bi, i: (i, 0)),
        pl.BlockSpec((tm, kw), lambda bi, i: (i, 0)),
        _resident(gq.shape),
        pl.BlockSpec((32, tm), lambda bi, i: (0, i)),
        pl.BlockSpec((32, tm), lambda bi, i: (0, i)),
        _resident(lbf.shape), _resident(lbb.shape),
    ]
    return pl.pallas_call(
        functools.partial(_inproj_kernel, lc=lc, tm=tm, tkv=tkv, d=d, fw=fw, hw=hw),
        grid=(b, t // tm),
        in_specs=in_specs, out_specs=out_specs, out_shape=out_shape,
        name="inproj",
        compiler_params=pltpu.CompilerParams(
            dimension_semantics=("parallel", "parallel"),
            vmem_limit_bytes=V7X_VMEM_LIMIT),
    )(xs, mod4, mod4, wn, wt, gk, ckn, skn, gq, cqt, sqt, lbf, lbb)


def _fourier_kernel(z_ref, f1_ref, twc_ref, tws_ref, c2_ref, s2_ref, cs_ref, cl_ref,
                    y_ref, a1_ref, p_ref, *, lc, n1, scale_c, scale_x):
    n2 = DFT_N2
    cs = cs_ref[...]
    pa, pp = 2 * n1 + DFT_ROW_PAD, n2 + DFT_ROW_PAD

    zc = z_ref[0, 0:lc, :].astype(BF16)
    pc = _dot(cl_ref[...], zc)
    pcat = jnp.concatenate([pc[0:lc], pc[lc:2 * lc]], axis=1).astype(BF16)
    y_ref[0, 0:lc, :] = (_dot(pcat, cs) * scale_c).astype(y_ref.dtype)

    f1 = f1_ref[...]

    def stage1(j, carry):
        xj = z_ref[0, pl.ds(lc + j, n1, stride=n2), :].astype(BF16)
        a1_ref[pl.ds(pl.multiple_of(j * pa, 8), 2 * n1), :] = _dot(f1, xj)
        return carry

    lax.fori_loop(0, n2, stage1, 0, unroll=16)

    c2 = c2_ref[...]
    s2 = s2_ref[...]

    group = 8 if n1 % 8 == 0 else 1

    def stage2(gidx, carry):
        k1s = [gidx * group + u for u in range(group)]
        pris = []
        for k1 in k1s:
            ar = a1_ref[pl.ds(k1, n2, stride=pa), :]
            ai = a1_ref[pl.ds(n1 + k1, n2, stride=pa), :]
            twc = twc_ref[pl.ds(k1, 1), :]
            tws = tws_ref[pl.ds(k1, 1), :]
            gr = c2 * twc - s2 * tws
            gi = -(s2 * twc + c2 * tws)
            gm = jnp.concatenate([jnp.concatenate([gr, -gi], axis=1),
                                  jnp.concatenate([gi, gr], axis=1)], axis=0).astype(BF16)
            rhs = jnp.concatenate([ar, ai], axis=0).astype(BF16)
            pris.append(_dot(gm, rhs))
        for k1, pri in zip(k1s, pris):
            pcat2 = jnp.concatenate([pri[0:n2], pri[n2:2 * n2]], axis=1).astype(BF16)
            p_ref[pl.ds(pl.multiple_of(k1 * pp, 8), n2), :] = _dot(pcat2, cs) * scale_x
        return carry

    lax.fori_loop(0, n1 // group, stage2, 0)

    def stage3(k2, carry):
        blk = p_ref[pl.ds(k2, n1, stride=pp), :]
        y_ref[0, pl.ds(pl.multiple_of(lc + k2 * n1, n1), n1), :] = blk.astype(y_ref.dtype)
        return carry

    lax.fori_loop(0, n2, stage3, 0, unroll=8)


def _fourier_consts(lc, s):
    n2 = DFT_N2
    n1 = s // n2
    f64 = np.float64
    k = np.arange(n1, dtype=f64)
    a1 = 2 * np.pi * np.outer(k, k) / n1
    f1 = np.concatenate([np.cos(a1), -np.sin(a1)], axis=0)
    at = 2 * np.pi * np.outer(np.arange(n1, dtype=f64), np.arange(n2, dtype=f64)) / s
    k2 = np.arange(n2, dtype=f64)
    a2 = 2 * np.pi * np.outer(k2, k2) / n2
    w = 2 * FOURIER_GD
    ch = np.arange(w)
    same = (ch[:, None] // FOURIER_GD) == (ch[None, :] // FOURIER_GD)
    ag = 2 * np.pi * np.outer(ch % FOURIER_GD, ch % FOURIER_GD) / FOURIER_GD
    cs = np.concatenate([np.where(same, np.cos(ag), 0.0), np.where(same, np.sin(ag), 0.0)], axis=0)
    kc = np.arange(lc, dtype=f64)
    al = 2 * np.pi * np.outer(kc, kc) / lc
    cl = np.concatenate([np.cos(al), -np.sin(al)], axis=0)
    return dict(
        f1=jnp.asarray(f1, BF16), twc=jnp.asarray(np.cos(at), F32), tws=jnp.asarray(np.sin(at), F32),
        c2=jnp.asarray(np.cos(a2), F32), s2=jnp.asarray(np.sin(a2), F32),
        cs=jnp.asarray(cs, BF16), cl=jnp.asarray(cl, BF16), n1=n1)


def _fourier(fz, fc, lc):
    b, t, fw = fz.shape
    s = t - lc
    n1 = fc["n1"]
    w = 2 * FOURIER_GD
    consts = [fc[k] for k in ("f1", "twc", "tws", "c2", "s2", "cs", "cl")]
    return pl.pallas_call(
        functools.partial(_fourier_kernel, lc=lc, n1=n1,
                          scale_c=float((lc * FOURIER_GD) ** -0.5),
                          scale_x=float((s * FOURIER_GD) ** -0.5)),
        grid=(b, fw // w),
        in_specs=[pl.BlockSpec((1, t, w), lambda bi, hi: (bi, 0, hi))] +
                 [_resident(c.shape) for c in consts],
        out_specs=pl.BlockSpec((1, t, w), lambda bi, hi: (bi, 0, hi)),
        out_shape=jax.ShapeDtypeStruct((b, t, fw), BF16),
        scratch_shapes=[pltpu.VMEM((DFT_N2 * (2 * n1 + DFT_ROW_PAD), w), F32),
                        pltpu.VMEM((n1 * (DFT_N2 + DFT_ROW_PAD), w), F32)],
        name="fourier",
        compiler_params=pltpu.CompilerParams(
            dimension_semantics=("parallel", "parallel"),
            vmem_limit_bytes=V7X_VMEM_LIMIT),
    )(fz, *consts)


def _place_queries(q_ref, qs_ref):
    g = pl.program_id(1)
    qcat = jnp.concatenate([q_ref[0, j] for j in range(ATT_GROUP)], axis=1)
    zero = jnp.zeros_like(qcat)
    for gg in range(ATT_KV_HEADS):
        @pl.when(g == gg)
        def _place():
            for hh in range(ATT_KV_HEADS):
                qs_ref[hh * HEAD_DIM:(hh + 1) * HEAD_DIM, :] = qcat if hh == gg else zero


def _score_block(k_ref, qs_ref, blk, tk, lc, masked):
    kb = k_ref[0, pl.ds(pl.multiple_of(blk * tk, tk), tk), :]
    s = _dot(kb, qs_ref[...])
    if masked:
        kidx = blk * tk + lax.broadcasted_iota(jnp.int32, (tk, 1), 0)
        s = jnp.where(kidx < lc, s, NEG_BIG)
    return s


def _attn_finish(acc_ref, o_ref, tq):
    acc = acc_ref[...]
    o = acc[0:HEAD_DIM] / acc[HEAD_DIM:HEAD_DIM + 1]
    o_ref[0] = jnp.concatenate([o[:, j * tq:(j + 1) * tq] for j in range(ATT_GROUP)],
                               axis=0).astype(o_ref.dtype)


def _attn_bounded_kernel(q_ref, k_ref, v_ref, o_ref, qs_ref, p0_ref, p1_ref, acc_ref,
                         *, lc, tq, tk, nqc, nkc, nk):
    qi = pl.program_id(2)
    _place_queries(q_ref, qs_ref)
    acc_ref[...] = jnp.zeros(acc_ref.shape, F32)

    p_ref = (p0_ref, p1_ref)
    tkv = v_ref.shape[-1]
    nsub = tk // tkv

    def probs(blk, slot):
        p_ref[slot][...] = jnp.exp2(_score_block(k_ref, qs_ref, blk, tk, lc, False)).astype(BF16)

    def weighted(blk, slot):
        pv = _dot(v_ref[0, 0, blk * nsub], p_ref[slot][0:tkv, :])
        for c in range(1, nsub):
            pv = pv + _dot(v_ref[0, 0, blk * nsub + c], p_ref[slot][c * tkv:(c + 1) * tkv, :])
        acc_ref[...] += pv

    @pl.when(qi < nqc)
    def _ctx():
        nsc = -(-lc // tkv)
        s = _dot(k_ref[0, 0:nsc * tkv, :], qs_ref[...])
        if lc % tkv:
            s = jnp.where(lax.broadcasted_iota(jnp.int32, (nsc * tkv, 1), 0) < lc, s, NEG_BIG)
        p = jnp.exp2(s).astype(BF16)
        pv = _dot(v_ref[0, 0, 0], p[0:tkv])
        for c in range(1, nsc):
            pv = pv + _dot(v_ref[0, 0, c], p[c * tkv:(c + 1) * tkv])
        acc_ref[...] += pv

    @pl.when(qi >= nqc)
    def _lat():
        def steady(t, slot):
            probs(t + 1, 1 - slot)
            weighted(t, slot)

        probs(0, 0)
        npairs = (nk - 1) // 2

        def pair(i, carry):
            steady(2 * i, 0)
            steady(2 * i + 1, 1)
            return carry

        lax.fori_loop(0, npairs, pair, 0)
        if (nk - 1) % 2:
            steady(nk - 2, (nk - 2) % 2)
        weighted(nk - 1, (nk - 1) % 2)

    _attn_finish(acc_ref, o_ref, tq)


def _attn_kernel(q_ref, k_ref, v_ref, o_ref, qs_ref, s0_ref, s1_ref, mb0_ref, mb1_ref, p0_ref, p1_ref,
                 al0_ref, al1_ref, m_ref, acc_ref, *, lc, tq, tk, nqc, nkc, nk):
    s_ref, mb_ref = (s0_ref, s1_ref), (mb0_ref, mb1_ref)
    p_ref, al_ref = (p0_ref, p1_ref), (al0_ref, al1_ref)
    qi = pl.program_id(2)

    _place_queries(q_ref, qs_ref)
    m_ref[...] = jnp.full(m_ref.shape, NEG_BIG, F32)
    acc_ref[...] = jnp.zeros(acc_ref.shape, F32)

    def scores(blk, slot, masked=False):
        s = _score_block(k_ref, qs_ref, blk, tk, lc, masked)
        s_ref[slot][...] = s
        mb_ref[slot][...] = jnp.max(s, axis=0, keepdims=True)

    def probs(slot):
        m_prev = m_ref[...]
        m_new = jnp.maximum(m_prev, mb_ref[slot][...])
        al_ref[slot][...] = jnp.exp2(m_prev - m_new)
        p_ref[slot][...] = jnp.exp2(s_ref[slot][...] - m_new).astype(BF16)
        m_ref[...] = m_new

    def weighted(blk, slot):
        acc_ref[...] = al_ref[slot][...] * acc_ref[...] + _dot(v_ref[0, 0, blk], p_ref[slot][...])

    @pl.when(qi < nqc)
    def _ctx():
        for blk in range(nkc):
            scores(blk, 0, masked=True)
            probs(0)
            weighted(blk, 0)

    @pl.when(qi >= nqc)
    def _lat():
        def steady(t, slot):
            scores(t + 2, slot)
            weighted(t, slot)
            probs(1 - slot)

        scores(0, 0)
        scores(1, 1)
        probs(0)
        npairs = (nk - 2) // 2

        def pair(i, carry):
            steady(2 * i, 0)
            steady(2 * i + 1, 1)
            return carry

        lax.fori_loop(0, npairs, pair, 0)
        if (nk - 2) % 2:
            steady(2 * npairs, 0)
        weighted(nk - 2, (nk - 2) % 2)
        probs((nk - 1) % 2)
        weighted(nk - 1, (nk - 1) % 2)

    _attn_finish(acc_ref, o_ref, tq)


def _attention(qt, kn, vt, lc, tq, bounded):
    b, _, _, t = qt.shape
    nkv, tkv = vt.shape[2], vt.shape[4]
    tk = _pick_tile(t, ATTN_BOUNDED_KEYS, mult=tkv) if bounded else tkv
    nk = t // tk
    nq = t // tq
    nqc = lc // tq
    nkc = -(-lc // tk)
    wide = ATT_GROUP * tq
    qs_scratch = pltpu.VMEM((ATT_KV_HEADS * HEAD_DIM, wide), BF16)
    acc_scratch = pltpu.VMEM((HEAD_DIM + 16, wide), F32)
    if bounded:
        body = _attn_bounded_kernel
        scratch = [qs_scratch, pltpu.VMEM((tk, wide), BF16), pltpu.VMEM((tk, wide), BF16), acc_scratch]
    else:
        assert nk >= 2
        body = _attn_kernel
        scratch = [qs_scratch,
                   pltpu.VMEM((tk, wide), F32), pltpu.VMEM((tk, wide), F32),
                   pltpu.VMEM((1, wide), F32), pltpu.VMEM((1, wide), F32),
                   pltpu.VMEM((tk, wide), BF16), pltpu.VMEM((tk, wide), BF16),
                   pltpu.VMEM((1, wide), F32), pltpu.VMEM((1, wide), F32),
                   pltpu.VMEM((1, wide), F32),
                   acc_scratch]
    return pl.pallas_call(
        functools.partial(body, lc=lc, tq=tq, tk=tk, nqc=nqc, nkc=nkc, nk=nk),
        grid=(b, ATT_KV_HEADS, nq),
        in_specs=[
            pl.BlockSpec((1, ATT_GROUP, HEAD_DIM, tq), lambda bi, g, qi: (bi, g, 0, qi)),
            pl.BlockSpec((1, t, ATT_KV_HEADS * HEAD_DIM), lambda bi, g, qi: (bi, 0, 0)),
            pl.BlockSpec((1, 1, nkv, HEAD_DIM + 16, tkv), lambda bi, g, qi: (bi, g, 0, 0, 0)),
        ],
        out_specs=pl.BlockSpec((1, ATT_GROUP * HEAD_DIM, tq), lambda bi, g, qi: (bi, g, qi)),
        out_shape=jax.ShapeDtypeStruct((b, ATT_HEADS * HEAD_DIM, t), BF16),
        scratch_shapes=scratch,
        name="attn_bounded" if bounded else "attn_online",
        compiler_params=pltpu.CompilerParams(
            dimension_semantics=("parallel", "parallel", "arbitrary"),
            vmem_limit_bytes=V7X_VMEM_LIMIT),
    )(qt, kn, vt)


def _hgrn_consts():
    c = HG_CHUNK
    t = np.arange(c)
    tt, rr = t[:, None], t[None, :]
    mats, masks = [], []
    for rev in (False, True):
        mats.append(np.concatenate([(rr >= tt) if rev else (rr <= tt), np.ones((8, c), bool)], axis=0))
        pair_masks = []
        for bsz in HG_LEVELS:
            blk, pos = t // (2 * bsz), t % (2 * bsz)
            qv = (pos < bsz) if rev else (pos >= bsz)
            pair_masks.append((blk[:, None] == blk[None, :]) & qv[:, None] & ~qv[None, :])
        for wsz in HG_WITHIN:
            same = (tt // wsz) == (rr // wsz)
            pair_masks.append(same & ((rr >= tt) if rev else (rr <= tt)))
        masks.append(np.stack([np.tile(m, (HG_HEADS, 1)) for m in pair_masks]))
    return jnp.asarray(np.stack(mats), BF16), jnp.asarray(np.stack(masks), F32)


def _level_reference(cum, bsz, rev):
    c, hw = cum.shape
    i = bsz if rev else bsz - 1
    if 2 * bsz >= 8:
        c3 = cum.reshape(c // (2 * bsz), 2 * bsz, hw)
        return jnp.broadcast_to(c3[:, i:i + 1, :], c3.shape).reshape(c, hw)
    c3 = cum.reshape(c // 8, 8, hw)
    sub = lax.broadcasted_iota(jnp.int32, (1, 8, 1), 1) // (2 * bsz)
    ref = jnp.broadcast_to(c3[:, i:i + 1, :], c3.shape)
    for gidx in range(1, 8 // (2 * bsz)):
        r = gidx * 2 * bsz + i
        ref = jnp.where(sub == gidx, jnp.broadcast_to(c3[:, r:r + 1, :], c3.shape), ref)
    return ref.reshape(c, hw)


def _block_entry_reference(cum, wsz, rev):
    c, hw = cum.shape
    nb = c // wsz
    if nb == 1:
        return jnp.zeros_like(cum)
    c3 = cum.reshape(nb, wsz, hw)
    zero = jnp.zeros((1, 1, hw), F32)
    if rev:
        edge = jnp.concatenate([c3[1:, 0:1, :], zero], axis=0)
    else:
        edge = jnp.concatenate([zero, c3[:-1, wsz - 1:wsz, :]], axis=0)
    return jnp.broadcast_to(edge, c3.shape).reshape(c, hw)


def _hgrn_prefix(d, r0, lf_ref, cm_ref):
    c = HG_CHUNK
    lf = lf_ref[0, pl.ds(r0, c), :]
    cm = cm_ref[d]
    hi, mid, lo = _split3(lf)
    e = _dot(cm, hi) + _dot(cm, mid) + _dot(cm, lo)
    return e[0:c], e[c:c + 1]


def _hgrn_pairs(d, r0, kk_ref, q_ref, am_ref, cum, tot, hw, within):
    c = HG_CHUNK
    dk = hw // HG_HEADS
    rev = d == 1
    rows = pl.ds(r0, c)
    q = q_ref[0, rows, :]
    k = kk_ref[0, rows, :]
    qdec = q * jnp.exp(cum)
    kdec = k * jnp.exp(tot - cum)
    ds = jnp.exp(tot)

    lane_head = lax.broadcasted_iota(jnp.int32, (1, hw), 1) // dk
    heads = [lane_head == hh for hh in range(HG_HEADS)]
    tpos = lax.broadcasted_iota(jnp.int32, (c, 1), 0)

    def pair_product(qd, kd, mask_idx):
        q4 = jnp.concatenate([jnp.where(hm, qd, 0.0) for hm in heads], axis=0).astype(BF16)
        return am_ref[d, mask_idx] * _dot_nt(q4, kd.astype(BF16))

    att = jnp.zeros((HG_HEADS * c, c), F32)
    for lv, bsz in enumerate(HG_LEVELS):
        if bsz < within:
            continue
        pos = tpos % (2 * bsz)
        is_q = (pos < bsz) if rev else (pos >= bsz)
        diff = cum - _level_reference(cum, bsz, rev)
        w = jnp.exp(jnp.where(is_q, diff, -diff))
        att = att + pair_product(jnp.where(is_q, q * w, 0.0), jnp.where(is_q, 0.0, k * w), lv)
    widx = len(HG_LEVELS) + HG_WITHIN.index(within)
    if within == 1:
        att = att + pair_product(q, k, widx)
    else:
        diff = cum - _block_entry_reference(cum, within, rev)
        att = att + pair_product(q * jnp.exp(diff), k * jnp.exp(-diff), widx)
    return att.astype(BF16), qdec.astype(BF16), kdec.astype(BF16), ds


def _hgrn_apply(r0, v_ref, att, kdec, hw):
    c = HG_CHUNK
    dk = hw // HG_HEADS
    v = v_ref[0, pl.ds(r0, c), :]
    lane_head = lax.broadcasted_iota(jnp.int32, (1, hw), 1) // dk
    ov = _dot(att, v)
    o = jnp.zeros((c, hw), F32)
    for hh in range(HG_HEADS):
        o = o + jnp.where(lane_head == hh, ov[hh * c:(hh + 1) * c], 0.0)
    u = _dot_tn(v, kdec)
    rh = lax.broadcasted_iota(jnp.int32, (hw, 1), 0) // dk
    return o, jnp.where(rh == lane_head, u, 0.0)


def _hgrn_kernel(kf_ref, lff_ref, vf_ref, qf_ref, kb_ref, lfb_ref, vb_ref, qb_ref,
                 cm_ref, am_ref, of_ref, ob_ref, sf_ref, sb_ref, *, nchunk, hw, within):
    @pl.when(pl.program_id(1) == 0)
    def _init():
        sf_ref[...] = jnp.zeros(sf_ref.shape, F32)
        sb_ref[...] = jnp.zeros(sb_ref.shape, F32)

    jobs = []
    for jj in range(nchunk):
        jobs.append((0, jj * HG_CHUNK, kf_ref, lff_ref, vf_ref, qf_ref, of_ref, sf_ref))
        jobs.append((1, (nchunk - 1 - jj) * HG_CHUNK, kb_ref, lfb_ref, vb_ref, qb_ref, ob_ref, sb_ref))
    pre = [_hgrn_prefix(d, r0, lf_ref, cm_ref) for d, r0, _, lf_ref, _, _, _, _ in jobs]
    pairs = [_hgrn_pairs(d, r0, kk_ref, q_ref, am_ref, cum, tot, hw, within)
             for (d, r0, kk_ref, _, _, q_ref, _, _), (cum, tot) in zip(jobs, pre)]
    applied = [_hgrn_apply(r0, v_ref, att, kdec, hw)
               for (_, r0, _, _, v_ref, _, _, _), (att, _, kdec, _) in zip(jobs, pairs)]
    for (_, r0, _, _, _, _, o_ref, s_ref), (_, qdec, _, ds), (o, u) in zip(jobs, pairs, applied):
        st = s_ref[...]
        o_ref[0, pl.ds(r0, HG_CHUNK), :] = o + _dot_nt(qdec, st.astype(BF16))
        s_ref[...] = st * ds + u


def _hgrn(kf, lff, kb, lfb, v, hq, hc, lc, nchunk, within):
    b, t, hw = kf.shape
    tb = nchunk * HG_CHUNK
    nblk, ncb = t // tb, lc // tb
    cm, am = hc
    fwd = lambda bi, p: (bi, p, 0)
    bwd = lambda bi, p: (bi, jnp.where(p < ncb, ncb - 1 - p, nblk - 1 - (p - ncb)), 0)
    blk = lambda im: pl.BlockSpec((1, tb, hw), im)
    return pl.pallas_call(
        functools.partial(_hgrn_kernel, nchunk=nchunk, hw=hw, within=within),
        grid=(b, nblk),
        in_specs=[blk(fwd)] * 4 + [blk(bwd)] * 4 + [_resident(cm.shape), _resident(am.shape)],
        out_specs=[blk(fwd), blk(bwd)],
        out_shape=[jax.ShapeDtypeStruct((b, t, hw), F32)] * 2,
        scratch_shapes=[pltpu.VMEM((hw, hw), F32), pltpu.VMEM((hw, hw), F32)],
        name=f"hgrn_within{within}",
        compiler_params=pltpu.CompilerParams(
            dimension_semantics=("parallel", "arbitrary"),
            vmem_limit_bytes=V7X_VMEM_LIMIT),
    )(kf, lff, v, hq, kb, lfb, v, hq, cm, am)


def _mixer_residual(x, mb, mc, is_ctx, y_ref, at_ref, of_ref, ob_ref, g_ref, wf_ref, wo_ref, gn_ref,
                    d, fw, aw, hw):
    gate = jnp.where(is_ctx, mc[:, 2 * d:3 * d], mb[:, 2 * d:3 * d])

    fx = _dot(y_ref[0], wf_ref[...]).astype(BF16)
    mix = _dot_tn(at_ref[0], wo_ref[fw:fw + aw, :])
    mix = mix + _dot(fx, wo_ref[0:fw, :])

    o = of_ref[0] + ob_ref[0]
    dv = hw // HG_HEADS
    lane_head = lax.broadcasted_iota(jnp.int32, (1, hw), 1) // dv
    o2 = o * o
    ms = jnp.zeros_like(o)
    for hh in range(HG_HEADS):
        sel = lane_head == hh
        ssh = jnp.sum(jnp.where(sel, o2, 0.0), axis=-1, keepdims=True)
        ms = jnp.where(sel, ssh * (1.0 / dv), ms)
    gg = g_ref[0]
    rx = (o * lax.rsqrt(ms + EPS) * gn_ref[...]) * (gg * jax.nn.sigmoid(gg))
    mix = mix + _dot(rx.astype(BF16), wo_ref[fw + aw:fw + aw + hw, :])
    return x + gate * mix


def _ffn_residual(x, mb, mc, is_ctx, wg_ref, wu_ref, wd_ref, d, fchunk):
    tm = x.shape[0]
    h = _modulated(x, mb, mc, is_ctx, d, 3, 4).astype(BF16)
    gate = jnp.where(is_ctx, mc[:, 5 * d:6 * d], mb[:, 5 * d:6 * d])
    dff = wg_ref.shape[1]

    def gate_up(c0):
        a = _dot(h, wg_ref[:, c0:c0 + fchunk])
        u = _dot(h, wu_ref[:, c0:c0 + fchunk])
        return (a * jax.nn.sigmoid(a) * u).astype(BF16)

    acc = jnp.zeros((tm, d), F32)
    starts = list(range(0, dff, fchunk))
    act = gate_up(starts[0])
    for ci, c0 in enumerate(starts):
        nxt = gate_up(starts[ci + 1]) if ci + 1 < len(starts) else None
        acc = acc + _dot(act, wd_ref[c0:c0 + fchunk, :])
        act = nxt
    return x + gate * acc


def _tail_kernel(x_ref, mb_ref, mc_ref, y_ref, at_ref, of_ref, ob_ref, g_ref, wf_ref, wo_ref, gn_ref,
                 wg_ref, wu_ref, wd_ref, *rest, lc, tm, d, fw, aw, hw, fchunk):
    o_ref = rest[-1]
    row = pl.program_id(1) * tm + lax.broadcasted_iota(jnp.int32, (tm, 1), 0)
    is_ctx = row < lc
    mb, mc = mb_ref[0, 0], mc_ref[0, 0]
    x = _mixer_residual(x_ref[0], mb, mc, is_ctx, y_ref, at_ref, of_ref, ob_ref, g_ref,
                        wf_ref, wo_ref, gn_ref, d, fw, aw, hw)
    y = _ffn_residual(x, mb, mc, is_ctx, wg_ref, wu_ref, wd_ref, d, fchunk)
    if len(rest) == 2:
        y = y * lax.rsqrt(jnp.mean(y * y, axis=-1, keepdims=True) + EPS) * rest[0][...]
    o_ref[0] = y


def _tail(xs, mod4, l, nb, y, at, of, ob, g, wf, wo, gn, wg, wu, wd, lc, tm, final_gain=None):
    b, t, d = xs.shape
    n6 = mod4.shape[-1]
    fw, hw = y.shape[-1], of.shape[-1]
    aw = at.shape[1]
    fchunk = _pick_tile(wg.shape[1], 768)
    tok = lambda w: pl.BlockSpec((1, tm, w), lambda bi, i: (bi, i, 0))
    weights = [wf, wo, gn, wg, wu, wd] + ([] if final_gain is None else [final_gain.reshape(1, d)])
    return pl.pallas_call(
        functools.partial(_tail_kernel, lc=lc, tm=tm, d=d, fw=fw, aw=aw, hw=hw, fchunk=fchunk),
        grid=(b, t // tm),
        in_specs=[tok(d),
                  pl.BlockSpec((1, 1, 1, n6), lambda bi, i: (l, bi, 0, 0)),
                  pl.BlockSpec((1, 1, 1, n6), lambda bi, i: (l, nb, 0, 0)),
                  tok(fw),
                  pl.BlockSpec((1, aw, tm), lambda bi, i: (bi, 0, i)),
                  tok(hw), tok(hw), tok(hw)] + [_resident(a.shape) for a in weights],
        out_specs=tok(d),
        out_shape=jax.ShapeDtypeStruct((b, t, d), F32),
        name="tail" if final_gain is None else "tail_final",
        compiler_params=pltpu.CompilerParams(
            dimension_semantics=("parallel", "parallel"),
            vmem_limit_bytes=V7X_VMEM_LIMIT),
    )(xs, mod4, mod4, y, at, of, ob, g, *weights)


def _rope_tables(lc, s):
    half = HEAD_DIM // 2
    n = jnp.arange(s)
    freqs = ROPE_THETA ** (-jnp.arange(0, half, 2, dtype=F32) / half)
    ang_r = (n // GRID_W).astype(F32)[:, None] * freqs
    ang_c = (n % GRID_W).astype(F32)[:, None] * freqs
    pad = lambda a, v: jnp.concatenate([jnp.full((lc, a.shape[1]), v, F32), a], axis=0)
    cr, sr = pad(jnp.cos(ang_r), 1.0), pad(jnp.sin(ang_r), 0.0)
    cc, sc = pad(jnp.cos(ang_c), 1.0), pad(jnp.sin(ang_c), 0.0)
    cos_h = jnp.concatenate([cr, cr, cc, cc], axis=1)
    sin_h = jnp.concatenate([-sr, sr, -sc, sc], axis=1)
    ckn = jnp.tile(cos_h, (1, ATT_KV_HEADS))
    skn = jnp.tile(sin_h, (1, ATT_KV_HEADS))
    cqt = jnp.concatenate([cr, cc], axis=1).T
    sqt = jnp.concatenate([sr, sc], axis=1).T
    return ckn, skn, cqt, sqt


def kernel(x, c, ctx, c_ctx, w_ada, b_ada, w_in, w_four, q_norm, k_norm, hg_lb_logits, hg_norm,
           w_out, w_gate, w_up, w_down, final_norm):
    b, s, d = x.shape
    lc = ctx.shape[1]
    t = lc + s
    depth = w_ada.shape[0]
    fw = w_four.shape[-1]
    hw = hg_lb_logits.shape[-1]
    qw = ATT_HEADS * HEAD_DIM
    kw = ATT_KV_HEADS * HEAD_DIM
    assert w_in.shape[-1] == fw + qw + 2 * kw + 5 * hw
    assert s % (DFT_N2 * 8) == 0 and lc % 128 == 0 and s % GRID_W == 0

    tq = 256 if (lc % 256 == 0 and t % 256 == 0) else 128
    tkv = tq
    tm = _pick_tile(t, 768, mult=tkv)
    nchunk = max(g for g in (4, 2, 1) if lc % (g * HG_CHUNK) == 0 and t % (g * HG_CHUNK) == 0)

    xs = jnp.concatenate([ctx, x], axis=1)
    r8 = -(-(b + 1) // 8) * 8
    cvec = jnp.concatenate([c, c_ctx[None], jnp.zeros((r8 - b - 1, d), F32)], axis=0)
    mod = _adaln(cvec, w_ada, b_ada)
    mod4 = mod.reshape(depth, r8, 1, mod.shape[-1])

    lb_sm = jax.nn.softmax(hg_lb_logits.astype(F32), axis=1)
    lb_all = jnp.cumsum(lb_sm, axis=1) - lb_sm[:, :1]

    ckn, skn, cqt, sqt = _rope_tables(lc, s)
    fc = _fourier_consts(lc, s)
    hc = _hgrn_consts()

    o_q, o_k, o_v, o_h = fw, fw + qw, fw + qw + kw, fw + qw + 2 * kw
    for l in range(depth):
        wl = w_in[l]
        wn = jnp.concatenate([wl[:, 0:fw], wl[:, o_k:o_v], wl[:, o_h:]], axis=1).astype(BF16)
        wt = jnp.concatenate([wl[:, o_q:o_k], wl[:, o_v:o_h]], axis=1).T.astype(BF16)
        gk = jnp.tile(k_norm[l], ATT_KV_HEADS)[None]
        gq = q_norm[l][:, None]
        gn = jnp.tile(hg_norm[l], HG_HEADS)[None]
        (fz, kn, kf, lff, kb, lfb, v, hq, g, qt, vt, gd) = _inproj(
            xs, mod4, l, b, wn, wt, gk, ckn, skn, gq, cqt, sqt,
            lb_all[0, l][None], lb_all[1, l][None], lc, tm, tkv)
        y = _fourier(fz, fc, lc)
        bound = (HEAD_DIM ** 0.5 * LOG2E * BOUND_SLACK) * jnp.max(jnp.abs(q_norm[l])) * jnp.max(jnp.abs(k_norm[l]))
        at = lax.cond(bound < MAX_SAFE_LOG2_SCORE,
                      functools.partial(_attention, lc=lc, tq=tq, bounded=True),
                      functools.partial(_attention, lc=lc, tq=tq, bounded=False),
                      qt, kn, vt)
        hg_scan = functools.partial(_hgrn, hc=hc, lc=lc, nchunk=nchunk, within=HG_WITHIN[0])
        for gi, wsz in enumerate(HG_WITHIN[1:]):
            hg_scan = functools.partial(
                lambda ok, big, small, *a: lax.cond(ok, big, small, *a),
                jnp.max(gd[:, :, gi, 0]) < HG_MAX_BLOCK_DECAY,
                functools.partial(_hgrn, hc=hc, lc=lc, nchunk=nchunk, within=wsz), hg_scan)
        of, ob = hg_scan(kf, lff, kb, lfb, v, hq)
        xs = _tail(xs, mod4, l, b, y, at, of, ob, g, w_four[l].astype(BF16), w_out[l].astype(BF16), gn,
                   w_gate[l].astype(BF16), w_up[l].astype(BF16), w_down[l].astype(BF16),
                   lc, tm, final_norm if l == depth - 1 else None)
    return xs[:, lc:]
```
